```python
import jax
import jax.numpy as jnp
from jax import lax
import numpy as np

D_MODEL = 1024
BATCH = 16
SEQ = 2048
DEPTH = 2

CONV_K = 4
CHUNK = 64
GDN_QK_HEADS = 4
GDN_V_HEADS = 8
GDN_HEAD_K = 128
GDN_HEAD_V = 128
GDN_QK_DIM = GDN_QK_HEADS * GDN_HEAD_K
GDN_V_DIM = GDN_V_HEADS * GDN_HEAD_V
GDN_CONV_DIM = 2 * GDN_QK_DIM + GDN_V_DIM
SSM_EXPAND = 2
SSM_D_INNER = SSM_EXPAND * D_MODEL
SSM_HEAD_DIM = 64
SSM_HEADS = SSM_D_INNER // SSM_HEAD_DIM
SSM_GROUPS = 4
SSM_STATE = 128
SSM_CONV_DIM = SSM_D_INNER + 2 * SSM_GROUPS * SSM_STATE
IN_WIDTHS = (GDN_CONV_DIM, GDN_V_DIM, GDN_V_HEADS, GDN_V_HEADS, SSM_D_INNER, SSM_CONV_DIM, SSM_HEADS, D_MODEL, D_MODEL)
IN_DIM = sum(IN_WIDTHS)
FFN_DENSE = 2816
N_EXPERTS = 8
TOP_K = 2
FFN_EXPERT = 3584
MOE_BLOCK = 256
N_DENSE_LAYERS = (DEPTH + 1) // 2
N_MOE_LAYERS = DEPTH // 2
NORM_EPS = 1e-6
SSM_NORM_EPS = 1e-5

kernel_name = 'hybrid_gdn_mamba2_moe_block'


def rms_norm(x, w, eps=NORM_EPS):
    x32 = x.astype(jnp.float32)
    y = x32 * lax.rsqrt(jnp.mean(x32 * x32, axis=-1, keepdims=True) + eps)
    return (y * w.astype(jnp.float32)).astype(x.dtype)


def l2_normalize(x, eps=1e-6):
    x32 = x.astype(jnp.float32)
    return x32 * lax.rsqrt(jnp.sum(x32 * x32, axis=-1, keepdims=True) + eps)


def causal_depthwise_conv(x, w):
    k = w.shape[0]
    return lax.conv_general_dilated(x, w[:, None, :].astype(x.dtype), window_strides=(1,), padding=[(k - 1, 0)], dimension_numbers=('NWC', 'WIO', 'NWC'), feature_group_count=x.shape[-1])


def gated_delta_rule_chunked(q, k, v, g, beta):
    bsz, seq, heads, dk = q.shape
    dv = v.shape[-1]
    nc = seq // CHUNK

    def chunk(t):
        t = t.reshape((bsz, nc, CHUNK, heads) + t.shape[3:])
        return jnp.moveaxis(t, 3, 1)

    q = chunk(q * (dk ** -0.5))
    k = chunk(k)
    v = chunk(v)
    beta = chunk(beta)
    g = jnp.cumsum(chunk(g), axis=-1)
    causal = jnp.tril(jnp.ones((CHUNK, CHUNK), bool))
    strict = jnp.tril(jnp.ones((CHUNK, CHUNK), bool), -1)
    decay = jnp.exp(jnp.where(causal, g[..., :, None] - g[..., None, :], -jnp.inf))
    k_beta = k * beta[..., None]
    kk = jnp.where(strict, jnp.einsum('bhnik,bhnjk->bhnij', k_beta, k) * decay, 0.0)
    rhs = jnp.concatenate([v * beta[..., None], k_beta * jnp.exp(g)[..., None]], axis=-1)
    sol = lax.linalg.triangular_solve(kk + jnp.eye(CHUNK, dtype=kk.dtype), rhs, left_side=True, lower=True, unit_diagonal=True)
    u = sol[..., :dv]
    w = sol[..., dv:]
    qk = jnp.einsum('bhnik,bhnjk->bhnij', q, k) * decay

    def step(state, inp):
        q_c, k_c, u_c, w_c, g_c, qk_c = inp
        v_new = u_c - jnp.einsum('bhck,bhkv->bhcv', w_c, state)
        o_c = jnp.einsum('bhck,bhkv->bhcv', q_c * jnp.exp(g_c)[..., None], state) + jnp.einsum('bhij,bhjv->bhiv', qk_c, v_new)
        g_last = g_c[..., -1]
        state = state * jnp.exp(g_last)[..., None, None] + jnp.einsum('bhck,bhcv->bhkv', k_c * jnp.exp(g_last[..., None] - g_c)[..., None], v_new)
        return state, o_c

    xs = tuple(jnp.moveaxis(t, 2, 0) for t in (q, k, u, w, g, qk))
    state0 = jnp.zeros((bsz, heads, dk, dv), jnp.float32)
    _, o = lax.scan(step, state0, xs)
    return jnp.transpose(o, (1, 0, 3, 2, 4)).reshape(bsz, seq, heads, dv)


def segsum_exp(a):
    cs = jnp.cumsum(a, axis=-1)
    n = a.shape[-1]
    causal = jnp.tril(jnp.ones((n, n), bool))
    return jnp.exp(jnp.where(causal, cs[..., :, None] - cs[..., None, :], -jnp.inf))


def ssd_chunked(xdt, adt, bm, cm):
    bsz, seq, heads, p = xdt.shape
    groups, n = bm.shape[2], bm.shape[3]
    r = heads // groups
    nc = seq // CHUNK
    x = xdt.reshape(bsz, nc, CHUNK, groups, r, p)
    a = jnp.transpose(adt.reshape(bsz, nc, CHUNK, groups, r), (0, 3, 4, 1, 2))
    bm = bm.reshape(bsz, nc, CHUNK, groups, n)
    cm = cm.reshape(bsz, nc, CHUNK, groups, n)
    a_cs = jnp.cumsum(a, axis=-1)
    cb = jnp.einsum('bclgn,bcsgn->bcgls', cm, bm)
    lmat = jnp.transpose(segsum_exp(a), (0, 3, 1, 2, 4, 5))
    y_diag = jnp.einsum('bcgrls,bcsgrp->bclgrp', cb[:, :, :, None] * lmat, x)

    def step(h, inp):
        x_c, b_c, c_c, acs_c = inp
        y_off = jnp.einsum('blgn,bgrpn,bgrl->blgrp', c_c, h, jnp.exp(acs_c))
        decay_in = jnp.exp(acs_c[..., -1:] - acs_c)
        h = h * jnp.exp(acs_c[..., -1])[..., None, None] + jnp.einsum('blgn,bgrl,blgrp->bgrpn', b_c, decay_in, x_c)
        return h, y_off

    xs = (jnp.moveaxis(x, 1, 0), jnp.moveaxis(bm, 1, 0), jnp.moveaxis(cm, 1, 0), jnp.moveaxis(a_cs, 3, 0))
    h0 = jnp.zeros((bsz, groups, r, p, n), jnp.float32)
    _, y_off = lax.scan(step, h0, xs)
    y = y_diag + jnp.moveaxis(y_off, 0, 1)
    return y.reshape(bsz, seq, heads, p)


def hybrid_mixer(u, w_in, gdn_conv_w, gdn_A_log, gdn_dt_bias, gdn_norm_w, gdn_proj, ssm_conv_w, ssm_conv_b, ssm_A_log, ssm_dt_bias, ssm_D, ssm_norm_w, ssm_proj, w_out):
    bsz, seq, _ = u.shape
    f32 = jnp.float32
    cuts, acc = [], 0
    for width in IN_WIDTHS[:-1]:
        acc += width
        cuts.append(acc)
    proj = jnp.einsum('btd,de->bte', u, w_in)
    qkv, z_a, a_raw, b_raw, z_b, xbc, dt_raw, gate_a, gate_b = jnp.split(proj, cuts, axis=-1)

    qkv = jax.nn.silu(causal_depthwise_conv(qkv, gdn_conv_w))
    q, k, v = jnp.split(qkv, [GDN_QK_DIM, 2 * GDN_QK_DIM], axis=-1)
    rep = GDN_V_HEADS // GDN_QK_HEADS
    q = jnp.repeat(l2_normalize(q.reshape(bsz, seq, GDN_QK_HEADS, GDN_HEAD_K)), rep, axis=2)
    k = jnp.repeat(l2_normalize(k.reshape(bsz, seq, GDN_QK_HEADS, GDN_HEAD_K)), rep, axis=2)
    v = v.reshape(bsz, seq, GDN_V_HEADS, GDN_HEAD_V).astype(f32)
    beta = jax.nn.sigmoid(b_raw.astype(f32))
    log_decay = -jnp.exp(gdn_A_log.astype(f32)) * jax.nn.softplus(a_raw.astype(f32) + gdn_dt_bias.astype(f32))
    o = gated_delta_rule_chunked(q, k, v, log_decay, beta)
    o = rms_norm(o, gdn_norm_w) * jax.nn.silu(z_a.reshape(bsz, seq, GDN_V_HEADS, GDN_HEAD_V).astype(f32))
    branch_a = jnp.einsum('bte,ed->btd', o.reshape(bsz, seq, GDN_V_DIM).astype(u.dtype), gdn_proj)

    xbc = jax.nn.silu(causal_depthwise_conv(xbc, ssm_conv_w) + ssm_conv_b)
    xs, bm, cm = jnp.split(xbc, [SSM_D_INNER, SSM_D_INNER + SSM_GROUPS * SSM_STATE], axis=-1)
    xs = xs.reshape(bsz, seq, SSM_HEADS, SSM_HEAD_DIM).astype(f32)
    bm = bm.reshape(bsz, seq, SSM_GROUPS, SSM_STATE).astype(f32)
    cm = cm.reshape(bsz, seq, SSM_GROUPS, SSM_STATE).astype(f32)
    dt = jax.nn.softplus(dt_raw.astype(f32) + ssm_dt_bias.astype(f32))
    a = -jnp.exp(ssm_A_log.astype(f32))
    y = ssd_chunked(xs * dt[..., None], a * dt, bm, cm) + xs * ssm_D.astype(f32)[:, None]
    gs = SSM_D_INNER // SSM_GROUPS
    y = y.reshape(bsz, seq, SSM_GROUPS, gs) * jax.nn.silu(z_b.astype(f32)).reshape(bsz, seq, SSM_GROUPS, gs)
    y = rms_norm(y, ssm_norm_w.reshape(SSM_GROUPS, gs), SSM_NORM_EPS)
    branch_b = jnp.einsum('bte,ed->btd', y.reshape(bsz, seq, SSM_D_INNER).astype(u.dtype), ssm_proj)

    mixed = jax.nn.sigmoid(gate_a) * branch_a + jax.nn.sigmoid(gate_b) * branch_b
    return jnp.einsum('btd,de->bte', mixed, w_out)


def swiglu(u, w_gate, w_up, w_down):
    hid = jax.nn.silu(jnp.einsum('btd,df->btf', u, w_gate)) * jnp.einsum('btd,df->btf', u, w_up)
    return jnp.einsum('btf,fd->btd', hid, w_down)


def moe_swiglu(u, router_w, w_gate, w_up, w_down):
    bsz, seq, d = u.shape
    n_tok = bsz * seq
    n_assign = n_tok * TOP_K
    xt = u.reshape(n_tok, d)
    logits = jnp.einsum('nd,de->ne', xt, router_w).astype(jnp.float32)
    top_logits, top_idx = lax.top_k(logits, TOP_K)
    gates = jax.nn.softmax(top_logits, axis=-1)
    flat_e = top_idx.reshape(-1)
    order = jnp.argsort(flat_e)
    sorted_e = flat_e[order]
    sorted_tok = (order // TOP_K).astype(jnp.int32)
    counts = jnp.bincount(flat_e, length=N_EXPERTS)
    padded = (counts + MOE_BLOCK - 1) // MOE_BLOCK * MOE_BLOCK
    start = jnp.cumsum(counts) - counts
    ends = jnp.cumsum(padded)
    pstart = ends - padded
    dest_sorted = (pstart[sorted_e] + jnp.arange(n_assign) - start[sorted_e]).astype(jnp.int32)
    n_blocks = -(-n_assign // MOE_BLOCK) + N_EXPERTS
    n_slots = n_blocks * MOE_BLOCK
    slot_tok = jnp.full((n_slots,), n_tok, jnp.int32).at[dest_sorted].set(sorted_tok)
    x_pad = jnp.concatenate([xt, jnp.zeros((1, d), xt.dtype)], axis=0)
    xb = x_pad[slot_tok].reshape(n_blocks, MOE_BLOCK, d)
    block_e = jnp.minimum(jnp.searchsorted(ends, jnp.arange(n_blocks) * MOE_BLOCK, side='right'), N_EXPERTS - 1)

    def run_block(args):
        xblk, e = args
        hid = jax.nn.silu(xblk @ w_gate[e]) * (xblk @ w_up[e])
        return hid @ w_down[e]

    yb = lax.map(run_block, (xb, block_e)).reshape(n_slots, d)
    dest = jnp.zeros((n_assign,), jnp.int32).at[order].set(dest_sorted)
    y = jnp.einsum('nk,nkd->nd', gates.astype(u.dtype), yb[dest].reshape(n_tok, TOP_K, d))
    return y.reshape(bsz, seq, d)


def setup_inputs(seed: int = 0) -> dict:
    key = jax.random.key(seed)
    ks = iter(jax.random.split(key, 40))
    f32 = jnp.float32

    def nrm(shape, scale):
        return jax.random.normal(next(ks), shape, f32) * scale

    L = DEPTH
    x = nrm((BATCH, SEQ, D_MODEL), 1.0)
    mix_norm_w = 1.0 + nrm((L, D_MODEL), 0.02)
    w_in = nrm((L, D_MODEL, IN_DIM), D_MODEL ** -0.5)
    gdn_conv_w = nrm((L, CONV_K, GDN_CONV_DIM), CONV_K ** -0.5)
    gdn_A_log = jnp.log(jax.random.uniform(next(ks), (L, GDN_V_HEADS), f32, 1.0, 16.0))
    gdn_dt_bias = 1.0 + nrm((L, GDN_V_HEADS), 0.1)
    gdn_norm_w = 1.0 + nrm((L, GDN_HEAD_V), 0.02)
    gdn_proj = nrm((L, GDN_V_DIM, D_MODEL), GDN_V_DIM ** -0.5)
    ssm_conv_w = nrm((L, CONV_K, SSM_CONV_DIM), CONV_K ** -0.5)
    ssm_conv_b = nrm((L, SSM_CONV_DIM), 0.02)
    ssm_A_log = jnp.log(jax.random.uniform(next(ks), (L, SSM_HEADS), f32, 1.0, 16.0))
    dt0 = jnp.exp(jax.random.uniform(next(ks), (L, SSM_HEADS), f32, float(np.log(1e-3)), float(np.log(1e-1))))
    ssm_dt_bias = dt0 + jnp.log(-jnp.expm1(-dt0))
    ssm_D = 1.0 + nrm((L, SSM_HEADS), 0.1)
    ssm_norm_w = 1.0 + nrm((L, SSM_D_INNER), 0.02)
    ssm_proj = nrm((L, SSM_D_INNER, D_MODEL), SSM_D_INNER ** -0.5)
    w_out = nrm((L, D_MODEL, D_MODEL), D_MODEL ** -0.5)
    ffn_norm_w = 1.0 + nrm((L, D_MODEL), 0.02)
    dense_w_gate = nrm((N_DENSE_LAYERS, D_MODEL, FFN_DENSE), D_MODEL ** -0.5)
    dense_w_up = nrm((N_DENSE_LAYERS, D_MODEL, FFN_DENSE), D_MODEL ** -0.5)
    dense_w_down = nrm((N_DENSE_LAYERS, FFN_DENSE, D_MODEL), FFN_DENSE ** -0.5)
    router_w = nrm((N_MOE_LAYERS, D_MODEL, N_EXPERTS), D_MODEL ** -0.5)
    moe_w_gate = nrm((N_MOE_LAYERS, N_EXPERTS, D_MODEL, FFN_EXPERT), D_MODEL ** -0.5)
    moe_w_up = nrm((N_MOE_LAYERS, N_EXPERTS, D_MODEL, FFN_EXPERT), D_MODEL ** -0.5)
    moe_w_down = nrm((N_MOE_LAYERS, N_EXPERTS, FFN_EXPERT, D_MODEL), FFN_EXPERT ** -0.5)
    final_norm_w = 1.0 + nrm((D_MODEL,), 0.02)
    return {'x': x, 'mix_norm_w': mix_norm_w, 'w_in': w_in, 'gdn_conv_w': gdn_conv_w, 'gdn_A_log': gdn_A_log, 'gdn_dt_bias': gdn_dt_bias, 'gdn_norm_w': gdn_norm_w, 'gdn_proj': gdn_proj, 'ssm_conv_w': ssm_conv_w, 'ssm_conv_b': ssm_conv_b, 'ssm_A_log': ssm_A_log, 'ssm_dt_bias': ssm_dt_bias, 'ssm_D': ssm_D, 'ssm_norm_w': ssm_norm_w, 'ssm_proj': ssm_proj, 'w_out': w_out, 'ffn_norm_w': ffn_norm_w, 'dense_w_gate': dense_w_gate, 'dense_w_up': dense_w_up, 'dense_w_down': dense_w_down, 'router_w': router_w, 'moe_w_gate': moe_w_gate, 'moe_w_up': moe_w_up, 'moe_w_down': moe_w_down, 'final_norm_w': final_norm_w}


def reference(x, mix_norm_w, w_in, gdn_conv_w, gdn_A_log, gdn_dt_bias, gdn_norm_w, gdn_proj, ssm_conv_w, ssm_conv_b, ssm_A_log, ssm_dt_bias, ssm_D, ssm_norm_w, ssm_proj, w_out, ffn_norm_w, dense_w_gate, dense_w_up, dense_w_down, router_w, moe_w_gate, moe_w_up, moe_w_down, final_norm_w):
    h = x
    for layer in range(DEPTH):
        u = rms_norm(h, mix_norm_w[layer])
        h = h + hybrid_mixer(u, w_in[layer], gdn_conv_w[layer], gdn_A_log[layer], gdn_dt_bias[layer], gdn_norm_w[layer], gdn_proj[layer], ssm_conv_w[layer], ssm_conv_b[layer], ssm_A_log[layer], ssm_dt_bias[layer], ssm_D[layer], ssm_norm_w[layer], ssm_proj[layer], w_out[layer])
        u = rms_norm(h, ffn_norm_w[layer])
        j = layer // 2
        if layer % 2 == 0:
            h = h + swiglu(u, dense_w_gate[j], dense_w_up[j], dense_w_down[j])
        else:
            h = h + moe_swiglu(u, router_w[j], moe_w_gate[j], moe_w_up[j], moe_w_down[j])
    return rms_norm(h, final_norm_w)
```

```python
import functools

import jax
import jax.numpy as jnp
from jax import lax
from jax.experimental import pallas as pl
from jax.experimental.pallas import tpu as pltpu

F32 = jnp.float32
BF16 = jnp.bfloat16

D_MODEL = 1024
CONV_K = 4
CHUNK = 64
GDN_QK_HEADS = 4
GDN_V_HEADS = 8
GDN_HEAD = 128
GDN_QK_DIM = GDN_QK_HEADS * GDN_HEAD
GDN_V_DIM = GDN_V_HEADS * GDN_HEAD
GDN_CONV_DIM = 2 * GDN_QK_DIM + GDN_V_DIM
SSM_D_INNER = 2048
SSM_HEAD_DIM = 64
SSM_HEADS = SSM_D_INNER // SSM_HEAD_DIM
SSM_GROUPS = 4
SSM_HPG = SSM_HEADS // SSM_GROUPS
SSM_GROUP_DIM = SSM_D_INNER // SSM_GROUPS
SSM_STATE = 128
SSM_CONV_DIM = SSM_D_INNER + 2 * SSM_GROUPS * SSM_STATE
N_EXPERTS = 8
TOP_K = 2
MOE_BLOCK = 256
NORM_EPS = 1e-6
SSM_NORM_EPS = 1e-5
CONV_HALO = 16

VMEM_LIMIT = 56 * 1024 * 1024


def _cparams(sem):
    return pltpu.CompilerParams(dimension_semantics=sem, vmem_limit_bytes=VMEM_LIMIT)


def _silu(x):
    return x * jax.nn.sigmoid(x)


def _softplus(x):
    return jnp.maximum(x, 0.0) + jnp.log1p(jnp.exp(-jnp.abs(x)))


def _dot(a, b):
    return jnp.dot(a.astype(BF16), b.astype(BF16), preferred_element_type=F32)


def _dot_nt(a, b):
    return lax.dot_general(a.astype(BF16), b.astype(BF16), (((1,), (1,)), ((), ())),
                           preferred_element_type=F32)


def _dot_tn(a, b):
    return lax.dot_general(a.astype(BF16), b.astype(BF16), (((0,), (0,)), ((), ())),
                           preferred_element_type=F32)


def _dot_hi(a, b):
    return jnp.dot(a, b, preferred_element_type=F32, precision=lax.Precision.HIGHEST)


def _rmsnorm_kernel(x_ref, w_ref, o_ref):
    x = x_ref[...]
    y = x * lax.rsqrt(jnp.mean(x * x, axis=-1, keepdims=True) + NORM_EPS)
    o_ref[...] = (y * w_ref[...]).astype(o_ref.dtype)


def rmsnorm(x, w, out_dtype, tm=1024):
    n, d = x.shape
    tm = min(tm, n)
    return pl.pallas_call(
        _rmsnorm_kernel,
        grid=(n // tm,),
        in_specs=[pl.BlockSpec((tm, d), lambda i: (i, 0)), pl.BlockSpec((1, d), lambda i: (0, 0))],
        out_specs=pl.BlockSpec((tm, d), lambda i: (i, 0)),
        out_shape=jax.ShapeDtypeStruct((n, d), out_dtype),
        compiler_params=_cparams(("parallel",)),
        name="rmsnorm",
    )(x, w.reshape(1, d))


def _matmul_kernel(x_ref, w_ref, o_ref, *, hi):
    if hi:
        o_ref[...] = _dot_hi(x_ref[...].astype(F32), w_ref[...]).astype(o_ref.dtype)
    else:
        o_ref[...] = jnp.dot(x_ref[...], w_ref[...], preferred_element_type=F32).astype(o_ref.dtype)


def matmul(x, w, out_dtype, tm=1024, tn=1024, hi=False, name="matmul"):
    n, k = x.shape
    m = w.shape[1]
    tm = min(tm, n)
    tn = min(tn, m)
    return pl.pallas_call(
        functools.partial(_matmul_kernel, hi=hi),
        grid=(m // tn, n // tm),
        in_specs=[pl.BlockSpec((tm, k), lambda j, i: (i, 0)), pl.BlockSpec((k, tn), lambda j, i: (0, j))],
        out_specs=pl.BlockSpec((tm, tn), lambda j, i: (i, j)),
        out_shape=jax.ShapeDtypeStruct((n, m), out_dtype),
        compiler_params=_cparams(("parallel", "parallel")),
        name=name,
    )(x, w)


def _causal_conv(prev_ref, cur_ref, w_ref, first, buf_ref):
    tt = cur_ref.shape[0]
    prev = prev_ref[...].astype(F32)
    buf_ref[0:CONV_HALO, :] = jnp.where(first, 0.0, prev)
    buf_ref[CONV_HALO:CONV_HALO + tt, :] = cur_ref[...].astype(F32)
    acc = None
    for j in range(CONV_K):
        off = CONV_HALO - (CONV_K - 1) + j
        term = buf_ref[off:off + tt, :] * w_ref[j:j + 1, :]
        acc = term if acc is None else acc + term
    return acc


def _col_from_row(row, eye):
    return jnp.sum(jnp.where(eye, row, 0.0), axis=-1, keepdims=True)


def _gdn_kernel(alog_ref, dtb_ref,
                qp_ref, q_ref, kp_ref, k_ref, vp_ref, v_ref, z_ref,
                wq_ref, wk_ref, wv_ref, a_ref, b_ref, nw_ref,
                o_ref,
                s_ref, qbuf, kbuf, vbuf, *, tt):
    h = pl.program_id(1)
    t = pl.program_id(2)
    first = t == 0
    nck = tt // CHUNK

    @pl.when(first)
    def _():
        s_ref[...] = jnp.zeros_like(s_ref)

    q = _silu(_causal_conv(qp_ref, q_ref, wq_ref, first, qbuf))
    k = _silu(_causal_conv(kp_ref, k_ref, wk_ref, first, kbuf))
    v = _silu(_causal_conv(vp_ref, v_ref, wv_ref, first, vbuf))
    q = q * (lax.rsqrt(jnp.sum(q * q, axis=-1, keepdims=True) + 1e-6) * (GDN_HEAD ** -0.5))
    k = k * lax.rsqrt(jnp.sum(k * k, axis=-1, keepdims=True) + 1e-6)

    neg_a = -jnp.exp(jnp.full((1, CHUNK), alog_ref[h], F32))
    g_rows = neg_a * _softplus(a_ref[...] + dtb_ref[h])
    beta_rows = jax.nn.sigmoid(b_ref[...])
    ri = lax.broadcasted_iota(jnp.int32, (CHUNK, CHUNK), 0)
    ci = lax.broadcasted_iota(jnp.int32, (CHUNK, CHUNK), 1)
    upper = (ri <= ci).astype(F32)
    gc_rows = _dot_hi(g_rows, upper)
    eye = ri == ci
    causal = ri >= ci
    strict = ri > ci

    s = s_ref[...]
    outs = []
    for c in range(nck):
        sl = slice(c * CHUNK, (c + 1) * CHUNK)
        qc, kc, vc = q[sl], k[sl], v[sl]
        gc_row = gc_rows[c:c + 1, :]
        gc_col = _col_from_row(gc_row, eye)
        beta_col = _col_from_row(beta_rows[c:c + 1, :], eye)
        g_last = jnp.sum(g_rows[c:c + 1, :], axis=-1, keepdims=True)
        decay = jnp.where(causal, jnp.exp(gc_col - gc_row), 0.0)
        eg_col = jnp.exp(gc_col)
        kq = _dot_nt(jnp.concatenate([kc, qc], axis=0), kc)
        kk = kq[:CHUNK]
        qk = kq[CHUNK:] * decay
        kb = kc * beta_col
        a = jnp.where(strict, kk * decay, 0.0) * beta_col
        x = jnp.concatenate([vc * beta_col, kb * eg_col], axis=1)
        p = a
        sign = -1.0
        nsteps = CHUNK.bit_length() - 1
        for step in range(nsteps):
            if step + 1 < nsteps:
                r = _dot(p, jnp.concatenate([x, p], axis=1))
                x = x + sign * r[:, :2 * GDN_HEAD]
                p = r[:, 2 * GDN_HEAD:]
            else:
                x = x + sign * _dot(p, x)
            sign = 1.0
        u = x[:, :GDN_HEAD]
        w = x[:, GDN_HEAD:]
        ws = _dot(jnp.concatenate([w, qc * eg_col], axis=0), s)
        v_new = u - ws[:CHUNK]
        o = ws[CHUNK:] + _dot(qk, v_new)
        k_dec = kc * jnp.exp(g_last - gc_col)
        s = s * jnp.exp(g_last) + _dot_tn(k_dec, v_new)
        outs.append(o)
    s_ref[...] = s
    o = jnp.concatenate(outs, axis=0)
    y = o * lax.rsqrt(jnp.mean(o * o, axis=-1, keepdims=True) + NORM_EPS) * nw_ref[...]
    o_ref[...] = (y * _silu(z_ref[...].astype(F32))).astype(o_ref.dtype)


def gdn_mixer(qkv_raw, z_a, a4, b4, conv_w, a_log, dt_bias, norm_w, *, bsz, seq, tt=256):
    n = bsz * seq
    nt = seq // tt
    hb = tt // CONV_HALO
    rep = GDN_V_HEADS // GDN_QK_HEADS
    nqk = GDN_QK_HEADS

    def cur(colf):
        return pl.BlockSpec((tt, GDN_HEAD), lambda b, h, t, *_: (b * nt + t, colf(h)))

    def prev(colf):
        return pl.BlockSpec((CONV_HALO, GDN_HEAD),
                            lambda b, h, t, *_: (jnp.maximum((b * nt + t) * hb - 1, 0), colf(h)))

    def wspec(colf):
        return pl.BlockSpec((CONV_K, GDN_HEAD), lambda b, h, t, *_: (0, colf(h)))

    qcol = lambda h: h // rep
    kcol = lambda h: nqk + h // rep
    vcol = lambda h: 2 * nqk + h
    small = pl.BlockSpec((None, None, None, tt // CHUNK, CHUNK), lambda b, h, t, *_: (b, h, t, 0, 0))
    a4 = a4.reshape(bsz, GDN_V_HEADS, nt, tt // CHUNK, CHUNK)
    b4 = b4.reshape(bsz, GDN_V_HEADS, nt, tt // CHUNK, CHUNK)
    grid_spec = pltpu.PrefetchScalarGridSpec(
        num_scalar_prefetch=2,
        grid=(bsz, GDN_V_HEADS, nt),
        in_specs=[prev(qcol), cur(qcol), prev(kcol), cur(kcol), prev(vcol), cur(vcol),
                  pl.BlockSpec((tt, GDN_HEAD), lambda b, h, t, *_: (b * nt + t, h)),
                  wspec(qcol), wspec(kcol), wspec(vcol), small, small,
                  pl.BlockSpec((1, GDN_HEAD), lambda b, h, t, *_: (0, 0))],
        out_specs=pl.BlockSpec((tt, GDN_HEAD), lambda b, h, t, *_: (b * nt + t, h)),
        scratch_shapes=[pltpu.VMEM((GDN_HEAD, GDN_HEAD), F32)]
        + [pltpu.VMEM((tt + CONV_HALO, GDN_HEAD), F32)] * 3,
    )
    return pl.pallas_call(
        functools.partial(_gdn_kernel, tt=tt),
        grid_spec=grid_spec,
        out_shape=jax.ShapeDtypeStruct((n, GDN_V_DIM), BF16),
        compiler_params=_cparams(("parallel", "parallel", "arbitrary")),
        name="gdn_mixer",
    )(a_log, dt_bias, qkv_raw, qkv_raw, qkv_raw, qkv_raw, qkv_raw, qkv_raw, z_a,
      conv_w, conv_w, conv_w, a4, b4, norm_w.reshape(1, GDN_HEAD))


def _ssd_kernel(alog_ref, dtb_ref, dskip_ref,
                xp_ref, x_ref, bp_ref, b_ref, cp_ref, c_ref, z_ref,
                wx_ref, wb_ref, wc_ref, bx_ref, bb_ref, bc_ref, dt_ref, nw_ref,
                o_ref,
                h_ref, xbuf, bbuf, cbuf, *, tt):
    g = pl.program_id(1)
    t = pl.program_id(2)
    first = t == 0
    nck = tt // CHUNK
    p_dim = SSM_HEAD_DIM

    @pl.when(first)
    def _():
        h_ref[...] = jnp.zeros_like(h_ref)

    xs = _silu(_causal_conv(xp_ref, x_ref, wx_ref, first, xbuf) + bx_ref[...])
    bm = _silu(_causal_conv(bp_ref, b_ref, wb_ref, first, bbuf) + bb_ref[...])
    cm = _silu(_causal_conv(cp_ref, c_ref, wc_ref, first, cbuf) + bc_ref[...])

    ri = lax.broadcasted_iota(jnp.int32, (CHUNK, CHUNK), 0)
    ci = lax.broadcasted_iota(jnp.int32, (CHUNK, CHUNK), 1)
    upper = (ri <= ci).astype(F32)
    eye = ri == ci
    causal = ri >= ci

    chunks = [slice(c * CHUNK, (c + 1) * CHUNK) for c in range(nck)]
    cbs = [_dot_nt(cm[sl], bm[sl]) for sl in chunks]
    y_heads = []
    for hh in range(SSM_HPG):
        head = g * SSM_HPG + hh
        dt_rows = _softplus(dt_ref[hh] + dtb_ref[head])
        adt_rows = -jnp.exp(jnp.full((1, CHUNK), alog_ref[head], F32)) * dt_rows
        acs_rows = _dot_hi(adt_rows, upper)
        xh = xs[:, hh * p_dim:(hh + 1) * p_dim]
        hstate = h_ref[hh]
        ys = []
        for c, sl in enumerate(chunks):
            acs_row = acs_rows[c:c + 1, :]
            acs_col = _col_from_row(acs_row, eye)
            dt_col = _col_from_row(dt_rows[c:c + 1, :], eye)
            a_last = jnp.sum(adt_rows[c:c + 1, :], axis=-1, keepdims=True)
            lmat = jnp.where(causal, jnp.exp(acs_col - acs_row), 0.0)
            xdt = xh[sl] * dt_col
            y = _dot(cbs[c] * lmat, xdt) + _dot_nt(cm[sl], hstate) * jnp.exp(acs_col)
            hstate = hstate * jnp.exp(a_last) + _dot_tn(xdt * jnp.exp(a_last - acs_col), bm[sl])
            ys.append(y)
        h_ref[hh] = hstate
        y_heads.append(jnp.concatenate(ys, axis=0) + xh * dskip_ref[head])
    y = jnp.concatenate(y_heads, axis=1)
    y = y * _silu(z_ref[...].astype(F32))
    y = y * lax.rsqrt(jnp.mean(y * y, axis=-1, keepdims=True) + SSM_NORM_EPS) * nw_ref[...]
    o_ref[...] = y.astype(o_ref.dtype)


def ssd_mixer(xbc_raw, z_b, dt4, conv_w, conv_b, a_log, dt_bias, d_skip, norm_w, *, bsz, seq, tt=256):
    n = bsz * seq
    nt = seq // tt
    hb = tt // CONV_HALO
    gd = SSM_GROUP_DIM
    st = SSM_STATE
    x_blocks = SSM_D_INNER // st

    def cur(width, colf):
        return pl.BlockSpec((tt, width), lambda b, g, t, *_: (b * nt + t, colf(g)))

    def prev(width, colf):
        return pl.BlockSpec((CONV_HALO, width),
                            lambda b, g, t, *_: (jnp.maximum((b * nt + t) * hb - 1, 0), colf(g)))

    def rowspec(rows, width, colf):
        return pl.BlockSpec((rows, width), lambda b, g, t, *_: (0, colf(g)))

    xcol = lambda g: g
    bcol = lambda g: x_blocks + g
    ccol = lambda g: x_blocks + SSM_GROUPS + g
    grid_spec = pltpu.PrefetchScalarGridSpec(
        num_scalar_prefetch=3,
        grid=(bsz, SSM_GROUPS, nt),
        in_specs=[prev(gd, xcol), cur(gd, xcol), prev(st, bcol), cur(st, bcol), prev(st, ccol), cur(st, ccol),
                  cur(gd, xcol),
                  rowspec(CONV_K, gd, xcol), rowspec(CONV_K, st, bcol), rowspec(CONV_K, st, ccol),
                  rowspec(1, gd, xcol), rowspec(1, st, bcol), rowspec(1, st, ccol),
                  pl.BlockSpec((None, None, SSM_HPG, tt // CHUNK, CHUNK), lambda b, g, t, *_: (b, t, g, 0, 0)),
                  rowspec(1, gd, xcol)],
        out_specs=cur(gd, xcol),
        scratch_shapes=[pltpu.VMEM((SSM_HPG, SSM_HEAD_DIM, SSM_STATE), F32),
                        pltpu.VMEM((tt + CONV_HALO, gd), F32),
                        pltpu.VMEM((tt + CONV_HALO, st), F32),
                        pltpu.VMEM((tt + CONV_HALO, st), F32)],
    )
    cb2 = conv_b.reshape(1, SSM_CONV_DIM)
    dt4 = jnp.transpose(dt4.reshape(bsz, SSM_HEADS, nt, tt // CHUNK, CHUNK), (0, 2, 1, 3, 4))
    return pl.pallas_call(
        functools.partial(_ssd_kernel, tt=tt),
        grid_spec=grid_spec,
        out_shape=jax.ShapeDtypeStruct((n, SSM_D_INNER), BF16),
        compiler_params=_cparams(("parallel", "parallel", "arbitrary")),
        name="ssd_mixer",
    )(a_log, dt_bias, d_skip, xbc_raw, xbc_raw, xbc_raw, xbc_raw, xbc_raw, xbc_raw, z_b,
      conv_w, conv_w, conv_w, cb2, cb2, cb2, dt4, norm_w.reshape(1, SSM_D_INNER))


def _merge_kernel(h_ref, oa_ref, ob_ref, ga_ref, gb_ref, wa_ref, wb_ref, wo_ref, nw_ref, hn_ref, u_ref):
    a = jnp.dot(oa_ref[...], wa_ref[...], preferred_element_type=F32)
    b = jnp.dot(ob_ref[...], wb_ref[...], preferred_element_type=F32)
    mixed = jax.nn.sigmoid(ga_ref[...].astype(F32)) * a + jax.nn.sigmoid(gb_ref[...].astype(F32)) * b
    hn = h_ref[...] + jnp.dot(mixed.astype(BF16), wo_ref[...], preferred_element_type=F32)
    hn_ref[...] = hn
    y = hn * lax.rsqrt(jnp.mean(hn * hn, axis=-1, keepdims=True) + NORM_EPS)
    u_ref[...] = (y * nw_ref[...]).astype(u_ref.dtype)


def merge_out(h, oa, ob, ga, gb, wa, wb, wo, next_norm_w, tm=512):
    n, d = h.shape
    tm = min(tm, n)
    row = lambda width: pl.BlockSpec((tm, width), lambda i: (i, 0))
    full = lambda r, c: pl.BlockSpec((r, c), lambda i: (0, 0))
    return pl.pallas_call(
        _merge_kernel,
        grid=(n // tm,),
        in_specs=[row(d), row(GDN_V_DIM), row(SSM_D_INNER), row(d), row(d),
                  full(GDN_V_DIM, d), full(SSM_D_INNER, d), full(d, d), full(1, d)],
        out_specs=[row(d), row(d)],
        out_shape=[jax.ShapeDtypeStruct((n, d), F32), jax.ShapeDtypeStruct((n, d), BF16)],
        compiler_params=_cparams(("parallel",)),
        name="merge_out",
    )(h, oa, ob, ga, gb, wa, wb, wo, next_norm_w.reshape(1, d))


def _ffn_kernel(h_ref, u_ref, wg_ref, wu_ref, wd_ref, nw_ref, hn_ref, un_ref, acc_ref):
    f = pl.program_id(1)

    @pl.when(f == 0)
    def _():
        acc_ref[...] = h_ref[...]

    u = u_ref[...]
    hid = _silu(jnp.dot(u, wg_ref[...], preferred_element_type=F32)) * jnp.dot(
        u, wu_ref[...], preferred_element_type=F32)
    acc_ref[...] += jnp.dot(hid.astype(BF16), wd_ref[...], preferred_element_type=F32)

    @pl.when(f == pl.num_programs(1) - 1)
    def _():
        hn = acc_ref[...]
        hn_ref[...] = hn
        y = hn * lax.rsqrt(jnp.mean(hn * hn, axis=-1, keepdims=True) + NORM_EPS)
        un_ref[...] = (y * nw_ref[...]).astype(un_ref.dtype)


def dense_ffn(h, u, wg, wu, wd, next_norm_w, tm=1024, tf=256):
    n, d = h.shape
    ff = wg.shape[1]
    tm = min(tm, n)
    return pl.pallas_call(
        _ffn_kernel,
        grid=(n // tm, ff // tf),
        in_specs=[pl.BlockSpec((tm, d), lambda i, f: (i, 0)), pl.BlockSpec((tm, d), lambda i, f: (i, 0)),
                  pl.BlockSpec((d, tf), lambda i, f: (0, f)), pl.BlockSpec((d, tf), lambda i, f: (0, f)),
                  pl.BlockSpec((tf, d), lambda i, f: (f, 0)), pl.BlockSpec((1, d), lambda i, f: (0, 0))],
        out_specs=[pl.BlockSpec((tm, d), lambda i, f: (i, 0)), pl.BlockSpec((tm, d), lambda i, f: (i, 0))],
        out_shape=[jax.ShapeDtypeStruct((n, d), F32), jax.ShapeDtypeStruct((n, d), BF16)],
        scratch_shapes=[pltpu.VMEM((tm, d), F32)],
        compiler_params=_cparams(("parallel", "arbitrary")),
        name="dense_ffn",
    )(h, u, wg, wu, wd, next_norm_w.reshape(1, d))


def _router_kernel(u_ref, w_ref, o_ref):
    logits = _dot_hi(u_ref[...].astype(F32), w_ref[...])
    lane = lax.broadcasted_iota(jnp.int32, logits.shape, 1)
    neg = jnp.float32(-3.0e38)
    logits = jnp.where(lane < N_EXPERTS, logits, neg)
    m1 = jnp.max(logits, axis=-1, keepdims=True)
    i1 = jnp.min(jnp.where(logits == m1, lane, 2 * N_EXPERTS), axis=-1, keepdims=True)
    rest = jnp.where(lane == i1, neg, logits)
    m2 = jnp.max(rest, axis=-1, keepdims=True)
    i2 = jnp.min(jnp.where(rest == m2, lane, 2 * N_EXPERTS), axis=-1, keepdims=True)
    e2 = jnp.exp(m2 - m1)
    g1 = 1.0 / (1.0 + e2)
    g2 = e2 / (1.0 + e2)
    out = jnp.where(lane == 0, i1.astype(F32), 0.0)
    out = jnp.where(lane == 1, i2.astype(F32), out)
    out = jnp.where(lane == 2, g1, out)
    out = jnp.where(lane == 3, g2, out)
    o_ref[...] = out


def router(u, w_pad, tm=1024):
    n, d = u.shape
    tm = min(tm, n)
    return pl.pallas_call(
        _router_kernel,
        grid=(n // tm,),
        in_specs=[pl.BlockSpec((tm, d), lambda i: (i, 0)), pl.BlockSpec((d, 128), lambda i: (0, 0))],
        out_specs=pl.BlockSpec((tm, 128), lambda i: (i, 0)),
        out_shape=jax.ShapeDtypeStruct((n, 128), F32),
        compiler_params=_cparams(("parallel",)),
        name="router",
    )(u, w_pad)


def _moe_up_kernel(be_ref, x_ref, wg_ref, wu_ref, o_ref):
    x = x_ref[...]
    hid = _silu(jnp.dot(x, wg_ref[...], preferred_element_type=F32)) * jnp.dot(
        x, wu_ref[...], preferred_element_type=F32)
    o_ref[...] = hid.astype(o_ref.dtype)


def moe_up(block_e, xb, wg, wu, tf=512):
    ns, d = xb.shape
    ff = wg.shape[2]
    nb = ns // MOE_BLOCK
    grid_spec = pltpu.PrefetchScalarGridSpec(
        num_scalar_prefetch=1,
        grid=(ff // tf, nb),
        in_specs=[pl.BlockSpec((MOE_BLOCK, d), lambda f, i, be: (i, 0)),
                  pl.BlockSpec((None, d, tf), lambda f, i, be: (be[i], 0, f)),
                  pl.BlockSpec((None, d, tf), lambda f, i, be: (be[i], 0, f))],
        out_specs=pl.BlockSpec((MOE_BLOCK, tf), lambda f, i, be: (i, f)),
    )
    return pl.pallas_call(
        _moe_up_kernel,
        grid_spec=grid_spec,
        out_shape=jax.ShapeDtypeStruct((ns, ff), BF16),
        compiler_params=_cparams(("parallel", "arbitrary")),
        name="moe_up",
    )(block_e, xb, wg, wu)


def _moe_down_kernel(be_ref, hid_ref, wd_ref, gate_ref, o_ref):
    y = jnp.dot(hid_ref[...], wd_ref[...], preferred_element_type=F32)
    o_ref[...] = (y * gate_ref[...]).astype(o_ref.dtype)


def moe_down(block_e, hid, wd, slot_gate):
    ns, ff = hid.shape
    d = wd.shape[2]
    nb = ns // MOE_BLOCK
    grid_spec = pltpu.PrefetchScalarGridSpec(
        num_scalar_prefetch=1,
        grid=(nb,),
        in_specs=[pl.BlockSpec((MOE_BLOCK, ff), lambda i, be: (i, 0)),
                  pl.BlockSpec((None, ff, d), lambda i, be: (be[i], 0, 0)),
                  pl.BlockSpec((MOE_BLOCK, 1), lambda i, be: (i, 0))],
        out_specs=pl.BlockSpec((MOE_BLOCK, d), lambda i, be: (i, 0)),
    )
    return pl.pallas_call(
        _moe_down_kernel,
        grid_spec=grid_spec,
        out_shape=jax.ShapeDtypeStruct((ns, d), F32),
        compiler_params=_cparams(("arbitrary",)),
        name="moe_down",
    )(block_e, hid, wd, slot_gate)


def _final_kernel(h_ref, y0_ref, y1_ref, nw_ref, o_ref):
    hn = h_ref[...] + y0_ref[...] + y1_ref[...]
    y = hn * lax.rsqrt(jnp.mean(hn * hn, axis=-1, keepdims=True) + NORM_EPS)
    o_ref[...] = y * nw_ref[...]


def final_combine(h, y0, y1, norm_w, tm=1024):
    n, d = h.shape
    tm = min(tm, n)
    row = pl.BlockSpec((tm, d), lambda i: (i, 0))
    return pl.pallas_call(
        _final_kernel,
        grid=(n // tm,),
        in_specs=[row, row, row, pl.BlockSpec((1, d), lambda i: (0, 0))],
        out_specs=row,
        out_shape=jax.ShapeDtypeStruct((n, d), F32),
        compiler_params=_cparams(("parallel",)),
        name="final_combine",
    )(h, y0, y1, norm_w.reshape(1, d))


def _chunk_rows(x, bsz, seq):
    hh = x.shape[1]
    return jnp.transpose(x.reshape(bsz, seq, hh), (0, 2, 1)).reshape(bsz, hh, seq // CHUNK, CHUNK)


def hybrid_mixer_layer(h, u, bsz, seq, w_in, gdn_conv_w, gdn_A_log, gdn_dt_bias, gdn_norm_w, gdn_proj,
                       ssm_conv_w, ssm_conv_b, ssm_A_log, ssm_dt_bias, ssm_D, ssm_norm_w, ssm_proj, w_out,
                       next_norm_w):
    c0 = 0
    c1 = c0 + GDN_CONV_DIM
    c2 = c1 + GDN_V_DIM
    c3 = c2 + GDN_V_HEADS
    c4 = c3 + GDN_V_HEADS
    c5 = c4 + SSM_D_INNER
    c6 = c5 + SSM_CONV_DIM
    c7 = c6 + SSM_HEADS
    c8 = c7 + D_MODEL
    wb = w_in.astype(BF16)
    qkv_raw = matmul(u, wb[:, c0:c1], BF16, name="proj_qkv")
    z_a = matmul(u, wb[:, c1:c2], BF16, name="proj_za")
    z_b = matmul(u, wb[:, c4:c5], BF16, name="proj_zb")
    xbc_raw = matmul(u, wb[:, c5:c6], BF16, name="proj_xbc")
    gate_a = matmul(u, wb[:, c7:c8], BF16, name="proj_ga")
    gate_b = matmul(u, wb[:, c8:], BF16, name="proj_gb")
    n_small = 2 * GDN_V_HEADS + SSM_HEADS
    w_small = jnp.concatenate([w_in[:, c2:c4], w_in[:, c6:c7]], axis=1)
    w_small = jnp.pad(w_small, ((0, 0), (0, 128 - n_small)))
    small = matmul(u, w_small, F32, tn=128, hi=True, name="proj_small")
    a4 = _chunk_rows(small[:, :GDN_V_HEADS], bsz, seq)
    b4 = _chunk_rows(small[:, GDN_V_HEADS:2 * GDN_V_HEADS], bsz, seq)
    dt4 = _chunk_rows(small[:, 2 * GDN_V_HEADS:n_small], bsz, seq)

    oa = gdn_mixer(qkv_raw, z_a, a4, b4, gdn_conv_w, gdn_A_log, gdn_dt_bias, gdn_norm_w, bsz=bsz, seq=seq)
    ob = ssd_mixer(xbc_raw, z_b, dt4, ssm_conv_w, ssm_conv_b, ssm_A_log, ssm_dt_bias, ssm_D, ssm_norm_w,
                   bsz=bsz, seq=seq)
    return merge_out(h, oa, ob, gate_a, gate_b, gdn_proj.astype(BF16), ssm_proj.astype(BF16),
                     w_out.astype(BF16), next_norm_w)


def moe_layer(h, u, router_w, w_gate, w_up, w_down, final_norm_w):
    n, d = h.shape
    n_assign = n * TOP_K
    r = router(u, jnp.pad(router_w, ((0, 0), (0, 128 - N_EXPERTS))))
    top_idx = r[:, :TOP_K].astype(jnp.int32)
    gates = r[:, TOP_K:2 * TOP_K]
    flat_e = top_idx.reshape(-1)
    onehot = (flat_e[:, None] == jnp.arange(N_EXPERTS)[None, :]).astype(jnp.int32)
    csum = jnp.cumsum(onehot, axis=0)
    counts = csum[-1]
    rank = jnp.sum((csum - onehot) * onehot, axis=1)
    padded = (counts + MOE_BLOCK - 1) // MOE_BLOCK * MOE_BLOCK
    ends = jnp.cumsum(padded)
    pstart = ends - padded
    dest = (pstart[flat_e] + rank).astype(jnp.int32)
    n_blocks = -(-n_assign // MOE_BLOCK) + N_EXPERTS
    n_slots = n_blocks * MOE_BLOCK
    tok = (jnp.arange(n_assign, dtype=jnp.int32) // TOP_K)
    slot_tok = jnp.zeros((n_slots,), jnp.int32).at[dest].set(tok)
    slot_gate = jnp.zeros((n_slots,), F32).at[dest].set(gates.reshape(-1))
    block_e = jnp.minimum(jnp.searchsorted(ends, jnp.arange(n_blocks) * MOE_BLOCK, side='right'),
                          N_EXPERTS - 1).astype(jnp.int32)
    xb = u[slot_tok]
    hid = moe_up(block_e, xb, w_gate.astype(BF16), w_up.astype(BF16))
    yb = moe_down(block_e, hid, w_down.astype(BF16), slot_gate.reshape(n_slots, 1))
    yk = yb[dest].reshape(n, TOP_K, d)
    return final_combine(h, yk[:, 0], yk[:, 1], final_norm_w)


def kernel(x, mix_norm_w, w_in, gdn_conv_w, gdn_A_log, gdn_dt_bias, gdn_norm_w, gdn_proj, ssm_conv_w, ssm_conv_b, ssm_A_log, ssm_dt_bias, ssm_D, ssm_norm_w, ssm_proj, w_out, ffn_norm_w, dense_w_gate, dense_w_up, dense_w_down, router_w, moe_w_gate, moe_w_up, moe_w_down, final_norm_w):
    bsz, seq, d = x.shape
    assert d == D_MODEL and w_in.shape[0] == 2, "dense-FFN layer followed by a final MoE layer"
    h = x.reshape(bsz * seq, d)
    u = rmsnorm(h, mix_norm_w[0], BF16)

    def mixer(layer, h, u):
        return hybrid_mixer_layer(
            h, u, bsz, seq, w_in[layer], gdn_conv_w[layer], gdn_A_log[layer], gdn_dt_bias[layer],
            gdn_norm_w[layer], gdn_proj[layer], ssm_conv_w[layer], ssm_conv_b[layer], ssm_A_log[layer],
            ssm_dt_bias[layer], ssm_D[layer], ssm_norm_w[layer], ssm_proj[layer], w_out[layer],
            ffn_norm_w[layer])

    h, u = mixer(0, h, u)
    h, u = dense_ffn(h, u, dense_w_gate[0].astype(BF16), dense_w_up[0].astype(BF16),
                     dense_w_down[0].astype(BF16), mix_norm_w[1])
    h, u = mixer(1, h, u)
    out = moe_layer(h, u, router_w[0], moe_w_gate[0], moe_w_up[0], moe_w_down[0], final_norm_w)
    return out.reshape(bsz, seq, d)
```

```python
import functools

import jax
import jax.numpy as jnp
from jax import lax
from jax.experimental import pallas as pl
from jax.experimental.pallas import tpu as pltpu

F32 = jnp.float32
BF16 = jnp.bfloat16

D_MODEL = 1024
CONV_K = 4
CHUNK = 64
GDN_QK_HEADS = 4
GDN_V_HEADS = 8
GDN_HEAD = 128
GDN_QK_DIM = GDN_QK_HEADS * GDN_HEAD
GDN_V_DIM = GDN_V_HEADS * GDN_HEAD
GDN_CONV_DIM = 2 * GDN_QK_DIM + GDN_V_DIM
SSM_D_INNER = 2048
SSM_HEAD_DIM = 64
SSM_HEADS = SSM_D_INNER // SSM_HEAD_DIM
SSM_GROUPS = 4
SSM_HPG = SSM_HEADS // SSM_GROUPS
SSM_GROUP_DIM = SSM_D_INNER // SSM_GROUPS
SSM_STATE = 128
SSM_CONV_DIM = SSM_D_INNER + 2 * SSM_GROUPS * SSM_STATE
N_EXPERTS = 8
TOP_K = 2
MOE_BLOCK = 256
NORM_EPS = 1e-6
SSM_NORM_EPS = 1e-5
CONV_HALO = 16

VMEM_LIMIT = 56 * 1024 * 1024


def _cparams(sem):
    return pltpu.CompilerParams(dimension_semantics=sem, vmem_limit_bytes=VMEM_LIMIT)


def _silu(x):
    return x * jax.nn.sigmoid(x)


def _softplus(x):
    return jnp.maximum(x, 0.0) + jnp.log1p(jnp.exp(-jnp.abs(x)))


def _dot(a, b):
    return jnp.dot(a.astype(BF16), b.astype(BF16), preferred_element_type=F32)


def _dot_nt(a, b):
    return lax.dot_general(a.astype(BF16), b.astype(BF16), (((1,), (1,)), ((), ())),
                           preferred_element_type=F32)


def _dot_tn(a, b):
    return lax.dot_general(a.astype(BF16), b.astype(BF16), (((0,), (0,)), ((), ())),
                           preferred_element_type=F32)


def _dot_hi(a, b):
    return jnp.dot(a, b, preferred_element_type=F32, precision=lax.Precision.HIGHEST)


def _rmsnorm_kernel(x_ref, w_ref, o_ref):
    x = x_ref[...]
    y = x * lax.rsqrt(jnp.mean(x * x, axis=-1, keepdims=True) + NORM_EPS)
    o_ref[...] = (y * w_ref[...]).astype(o_ref.dtype)


def rmsnorm(x, w, out_dtype, tm=1024):
    n, d = x.shape
    tm = min(tm, n)
    return pl.pallas_call(
        _rmsnorm_kernel,
        grid=(n // tm,),
        in_specs=[pl.BlockSpec((tm, d), lambda i: (i, 0)), pl.BlockSpec((1, d), lambda i: (0, 0))],
        out_specs=pl.BlockSpec((tm, d), lambda i: (i, 0)),
        out_shape=jax.ShapeDtypeStruct((n, d), out_dtype),
        compiler_params=_cparams(("parallel",)),
        name="rmsnorm",
    )(x, w.reshape(1, d))


def _matmul_kernel(x_ref, w_ref, o_ref, *, hi):
    if hi:
        o_ref[...] = _dot_hi(x_ref[...].astype(F32), w_ref[...]).astype(o_ref.dtype)
    else:
        o_ref[...] = jnp.dot(x_ref[...], w_ref[...], preferred_element_type=F32).astype(o_ref.dtype)


def matmul(x, w, out_dtype, tm=1024, tn=1024, hi=False, name="matmul"):
    n, k = x.shape
    m = w.shape[1]
    tm = min(tm, n)
    tn = min(tn, m)
    return pl.pallas_call(
        functools.partial(_matmul_kernel, hi=hi),
        grid=(m // tn, n // tm),
        in_specs=[pl.BlockSpec((tm, k), lambda j, i: (i, 0)), pl.BlockSpec((k, tn), lambda j, i: (0, j))],
        out_specs=pl.BlockSpec((tm, tn), lambda j, i: (i, j)),
        out_shape=jax.ShapeDtypeStruct((n, m), out_dtype),
        compiler_params=_cparams(("parallel", "parallel")),
        name=name,
    )(x, w)


def _causal_conv(prev_ref, cur_ref, w_ref, first):
    prev = jnp.where(first, 0.0, prev_ref[...].astype(F32))
    x = jnp.concatenate([prev, cur_ref[...].astype(F32)], axis=0)
    acc = x * w_ref[CONV_K - 1:CONV_K, :]
    for j in range(CONV_K - 1):
        acc = acc + pltpu.roll(x, CONV_K - 1 - j, 0) * w_ref[j:j + 1, :]
    return acc[CONV_HALO:]


def _col_from_row(row, eye):
    return jnp.sum(jnp.where(eye, row, 0.0), axis=-1, keepdims=True)


def _gdn_kernel(alog_ref, dtb_ref,
                qp_ref, q_ref, kp_ref, k_ref, vp_ref, v_ref, z_ref,
                wq_ref, wk_ref, wv_ref, a_ref, b_ref, nw_ref,
                o_ref,
                s_ref, *, tt, hpb):
    hblk = pl.program_id(1)
    t = pl.program_id(2)
    first = t == 0
    nck = tt // CHUNK
    rep = GDN_V_HEADS // GDN_QK_HEADS
    dh = GDN_HEAD

    @pl.when(first)
    def _():
        s_ref[...] = jnp.zeros_like(s_ref)

    q_all = _silu(_causal_conv(qp_ref, q_ref, wq_ref, first))
    k_all = _silu(_causal_conv(kp_ref, k_ref, wk_ref, first))
    v_all = _silu(_causal_conv(vp_ref, v_ref, wv_ref, first))

    ri = lax.broadcasted_iota(jnp.int32, (CHUNK, CHUNK), 0)
    ci = lax.broadcasted_iota(jnp.int32, (CHUNK, CHUNK), 1)
    upper = (ri <= ci).astype(F32)
    eye = ri == ci
    causal = ri >= ci
    strict = ri > ci
    chunks = [slice(c * CHUNK, (c + 1) * CHUNK) for c in range(nck)]
    nsteps = CHUNK.bit_length() - 1

    qs, ks = [], []
    for j in range(hpb // rep):
        q = q_all[:, j * dh:(j + 1) * dh]
        k = k_all[:, j * dh:(j + 1) * dh]
        qs.append(q * (lax.rsqrt(jnp.sum(q * q, axis=-1, keepdims=True) + 1e-6) * (dh ** -0.5)))
        ks.append(k * lax.rsqrt(jnp.sum(k * k, axis=-1, keepdims=True) + 1e-6))

    heads = range(hpb)
    g_rows, beta_rows, gc_rows = [], [], []
    for hh in heads:
        head = hblk * hpb + hh
        neg_a = -jnp.exp(jnp.full((1, CHUNK), alog_ref[head], F32))
        g_rows.append(neg_a * _softplus(a_ref[hh] + dtb_ref[head]))
        beta_rows.append(jax.nn.sigmoid(b_ref[hh]))
        gc_rows.append(_dot_hi(g_rows[hh], upper))

    ps, xs, qkk, qes, egl = {}, {}, {}, {}, {}

    def local_prep(c):
        sl = chunks[c]
        kq = [_dot_nt(jnp.concatenate([ks[j][sl], qs[j][sl]], axis=0), ks[j][sl]) for j in range(hpb // rep)]
        for hh in heads:
            qc, kc = qs[hh // rep][sl], ks[hh // rep][sl]
            vc = v_all[sl, hh * dh:(hh + 1) * dh]
            gc_row = gc_rows[hh][c:c + 1, :]
            gc_col = _col_from_row(gc_row, eye)
            beta_col = _col_from_row(beta_rows[hh][c:c + 1, :], eye)
            g_last = jnp.sum(g_rows[hh][c:c + 1, :], axis=-1, keepdims=True)
            decay = jnp.where(causal, jnp.exp(gc_col - gc_row), 0.0)
            eg_col = jnp.exp(gc_col)
            kk, qk = kq[hh // rep][:CHUNK], kq[hh // rep][CHUNK:]
            ps[hh, c] = jnp.where(strict, kk * decay, 0.0) * beta_col
            xs[hh, c] = jnp.concatenate([vc * beta_col, kc * (beta_col * eg_col)], axis=1)
            k_dec = kc * jnp.exp(g_last - gc_col)
            qkk[hh, c] = jnp.concatenate([qk * decay, k_dec.T], axis=0)
            qes[hh, c] = qc * eg_col
            egl[hh, c] = jnp.exp(g_last)

    def solve_step(c, step):
        for hh in heads:
            u = (hh, c)
            if step + 1 < nsteps:
                r = _dot(ps[u], jnp.concatenate([xs[u], ps[u]], axis=1))
                ps[u] = r[:, 2 * dh:]
                r = r[:, :2 * dh]
            else:
                r = _dot(ps[u], xs[u])
            xs[u] = xs[u] - r if step == 0 else xs[u] + r

    s = [s_ref[hh] for hh in heads]
    v_new = {}
    o_chunks = [[] for _ in heads]

    def rec_a(c):
        for hh in heads:
            x = xs[hh, c]
            wq_s = _dot(jnp.concatenate([x[:, dh:], qes[hh, c]], axis=0), s[hh])
            v_new[hh] = x[:, :dh] - wq_s[:CHUNK]
            o_chunks[hh].append(wq_s[CHUNK:])

    def rec_b(c):
        for hh in heads:
            r = _dot(qkk[hh, c], v_new[hh])
            o_chunks[hh][c] = o_chunks[hh][c] + r[:CHUNK]
            s[hh] = s[hh] * egl[hh, c] + r[CHUNK:]

    half = nsteps // 2
    local_prep(0)
    for step in range(nsteps):
        solve_step(0, step)
    for c in range(nck):
        nxt = c + 1 < nck
        if nxt:
            local_prep(c + 1)
        rec_a(c)
        if nxt:
            for step in range(half):
                solve_step(c + 1, step)
        rec_b(c)
        if nxt:
            for step in range(half, nsteps):
                solve_step(c + 1, step)
    outs = []
    for hh in range(hpb):
        s_ref[hh] = s[hh]
        o = jnp.concatenate(o_chunks[hh], axis=0)
        outs.append(o * lax.rsqrt(jnp.mean(o * o, axis=-1, keepdims=True) + NORM_EPS) * nw_ref[...])
    y = jnp.concatenate(outs, axis=1)
    o_ref[...] = (y * _silu(z_ref[...].astype(F32))).astype(o_ref.dtype)


def gdn_mixer(qkv_raw, z_a, a4, b4, conv_w, a_log, dt_bias, norm_w, *, bsz, seq, tt=256, hpb=GDN_V_HEADS):
    n = bsz * seq
    nt = seq // tt
    hb = tt // CONV_HALO
    rep = GDN_V_HEADS // GDN_QK_HEADS
    nhb = GDN_V_HEADS // hpb
    qw = hpb // rep * GDN_HEAD
    vw = hpb * GDN_HEAD
    q0, k0, v0 = 0, GDN_QK_DIM // qw, 2 * GDN_QK_DIM // vw

    def cur(width, c0):
        return pl.BlockSpec((tt, width), lambda b, h, t, *_: (b * nt + t, c0 + h))

    def prev(width, c0):
        return pl.BlockSpec((CONV_HALO, width),
                            lambda b, h, t, *_: (jnp.maximum((b * nt + t) * hb - 1, 0), c0 + h))

    def wspec(width, c0):
        return pl.BlockSpec((CONV_K, width), lambda b, h, t, *_: (0, c0 + h))

    small = pl.BlockSpec((None, hpb, None, tt // CHUNK, CHUNK), lambda b, h, t, *_: (b, h, t, 0, 0))
    a4 = a4.reshape(bsz, GDN_V_HEADS, nt, tt // CHUNK, CHUNK)
    b4 = b4.reshape(bsz, GDN_V_HEADS, nt, tt // CHUNK, CHUNK)
    grid_spec = pltpu.PrefetchScalarGridSpec(
        num_scalar_prefetch=2,
        grid=(bsz, nhb, nt),
        in_specs=[prev(qw, q0), cur(qw, q0), prev(qw, k0), cur(qw, k0), prev(vw, v0), cur(vw, v0),
                  cur(vw, 0),
                  wspec(qw, q0), wspec(qw, k0), wspec(vw, v0), small, small,
                  pl.BlockSpec((1, GDN_HEAD), lambda b, h, t, *_: (0, 0))],
        out_specs=cur(vw, 0),
        scratch_shapes=[pltpu.VMEM((hpb, GDN_HEAD, GDN_HEAD), F32)],
    )
    return pl.pallas_call(
        functools.partial(_gdn_kernel, tt=tt, hpb=hpb),
        grid_spec=grid_spec,
        out_shape=jax.ShapeDtypeStruct((n, GDN_V_DIM), BF16),
        compiler_params=_cparams(("parallel", "parallel", "arbitrary")),
        name="gdn_mixer",
    )(a_log, dt_bias, qkv_raw, qkv_raw, qkv_raw, qkv_raw, qkv_raw, qkv_raw, z_a,
      conv_w, conv_w, conv_w, a4, b4, norm_w.reshape(1, GDN_HEAD))


def _ssd_kernel(xp_ref, x_ref, bp_ref, b_ref, cp_ref, c_ref, z_ref,
                wx_ref, wb_ref, wc_ref, bx_ref, bb_ref, bc_ref, dt_ref, alog_ref, dtb_ref, dskip_ref, nw_ref,
                o_ref,
                h_ref, *, tt):
    t = pl.program_id(2)
    first = t == 0
    nck = tt // CHUNK
    pw = 2 * SSM_HEAD_DIM
    npair = SSM_GROUP_DIM // pw

    @pl.when(first)
    def _():
        h_ref[...] = jnp.zeros_like(h_ref)

    xs = _silu(_causal_conv(xp_ref, x_ref, wx_ref, first) + bx_ref[...])
    bm = _silu(_causal_conv(bp_ref, b_ref, wb_ref, first) + bb_ref[...])
    cm = _silu(_causal_conv(cp_ref, c_ref, wc_ref, first) + bc_ref[...])

    dt_rows = _softplus(dt_ref[...] + dtb_ref[...])
    adt_rows = -jnp.exp(alog_ref[...]) * dt_rows
    r2 = lax.broadcasted_iota(jnp.int32, (pw, pw), 0)
    c2 = lax.broadcasted_iota(jnp.int32, (pw, pw), 1)
    same_head = (r2 // CHUNK) == (c2 // CHUNK)
    cum_tot = jnp.concatenate([(same_head & (r2 <= c2)).astype(F32), same_head.astype(F32)], axis=1)
    adt_pc = jnp.concatenate([adt_rows[:, p * pw:(p + 1) * pw] for p in range(npair)], axis=0)
    ct = _dot_hi(adt_pc, cum_tot)
    acs_pc, tot_pc = ct[:, :pw], ct[:, pw:]

    li = lax.broadcasted_iota(jnp.int32, (CHUNK, pw), 0)
    ji = lax.broadcasted_iota(jnp.int32, (CHUNK, pw), 1)
    lo_half = ji < CHUNK
    pick_a = ji == li
    pick_b = ji == li + CHUNK
    causal2 = li >= jnp.where(lo_half, ji, ji - CHUNK)
    rb = lax.broadcasted_iota(jnp.int32, (pw, pw), 0)
    cb_ = lax.broadcasted_iota(jnp.int32, (pw, pw), 1)
    blockdiag = (rb < CHUNK) == (cb_ < CHUNK)

    def pair_col(row):
        a = jnp.sum(jnp.where(pick_a, row, 0.0), axis=-1, keepdims=True)
        b = jnp.sum(jnp.where(pick_b, row, 0.0), axis=-1, keepdims=True)
        return jnp.where(lo_half, a, b)

    hstate = h_ref[...]
    ys = []
    for c in range(nck):
        sl = slice(c * CHUNK, (c + 1) * CHUNK)
        cmc, bmc = cm[sl], bm[sl]
        cb2 = _dot_nt(cmc, jnp.concatenate([bmc, bmc], axis=0))
        yd_parts, eacs_parts, xdec_parts, eal_parts = [], [], [], []
        for p in range(npair):
            row = p * nck + c
            acs_row = acs_pc[row:row + 1, :]
            tot_row = tot_pc[row:row + 1, :]
            acs_col = pair_col(acs_row)
            dt_col = pair_col(dt_rows[c:c + 1, p * pw:(p + 1) * pw])
            lmat = jnp.where(causal2, jnp.exp(acs_col - acs_row), 0.0)
            xdt = xs[sl, p * pw:(p + 1) * pw] * dt_col
            xdt2 = jnp.where(blockdiag, jnp.concatenate([xdt, xdt], axis=0), 0.0)
            yd_parts.append(_dot(cb2 * lmat, xdt2))
            eacs_parts.append(jnp.exp(acs_col))
            xdec_parts.append(xdt * jnp.exp(tot_row - acs_col))
            eal_parts.append(jnp.exp(tot_row))
        upd = _dot_tn(bmc, jnp.concatenate(xdec_parts, axis=1))
        y_off = _dot(cmc, hstate)
        ys.append(jnp.concatenate(yd_parts, axis=1) + y_off * jnp.concatenate(eacs_parts, axis=1))
        hstate = hstate * jnp.concatenate(eal_parts, axis=1) + upd
    h_ref[...] = hstate
    y = jnp.concatenate(ys, axis=0) + xs * dskip_ref[...]
    y = y * _silu(z_ref[...].astype(F32))
    y = y * lax.rsqrt(jnp.mean(y * y, axis=-1, keepdims=True) + SSM_NORM_EPS) * nw_ref[...]
    o_ref[...] = y.astype(o_ref.dtype)


def ssd_mixer(xbc_raw, z_b, dt_raw, conv_w, conv_b, a_log, dt_bias, d_skip, norm_w, *, bsz, seq, tt=256):
    n = bsz * seq
    nt = seq // tt
    nck = tt // CHUNK
    hb = tt // CONV_HALO
    gd = SSM_GROUP_DIM
    st = SSM_STATE
    x_blocks = SSM_D_INNER // st
    dtl = jnp.transpose(dt_raw.reshape(bsz, seq // CHUNK, CHUNK, SSM_HEADS), (0, 1, 3, 2))
    dtl = dtl.reshape(bsz, nt, nck, SSM_D_INNER)
    per_head = lambda v: jnp.repeat(v, SSM_HEAD_DIM).reshape(1, SSM_D_INNER)

    def cur(width, colf):
        return pl.BlockSpec((tt, width), lambda b, g, t, *_: (b * nt + t, colf(g)))

    def prev(width, colf):
        return pl.BlockSpec((CONV_HALO, width),
                            lambda b, g, t, *_: (jnp.maximum((b * nt + t) * hb - 1, 0), colf(g)))

    def rowspec(rows, width, colf):
        return pl.BlockSpec((rows, width), lambda b, g, t, *_: (0, colf(g)))

    xcol = lambda g: g
    bcol = lambda g: x_blocks + g
    ccol = lambda g: x_blocks + SSM_GROUPS + g
    cb2 = conv_b.reshape(1, SSM_CONV_DIM)
    return pl.pallas_call(
        functools.partial(_ssd_kernel, tt=tt),
        grid=(bsz, SSM_GROUPS, nt),
        in_specs=[prev(gd, xcol), cur(gd, xcol), prev(st, bcol), cur(st, bcol), prev(st, ccol), cur(st, ccol),
                  cur(gd, xcol),
                  rowspec(CONV_K, gd, xcol), rowspec(CONV_K, st, bcol), rowspec(CONV_K, st, ccol),
                  rowspec(1, gd, xcol), rowspec(1, st, bcol), rowspec(1, st, ccol),
                  pl.BlockSpec((None, None, nck, gd), lambda b, g, t: (b, t, 0, g)),
                  rowspec(1, gd, xcol), rowspec(1, gd, xcol), rowspec(1, gd, xcol), rowspec(1, gd, xcol)],
        out_specs=cur(gd, xcol),
        out_shape=jax.ShapeDtypeStruct((n, SSM_D_INNER), BF16),
        scratch_shapes=[pltpu.VMEM((SSM_STATE, gd), F32)],
        compiler_params=_cparams(("parallel", "parallel", "arbitrary")),
        name="ssd_mixer",
    )(xbc_raw, xbc_raw, xbc_raw, xbc_raw, xbc_raw, xbc_raw, z_b,
      conv_w, conv_w, conv_w, cb2, cb2, cb2, dtl, per_head(a_log), per_head(dt_bias), per_head(d_skip),
      norm_w.reshape(1, SSM_D_INNER))


def _merge_kernel(h_ref, oa_ref, ob_ref, ga_ref, gb_ref, wa_ref, wb_ref, wo_ref, nw_ref, hn_ref, u_ref):
    a = jnp.dot(oa_ref[...], wa_ref[...], preferred_element_type=F32)
    b = jnp.dot(ob_ref[...], wb_ref[...], preferred_element_type=F32)
    mixed = jax.nn.sigmoid(ga_ref[...].astype(F32)) * a + jax.nn.sigmoid(gb_ref[...].astype(F32)) * b
    hn = h_ref[...] + jnp.dot(mixed.astype(BF16), wo_ref[...], preferred_element_type=F32)
    hn_ref[...] = hn
    y = hn * lax.rsqrt(jnp.mean(hn * hn, axis=-1, keepdims=True) + NORM_EPS)
    u_ref[...] = (y * nw_ref[...]).astype(u_ref.dtype)


def merge_out(h, oa, ob, ga, gb, wa, wb, wo, next_norm_w, tm=512):
    n, d = h.shape
    tm = min(tm, n)
    row = lambda width: pl.BlockSpec((tm, width), lambda i: (i, 0))
    full = lambda r, c: pl.BlockSpec((r, c), lambda i: (0, 0))
    return pl.pallas_call(
        _merge_kernel,
        grid=(n // tm,),
        in_specs=[row(d), row(GDN_V_DIM), row(SSM_D_INNER), row(d), row(d),
                  full(GDN_V_DIM, d), full(SSM_D_INNER, d), full(d, d), full(1, d)],
        out_specs=[row(d), row(d)],
        out_shape=[jax.ShapeDtypeStruct((n, d), F32), jax.ShapeDtypeStruct((n, d), BF16)],
        compiler_params=_cparams(("parallel",)),
        name="merge_out",
    )(h, oa, ob, ga, gb, wa, wb, wo, next_norm_w.reshape(1, d))


def _ffn_kernel(h_ref, u_ref, wg_ref, wu_ref, wd_ref, nw_ref, hn_ref, un_ref, acc_ref):
    f = pl.program_id(1)

    @pl.when(f == 0)
    def _():
        acc_ref[...] = h_ref[...]

    u = u_ref[...]
    hid = _silu(jnp.dot(u, wg_ref[...], preferred_element_type=F32)) * jnp.dot(
        u, wu_ref[...], preferred_element_type=F32)
    acc_ref[...] += jnp.dot(hid.astype(BF16), wd_ref[...], preferred_element_type=F32)

    @pl.when(f == pl.num_programs(1) - 1)
    def _():
        hn = acc_ref[...]
        hn_ref[...] = hn
        y = hn * lax.rsqrt(jnp.mean(hn * hn, axis=-1, keepdims=True) + NORM_EPS)
        un_ref[...] = (y * nw_ref[...]).astype(un_ref.dtype)


def dense_ffn(h, u, wg, wu, wd, next_norm_w, tm=1024, tf=256):
    n, d = h.shape
    ff = wg.shape[1]
    tm = min(tm, n)
    return pl.pallas_call(
        _ffn_kernel,
        grid=(n // tm, ff // tf),
        in_specs=[pl.BlockSpec((tm, d), lambda i, f: (i, 0)), pl.BlockSpec((tm, d), lambda i, f: (i, 0)),
                  pl.BlockSpec((d, tf), lambda i, f: (0, f)), pl.BlockSpec((d, tf), lambda i, f: (0, f)),
                  pl.BlockSpec((tf, d), lambda i, f: (f, 0)), pl.BlockSpec((1, d), lambda i, f: (0, 0))],
        out_specs=[pl.BlockSpec((tm, d), lambda i, f: (i, 0)), pl.BlockSpec((tm, d), lambda i, f: (i, 0))],
        out_shape=[jax.ShapeDtypeStruct((n, d), F32), jax.ShapeDtypeStruct((n, d), BF16)],
        scratch_shapes=[pltpu.VMEM((tm, d), F32)],
        compiler_params=_cparams(("parallel", "arbitrary")),
        name="dense_ffn",
    )(h, u, wg, wu, wd, next_norm_w.reshape(1, d))


def _router_kernel(u_ref, w_ref, o_ref):
    logits = _dot_hi(u_ref[...].astype(F32), w_ref[...])
    lane = lax.broadcasted_iota(jnp.int32, logits.shape, 1)
    neg = jnp.float32(-3.0e38)
    logits = jnp.where(lane < N_EXPERTS, logits, neg)
    m1 = jnp.max(logits, axis=-1, keepdims=True)
    i1 = jnp.min(jnp.where(logits == m1, lane, 2 * N_EXPERTS), axis=-1, keepdims=True)
    rest = jnp.where(lane == i1, neg, logits)
    m2 = jnp.max(rest, axis=-1, keepdims=True)
    i2 = jnp.min(jnp.where(rest == m2, lane, 2 * N_EXPERTS), axis=-1, keepdims=True)
    e2 = jnp.exp(m2 - m1)
    g1 = 1.0 / (1.0 + e2)
    g2 = e2 / (1.0 + e2)
    out = jnp.where(lane == 0, i1.astype(F32), 0.0)
    out = jnp.where(lane == 1, i2.astype(F32), out)
    out = jnp.where(lane == 2, g1, out)
    out = jnp.where(lane == 3, g2, out)
    o_ref[...] = out


def router(u, w_pad, tm=1024):
    n, d = u.shape
    tm = min(tm, n)
    return pl.pallas_call(
        _router_kernel,
        grid=(n // tm,),
        in_specs=[pl.BlockSpec((tm, d), lambda i: (i, 0)), pl.BlockSpec((d, 128), lambda i: (0, 0))],
        out_specs=pl.BlockSpec((tm, 128), lambda i: (i, 0)),
        out_shape=jax.ShapeDtypeStruct((n, 128), F32),
        compiler_params=_cparams(("parallel",)),
        name="router",
    )(u, w_pad)


def _moe_up_kernel(be_ref, x_ref, wg_ref, wu_ref, o_ref):
    x = x_ref[...]
    hid = _silu(jnp.dot(x, wg_ref[...], preferred_element_type=F32)) * jnp.dot(
        x, wu_ref[...], preferred_element_type=F32)
    o_ref[...] = hid.astype(o_ref.dtype)


def moe_up(block_e, xb, wg, wu, tf=512):
    ns, d = xb.shape
    ff = wg.shape[2]
    nb = ns // MOE_BLOCK
    grid_spec = pltpu.PrefetchScalarGridSpec(
        num_scalar_prefetch=1,
        grid=(ff // tf, nb),
        in_specs=[pl.BlockSpec((MOE_BLOCK, d), lambda f, i, be: (i, 0)),
                  pl.BlockSpec((None, d, tf), lambda f, i, be: (be[i], 0, f)),
                  pl.BlockSpec((None, d, tf), lambda f, i, be: (be[i], 0, f))],
        out_specs=pl.BlockSpec((MOE_BLOCK, tf), lambda f, i, be: (i, f)),
    )
    return pl.pallas_call(
        _moe_up_kernel,
        grid_spec=grid_spec,
        out_shape=jax.ShapeDtypeStruct((ns, ff), BF16),
        compiler_params=_cparams(("parallel", "arbitrary")),
        name="moe_up",
    )(block_e, xb, wg, wu)


def _moe_down_kernel(be_ref, hid_ref, wd_ref, gate_ref, o_ref):
    y = jnp.dot(hid_ref[...], wd_ref[...], preferred_element_type=F32)
    o_ref[...] = (y * gate_ref[...]).astype(o_ref.dtype)


def moe_down(block_e, hid, wd, slot_gate):
    ns, ff = hid.shape
    d = wd.shape[2]
    nb = ns // MOE_BLOCK
    grid_spec = pltpu.PrefetchScalarGridSpec(
        num_scalar_prefetch=1,
        grid=(nb,),
        in_specs=[pl.BlockSpec((MOE_BLOCK, ff), lambda i, be: (i, 0)),
                  pl.BlockSpec((None, ff, d), lambda i, be: (be[i], 0, 0)),
                  pl.BlockSpec((MOE_BLOCK, 1), lambda i, be: (i, 0))],
        out_specs=pl.BlockSpec((MOE_BLOCK, d), lambda i, be: (i, 0)),
    )
    return pl.pallas_call(
        _moe_down_kernel,
        grid_spec=grid_spec,
        out_shape=jax.ShapeDtypeStruct((ns, d), F32),
        compiler_params=_cparams(("arbitrary",)),
        name="moe_down",
    )(block_e, hid, wd, slot_gate)


def _final_kernel(h_ref, y0_ref, y1_ref, nw_ref, o_ref):
    hn = h_ref[...] + y0_ref[...] + y1_ref[...]
    y = hn * lax.rsqrt(jnp.mean(hn * hn, axis=-1, keepdims=True) + NORM_EPS)
    o_ref[...] = y * nw_ref[...]


def final_combine(h, y0, y1, norm_w, tm=1024):
    n, d = h.shape
    tm = min(tm, n)
    row = pl.BlockSpec((tm, d), lambda i: (i, 0))
    return pl.pallas_call(
        _final_kernel,
        grid=(n // tm,),
        in_specs=[row, row, row, pl.BlockSpec((1, d), lambda i: (0, 0))],
        out_specs=row,
        out_shape=jax.ShapeDtypeStruct((n, d), F32),
        compiler_params=_cparams(("parallel",)),
        name="final_combine",
    )(h, y0, y1, norm_w.reshape(1, d))


def _chunk_rows(x, bsz, seq):
    hh = x.shape[1]
    return jnp.transpose(x.reshape(bsz, seq, hh), (0, 2, 1)).reshape(bsz, hh, seq // CHUNK, CHUNK)


def hybrid_mixer_layer(h, u, bsz, seq, w_in, gdn_conv_w, gdn_A_log, gdn_dt_bias, gdn_norm_w, gdn_proj,
                       ssm_conv_w, ssm_conv_b, ssm_A_log, ssm_dt_bias, ssm_D, ssm_norm_w, ssm_proj, w_out,
                       next_norm_w):
    c0 = 0
    c1 = c0 + GDN_CONV_DIM
    c2 = c1 + GDN_V_DIM
    c3 = c2 + GDN_V_HEADS
    c4 = c3 + GDN_V_HEADS
    c5 = c4 + SSM_D_INNER
    c6 = c5 + SSM_CONV_DIM
    c7 = c6 + SSM_HEADS
    c8 = c7 + D_MODEL
    wb = w_in.astype(BF16)
    qkv_raw = matmul(u, wb[:, c0:c1], BF16, name="proj_qkv")
    z_a = matmul(u, wb[:, c1:c2], BF16, name="proj_za")
    z_b = matmul(u, wb[:, c4:c5], BF16, name="proj_zb")
    xbc_raw = matmul(u, wb[:, c5:c6], BF16, name="proj_xbc")
    gate_a = matmul(u, wb[:, c7:c8], BF16, name="proj_ga")
    gate_b = matmul(u, wb[:, c8:], BF16, name="proj_gb")
    n_small = 2 * GDN_V_HEADS + SSM_HEADS
    w_small = jnp.concatenate([w_in[:, c2:c4], w_in[:, c6:c7]], axis=1)
    w_small = jnp.pad(w_small, ((0, 0), (0, 128 - n_small)))
    small = matmul(u, w_small, F32, tn=128, hi=True, name="proj_small")
    a4 = _chunk_rows(small[:, :GDN_V_HEADS], bsz, seq)
    b4 = _chunk_rows(small[:, GDN_V_HEADS:2 * GDN_V_HEADS], bsz, seq)
    dt_raw = small[:, 2 * GDN_V_HEADS:n_small]

    oa = gdn_mixer(qkv_raw, z_a, a4, b4, gdn_conv_w, gdn_A_log, gdn_dt_bias, gdn_norm_w, bsz=bsz, seq=seq)
    ob = ssd_mixer(xbc_raw, z_b, dt_raw, ssm_conv_w, ssm_conv_b, ssm_A_log, ssm_dt_bias, ssm_D, ssm_norm_w,
                   bsz=bsz, seq=seq)
    return merge_out(h, oa, ob, gate_a, gate_b, gdn_proj.astype(BF16), ssm_proj.astype(BF16),
                     w_out.astype(BF16), next_norm_w)


def moe_layer(h, u, router_w, w_gate, w_up, w_down, final_norm_w):
    n, d = h.shape
    n_assign = n * TOP_K
    r = router(u, jnp.pad(router_w, ((0, 0), (0, 128 - N_EXPERTS))))
    top_idx = r[:, :TOP_K].astype(jnp.int32)
    gates = r[:, TOP_K:2 * TOP_K]
    flat_e = top_idx.reshape(-1)
    onehot = (flat_e[:, None] == jnp.arange(N_EXPERTS)[None, :]).astype(jnp.int32)
    csum = jnp.cumsum(onehot, axis=0)
    counts = csum[-1]
    rank = jnp.sum((csum - onehot) * onehot, axis=1)
    padded = (counts + MOE_BLOCK - 1) // MOE_BLOCK * MOE_BLOCK
    ends = jnp.cumsum(padded)
    pstart = ends - padded
    dest = (pstart[flat_e] + rank).astype(jnp.int32)
    n_blocks = -(-n_assign // MOE_BLOCK) + N_EXPERTS
    n_slots = n_blocks * MOE_BLOCK
    tok = (jnp.arange(n_assign, dtype=jnp.int32) // TOP_K)
    slot_tok = jnp.zeros((n_slots,), jnp.int32).at[dest].set(tok)
    slot_gate = jnp.zeros((n_slots,), F32).at[dest].set(gates.reshape(-1))
    block_e = jnp.minimum(jnp.searchsorted(ends, jnp.arange(n_blocks) * MOE_BLOCK, side='right'),
                          N_EXPERTS - 1).astype(jnp.int32)
    xb = u[slot_tok]
    hid = moe_up(block_e, xb, w_gate.astype(BF16), w_up.astype(BF16))
    yb = moe_down(block_e, hid, w_down.astype(BF16), slot_gate.reshape(n_slots, 1))
    yk = yb[dest].reshape(n, TOP_K, d)
    return final_combine(h, yk[:, 0], yk[:, 1], final_norm_w)


def kernel(x, mix_norm_w, w_in, gdn_conv_w, gdn_A_log, gdn_dt_bias, gdn_norm_w, gdn_proj, ssm_conv_w, ssm_conv_b, ssm_A_log, ssm_dt_bias, ssm_D, ssm_norm_w, ssm_proj, w_out, ffn_norm_w, dense_w_gate, dense_w_up, dense_w_down, router_w, moe_w_gate, moe_w_up, moe_w_down, final_norm_w):
    bsz, seq, d = x.shape
    assert d == D_MODEL and w_in.shape[0] == 2, "dense-FFN layer followed by a final MoE layer"
    h = x.reshape(bsz * seq, d)
    u = rmsnorm(h, mix_norm_w[0], BF16)

    def mixer(layer, h, u):
        return hybrid_mixer_layer(
            h, u, bsz, seq, w_in[layer], gdn_conv_w[layer], gdn_A_log[layer], gdn_dt_bias[layer],
            gdn_norm_w[layer], gdn_proj[layer], ssm_conv_w[layer], ssm_conv_b[layer], ssm_A_log[layer],
            ssm_dt_bias[layer], ssm_D[layer], ssm_norm_w[layer], ssm_proj[layer], w_out[layer],
            ffn_norm_w[layer])

    h, u = mixer(0, h, u)
    h, u = dense_ffn(h, u, dense_w_gate[0].astype(BF16), dense_w_up[0].astype(BF16),
                     dense_w_down[0].astype(BF16), mix_norm_w[1])
    h, u = mixer(1, h, u)
    out = moe_layer(h, u, router_w[0], moe_w_gate[0], moe_w_up[0], moe_w_down[0], final_norm_w)
    return out.reshape(bsz, seq, d)
```

```python
import functools

import jax
import jax.numpy as jnp
from jax import lax
from jax.experimental import pallas as pl
from jax.experimental.pallas import tpu as pltpu
from jax.experimental.pallas import tpu_sc as plsc

F32 = jnp.float32
BF16 = jnp.bfloat16

D_MODEL = 1024
CONV_K = 4
CHUNK = 64
GDN_QK_HEADS = 4
GDN_V_HEADS = 8
GDN_HEAD = 128
GDN_QK_DIM = GDN_QK_HEADS * GDN_HEAD
GDN_V_DIM = GDN_V_HEADS * GDN_HEAD
GDN_CONV_DIM = 2 * GDN_QK_DIM + GDN_V_DIM
SSM_D_INNER = 2048
SSM_HEAD_DIM = 64
SSM_HEADS = SSM_D_INNER // SSM_HEAD_DIM
SSM_GROUPS = 4
SSM_HPG = SSM_HEADS // SSM_GROUPS
SSM_GROUP_DIM = SSM_D_INNER // SSM_GROUPS
SSM_STATE = 128
SSM_CONV_DIM = SSM_D_INNER + 2 * SSM_GROUPS * SSM_STATE
N_EXPERTS = 8
TOP_K = 2
MOE_BLOCK = 256
NORM_EPS = 1e-6
SSM_NORM_EPS = 1e-5
CONV_HALO = 16

VMEM_LIMIT = 56 * 1024 * 1024
SC_CORES = 2
SC_SUBCORES = 16
SC_GATHER_CHUNK = 64


def _cparams(sem):
    return pltpu.CompilerParams(dimension_semantics=sem, vmem_limit_bytes=VMEM_LIMIT)


def _silu(x):
    return x * jax.nn.sigmoid(x)


def _softplus(x):
    return jnp.maximum(x, 0.0) + jnp.log1p(jnp.exp(-jnp.abs(x)))


def _dot(a, b):
    return jnp.dot(a.astype(BF16), b.astype(BF16), preferred_element_type=F32)


def _dot_nt(a, b):
    return lax.dot_general(a.astype(BF16), b.astype(BF16), (((1,), (1,)), ((), ())),
                           preferred_element_type=F32)


def _dot_tn(a, b):
    return lax.dot_general(a.astype(BF16), b.astype(BF16), (((0,), (0,)), ((), ())),
                           preferred_element_type=F32)


def _dot_hi(a, b):
    return jnp.dot(a, b, preferred_element_type=F32, precision=lax.Precision.HIGHEST)


def _rmsnorm_kernel(x_ref, w_ref, o_ref):
    x = x_ref[...]
    y = x * lax.rsqrt(jnp.mean(x * x, axis=-1, keepdims=True) + NORM_EPS)
    o_ref[...] = (y * w_ref[...]).astype(o_ref.dtype)


def rmsnorm(x, w, out_dtype, tm=1024):
    n, d = x.shape
    tm = min(tm, n)
    return pl.pallas_call(
        _rmsnorm_kernel,
        grid=(n // tm,),
        in_specs=[pl.BlockSpec((tm, d), lambda i: (i, 0)), pl.BlockSpec((1, d), lambda i: (0, 0))],
        out_specs=pl.BlockSpec((tm, d), lambda i: (i, 0)),
        out_shape=jax.ShapeDtypeStruct((n, d), out_dtype),
        compiler_params=_cparams(("parallel",)),
        name="rmsnorm",
    )(x, w.reshape(1, d))


def _matmul_kernel(x_ref, w_ref, o_ref, *, hi):
    if hi:
        o_ref[...] = _dot_hi(x_ref[...].astype(F32), w_ref[...]).astype(o_ref.dtype)
    else:
        o_ref[...] = jnp.dot(x_ref[...], w_ref[...], preferred_element_type=F32).astype(o_ref.dtype)


def matmul(x, w, out_dtype, tm=1024, tn=1024, hi=False, name="matmul"):
    n, k = x.shape
    m = w.shape[1]
    tm = min(tm, n)
    tn = min(tn, m)
    return pl.pallas_call(
        functools.partial(_matmul_kernel, hi=hi),
        grid=(m // tn, n // tm),
        in_specs=[pl.BlockSpec((tm, k), lambda j, i: (i, 0)), pl.BlockSpec((k, tn), lambda j, i: (0, j))],
        out_specs=pl.BlockSpec((tm, tn), lambda j, i: (i, j)),
        out_shape=jax.ShapeDtypeStruct((n, m), out_dtype),
        compiler_params=_cparams(("parallel", "parallel")),
        name=name,
    )(x, w)


def _causal_conv(prev_ref, cur_ref, w_ref, first):
    prev = jnp.where(first, 0.0, prev_ref[...].astype(F32))
    x = jnp.concatenate([prev, cur_ref[...].astype(F32)], axis=0)
    acc = x * w_ref[CONV_K - 1:CONV_K, :]
    for j in range(CONV_K - 1):
        acc = acc + pltpu.roll(x, CONV_K - 1 - j, 0) * w_ref[j:j + 1, :]
    return acc[CONV_HALO:]


def _col_from_row(row, eye):
    return jnp.sum(jnp.where(eye, row, 0.0), axis=-1, keepdims=True)


def _gdn_kernel(alog_ref, dtb_ref,
                qp_ref, q_ref, kp_ref, k_ref, vp_ref, v_ref, z_ref,
                wq_ref, wk_ref, wv_ref, a_ref, b_ref, nw_ref,
                o_ref,
                s_ref, *, tt, hpb):
    hblk = pl.program_id(1)
    t = pl.program_id(2)
    first = t == 0
    nck = tt // CHUNK
    rep = GDN_V_HEADS // GDN_QK_HEADS
    dh = GDN_HEAD

    @pl.when(first)
    def _():
        s_ref[...] = jnp.zeros_like(s_ref)

    q_all = _silu(_causal_conv(qp_ref, q_ref, wq_ref, first))
    k_all = _silu(_causal_conv(kp_ref, k_ref, wk_ref, first))
    v_all = _silu(_causal_conv(vp_ref, v_ref, wv_ref, first))

    ri = lax.broadcasted_iota(jnp.int32, (CHUNK, CHUNK), 0)
    ci = lax.broadcasted_iota(jnp.int32, (CHUNK, CHUNK), 1)
    upper = (ri <= ci).astype(F32)
    eye = ri == ci
    causal = ri >= ci
    strict = ri > ci
    chunks = [slice(c * CHUNK, (c + 1) * CHUNK) for c in range(nck)]
    nsteps = CHUNK.bit_length() - 1

    qs, ks = [], []
    for j in range(hpb // rep):
        q = q_all[:, j * dh:(j + 1) * dh]
        k = k_all[:, j * dh:(j + 1) * dh]
        qs.append(q * (lax.rsqrt(jnp.sum(q * q, axis=-1, keepdims=True) + 1e-6) * (dh ** -0.5)))
        ks.append(k * lax.rsqrt(jnp.sum(k * k, axis=-1, keepdims=True) + 1e-6))

    heads = range(hpb)
    g_rows, beta_rows, gc_rows = [], [], []
    for hh in heads:
        head = hblk * hpb + hh
        neg_a = -jnp.exp(jnp.full((1, CHUNK), alog_ref[head], F32))
        g_rows.append(neg_a * _softplus(a_ref[hh] + dtb_ref[head]))
        beta_rows.append(jax.nn.sigmoid(b_ref[hh]))
        gc_rows.append(_dot_hi(g_rows[hh], upper))

    ps, xs, qkk, qes, egl = {}, {}, {}, {}, {}

    def local_prep(c):
        sl = chunks[c]
        kq = [_dot_nt(jnp.concatenate([ks[j][sl], qs[j][sl]], axis=0), ks[j][sl]) for j in range(hpb // rep)]
        for hh in heads:
            qc, kc = qs[hh // rep][sl], ks[hh // rep][sl]
            vc = v_all[sl, hh * dh:(hh + 1) * dh]
            gc_row = gc_rows[hh][c:c + 1, :]
            gc_col = _col_from_row(gc_row, eye)
            beta_col = _col_from_row(beta_rows[hh][c:c + 1, :], eye)
            g_last = jnp.sum(g_rows[hh][c:c + 1, :], axis=-1, keepdims=True)
            decay = jnp.where(causal, jnp.exp(gc_col - gc_row), 0.0)
            eg_col = jnp.exp(gc_col)
            kk, qk = kq[hh // rep][:CHUNK], kq[hh // rep][CHUNK:]
            ps[hh, c] = jnp.where(strict, kk * decay, 0.0) * beta_col
            xs[hh, c] = jnp.concatenate([vc * beta_col, kc * (beta_col * eg_col)], axis=1)
            k_dec = kc * jnp.exp(g_last - gc_col)
            qkk[hh, c] = jnp.concatenate([qk * decay, k_dec.T], axis=0)
            qes[hh, c] = qc * eg_col
            egl[hh, c] = jnp.exp(g_last)

    def solve_step(c, step):
        for hh in heads:
            u = (hh, c)
            if step + 1 < nsteps:
                r = _dot(ps[u], jnp.concatenate([xs[u], ps[u]], axis=1))
                ps[u] = r[:, 2 * dh:]
                r = r[:, :2 * dh]
            else:
                r = _dot(ps[u], xs[u])
            xs[u] = xs[u] - r if step == 0 else xs[u] + r

    s = [s_ref[hh] for hh in heads]
    v_new = {}
    o_chunks = [[] for _ in heads]

    def rec_a(c):
        for hh in heads:
            x = xs[hh, c]
            wq_s = _dot(jnp.concatenate([x[:, dh:], qes[hh, c]], axis=0), s[hh])
            v_new[hh] = x[:, :dh] - wq_s[:CHUNK]
            o_chunks[hh].append(wq_s[CHUNK:])

    def rec_b(c):
        for hh in heads:
            r = _dot(qkk[hh, c], v_new[hh])
            o_chunks[hh][c] = o_chunks[hh][c] + r[:CHUNK]
            s[hh] = s[hh] * egl[hh, c] + r[CHUNK:]

    half = nsteps // 2
    local_prep(0)
    for step in range(nsteps):
        solve_step(0, step)
    for c in range(nck):
        nxt = c + 1 < nck
        if nxt:
            local_prep(c + 1)
        rec_a(c)
        if nxt:
            for step in range(half):
                solve_step(c + 1, step)
        rec_b(c)
        if nxt:
            for step in range(half, nsteps):
                solve_step(c + 1, step)
    outs = []
    for hh in range(hpb):
        s_ref[hh] = s[hh]
        o = jnp.concatenate(o_chunks[hh], axis=0)
        outs.append(o * lax.rsqrt(jnp.mean(o * o, axis=-1, keepdims=True) + NORM_EPS) * nw_ref[...])
    y = jnp.concatenate(outs, axis=1)
    o_ref[...] = (y * _silu(z_ref[...].astype(F32))).astype(o_ref.dtype)


def gdn_mixer(qkv_raw, z_a, a4, b4, conv_w, a_log, dt_bias, norm_w, *, bsz, seq, tt=256, hpb=GDN_V_HEADS):
    n = bsz * seq
    nt = seq // tt
    hb = tt // CONV_HALO
    rep = GDN_V_HEADS // GDN_QK_HEADS
    nhb = GDN_V_HEADS // hpb
    qw = hpb // rep * GDN_HEAD
    vw = hpb * GDN_HEAD
    q0, k0, v0 = 0, GDN_QK_DIM // qw, 2 * GDN_QK_DIM // vw

    def cur(width, c0):
        return pl.BlockSpec((tt, width), lambda b, h, t, *_: (b * nt + t, c0 + h))

    def prev(width, c0):
        return pl.BlockSpec((CONV_HALO, width),
                            lambda b, h, t, *_: (jnp.maximum((b * nt + t) * hb - 1, 0), c0 + h))

    def wspec(width, c0):
        return pl.BlockSpec((CONV_K, width), lambda b, h, t, *_: (0, c0 + h))

    small = pl.BlockSpec((None, hpb, None, tt // CHUNK, CHUNK), lambda b, h, t, *_: (b, h, t, 0, 0))
    a4 = a4.reshape(bsz, GDN_V_HEADS, nt, tt // CHUNK, CHUNK)
    b4 = b4.reshape(bsz, GDN_V_HEADS, nt, tt // CHUNK, CHUNK)
    grid_spec = pltpu.PrefetchScalarGridSpec(
        num_scalar_prefetch=2,
        grid=(bsz, nhb, nt),
        in_specs=[prev(qw, q0), cur(qw, q0), prev(qw, k0), cur(qw, k0), prev(vw, v0), cur(vw, v0),
                  cur(vw, 0),
                  wspec(qw, q0), wspec(qw, k0), wspec(vw, v0), small, small,
                  pl.BlockSpec((1, GDN_HEAD), lambda b, h, t, *_: (0, 0))],
        out_specs=cur(vw, 0),
        scratch_shapes=[pltpu.VMEM((hpb, GDN_HEAD, GDN_HEAD), F32)],
    )
    return pl.pallas_call(
        functools.partial(_gdn_kernel, tt=tt, hpb=hpb),
        grid_spec=grid_spec,
        out_shape=jax.ShapeDtypeStruct((n, GDN_V_DIM), BF16),
        compiler_params=_cparams(("parallel", "parallel", "arbitrary")),
        name="gdn_mixer",
    )(a_log, dt_bias, qkv_raw, qkv_raw, qkv_raw, qkv_raw, qkv_raw, qkv_raw, z_a,
      conv_w, conv_w, conv_w, a4, b4, norm_w.reshape(1, GDN_HEAD))


def _ssd_kernel(xp_ref, x_ref, bp_ref, b_ref, cp_ref, c_ref, z_ref,
                wx_ref, wb_ref, wc_ref, bx_ref, bb_ref, bc_ref, dt_ref, alog_ref, dtb_ref, dskip_ref, nw_ref,
                o_ref,
                h_ref, *, tt):
    t = pl.program_id(2)
    first = t == 0
    nck = tt // CHUNK
    pw = 2 * SSM_HEAD_DIM
    npair = SSM_GROUP_DIM // pw

    @pl.when(first)
    def _():
        h_ref[...] = jnp.zeros_like(h_ref)

    xs = _silu(_causal_conv(xp_ref, x_ref, wx_ref, first) + bx_ref[...])
    bm = _silu(_causal_conv(bp_ref, b_ref, wb_ref, first) + bb_ref[...])
    cm = _silu(_causal_conv(cp_ref, c_ref, wc_ref, first) + bc_ref[...])

    dt_rows = _softplus(dt_ref[...] + dtb_ref[...])
    adt_rows = -jnp.exp(alog_ref[...]) * dt_rows
    r2 = lax.broadcasted_iota(jnp.int32, (pw, pw), 0)
    c2 = lax.broadcasted_iota(jnp.int32, (pw, pw), 1)
    same_head = (r2 // CHUNK) == (c2 // CHUNK)
    cum_tot = jnp.concatenate([(same_head & (r2 <= c2)).astype(F32), same_head.astype(F32)], axis=1)
    adt_pc = jnp.concatenate([adt_rows[:, p * pw:(p + 1) * pw] for p in range(npair)], axis=0)
    ct = _dot_hi(adt_pc, cum_tot)
    acs_pc, tot_pc = ct[:, :pw], ct[:, pw:]

    li = lax.broadcasted_iota(jnp.int32, (CHUNK, pw), 0)
    ji = lax.broadcasted_iota(jnp.int32, (CHUNK, pw), 1)
    lo_half = ji < CHUNK
    pick_a = ji == li
    pick_b = ji == li + CHUNK
    causal2 = li >= jnp.where(lo_half, ji, ji - CHUNK)
    rb = lax.broadcasted_iota(jnp.int32, (pw, pw), 0)
    cb_ = lax.broadcasted_iota(jnp.int32, (pw, pw), 1)
    blockdiag = (rb < CHUNK) == (cb_ < CHUNK)

    def pair_col(row):
        a = jnp.sum(jnp.where(pick_a, row, 0.0), axis=-1, keepdims=True)
        b = jnp.sum(jnp.where(pick_b, row, 0.0), axis=-1, keepdims=True)
        return jnp.where(lo_half, a, b)

    hstate = h_ref[...]
    ys = []
    for c in range(nck):
        sl = slice(c * CHUNK, (c + 1) * CHUNK)
        cmc, bmc = cm[sl], bm[sl]
        cb2 = _dot_nt(cmc, jnp.concatenate([bmc, bmc], axis=0))
        yd_parts, eacs_parts, xdec_parts, eal_parts = [], [], [], []
        for p in range(npair):
            row = p * nck + c
            acs_row = acs_pc[row:row + 1, :]
            tot_row = tot_pc[row:row + 1, :]
            acs_col = pair_col(acs_row)
            dt_col = pair_col(dt_rows[c:c + 1, p * pw:(p + 1) * pw])
            lmat = jnp.where(causal2, jnp.exp(acs_col - acs_row), 0.0)
            xdt = xs[sl, p * pw:(p + 1) * pw] * dt_col
            xdt2 = jnp.where(blockdiag, jnp.concatenate([xdt, xdt], axis=0), 0.0)
            yd_parts.append(_dot(cb2 * lmat, xdt2))
            eacs_parts.append(jnp.exp(acs_col))
            xdec_parts.append(xdt * jnp.exp(tot_row - acs_col))
            eal_parts.append(jnp.exp(tot_row))
        upd = _dot_tn(bmc, jnp.concatenate(xdec_parts, axis=1))
        y_off = _dot(cmc, hstate)
        ys.append(jnp.concatenate(yd_parts, axis=1) + y_off * jnp.concatenate(eacs_parts, axis=1))
        hstate = hstate * jnp.concatenate(eal_parts, axis=1) + upd
    h_ref[...] = hstate
    y = jnp.concatenate(ys, axis=0) + xs * dskip_ref[...]
    y = y * _silu(z_ref[...].astype(F32))
    y = y * lax.rsqrt(jnp.mean(y * y, axis=-1, keepdims=True) + SSM_NORM_EPS) * nw_ref[...]
    o_ref[...] = y.astype(o_ref.dtype)


def ssd_mixer(xbc_raw, z_b, dt_raw, conv_w, conv_b, a_log, dt_bias, d_skip, norm_w, *, bsz, seq, tt=256):
    n = bsz * seq
    nt = seq // tt
    nck = tt // CHUNK
    hb = tt // CONV_HALO
    gd = SSM_GROUP_DIM
    st = SSM_STATE
    x_blocks = SSM_D_INNER // st
    dtl = jnp.transpose(dt_raw.reshape(bsz, seq // CHUNK, CHUNK, SSM_HEADS), (0, 1, 3, 2))
    dtl = dtl.reshape(bsz, nt, nck, SSM_D_INNER)
    per_head = lambda v: jnp.repeat(v, SSM_HEAD_DIM).reshape(1, SSM_D_INNER)

    def cur(width, colf):
        return pl.BlockSpec((tt, width), lambda b, g, t, *_: (b * nt + t, colf(g)))

    def prev(width, colf):
        return pl.BlockSpec((CONV_HALO, width),
                            lambda b, g, t, *_: (jnp.maximum((b * nt + t) * hb - 1, 0), colf(g)))

    def rowspec(rows, width, colf):
        return pl.BlockSpec((rows, width), lambda b, g, t, *_: (0, colf(g)))

    xcol = lambda g: g
    bcol = lambda g: x_blocks + g
    ccol = lambda g: x_blocks + SSM_GROUPS + g
    cb2 = conv_b.reshape(1, SSM_CONV_DIM)
    return pl.pallas_call(
        functools.partial(_ssd_kernel, tt=tt),
        grid=(bsz, SSM_GROUPS, nt),
        in_specs=[prev(gd, xcol), cur(gd, xcol), prev(st, bcol), cur(st, bcol), prev(st, ccol), cur(st, ccol),
                  cur(gd, xcol),
                  rowspec(CONV_K, gd, xcol), rowspec(CONV_K, st, bcol), rowspec(CONV_K, st, ccol),
                  rowspec(1, gd, xcol), rowspec(1, st, bcol), rowspec(1, st, ccol),
                  pl.BlockSpec((None, None, nck, gd), lambda b, g, t: (b, t, 0, g)),
                  rowspec(1, gd, xcol), rowspec(1, gd, xcol), rowspec(1, gd, xcol), rowspec(1, gd, xcol)],
        out_specs=cur(gd, xcol),
        out_shape=jax.ShapeDtypeStruct((n, SSM_D_INNER), BF16),
        scratch_shapes=[pltpu.VMEM((SSM_STATE, gd), F32)],
        compiler_params=_cparams(("parallel", "parallel", "arbitrary")),
        name="ssd_mixer",
    )(xbc_raw, xbc_raw, xbc_raw, xbc_raw, xbc_raw, xbc_raw, z_b,
      conv_w, conv_w, conv_w, cb2, cb2, cb2, dtl, per_head(a_log), per_head(dt_bias), per_head(d_skip),
      norm_w.reshape(1, SSM_D_INNER))


def _merge_kernel(h_ref, oa_ref, ob_ref, ga_ref, gb_ref, wa_ref, wb_ref, wo_ref, nw_ref, hn_ref, u_ref):
    a = jnp.dot(oa_ref[...], wa_ref[...], preferred_element_type=F32)
    b = jnp.dot(ob_ref[...], wb_ref[...], preferred_element_type=F32)
    mixed = jax.nn.sigmoid(ga_ref[...].astype(F32)) * a + jax.nn.sigmoid(gb_ref[...].astype(F32)) * b
    hn = h_ref[...] + jnp.dot(mixed.astype(BF16), wo_ref[...], preferred_element_type=F32)
    hn_ref[...] = hn
    y = hn * lax.rsqrt(jnp.mean(hn * hn, axis=-1, keepdims=True) + NORM_EPS)
    u_ref[...] = (y * nw_ref[...]).astype(u_ref.dtype)


def merge_out(h, oa, ob, ga, gb, wa, wb, wo, next_norm_w, u_dtype, tm=512):
    n, d = h.shape
    tm = min(tm, n)
    row = lambda width: pl.BlockSpec((tm, width), lambda i: (i, 0))
    full = lambda r, c: pl.BlockSpec((r, c), lambda i: (0, 0))
    return pl.pallas_call(
        _merge_kernel,
        grid=(n // tm,),
        in_specs=[row(d), row(GDN_V_DIM), row(SSM_D_INNER), row(d), row(d),
                  full(GDN_V_DIM, d), full(SSM_D_INNER, d), full(d, d), full(1, d)],
        out_specs=[row(d), row(d)],
        out_shape=[jax.ShapeDtypeStruct((n, d), F32), jax.ShapeDtypeStruct((n, d), u_dtype)],
        compiler_params=_cparams(("parallel",)),
        name="merge_out",
    )(h, oa, ob, ga, gb, wa, wb, wo, next_norm_w.reshape(1, d))


def _ffn_kernel(h_ref, u_ref, wg_ref, wu_ref, wd_ref, nw_ref, hn_ref, un_ref, acc_ref):
    f = pl.program_id(1)

    @pl.when(f == 0)
    def _():
        acc_ref[...] = h_ref[...]

    u = u_ref[...]
    hid = _silu(jnp.dot(u, wg_ref[...], preferred_element_type=F32)) * jnp.dot(
        u, wu_ref[...], preferred_element_type=F32)
    acc_ref[...] += jnp.dot(hid.astype(BF16), wd_ref[...], preferred_element_type=F32)

    @pl.when(f == pl.num_programs(1) - 1)
    def _():
        hn = acc_ref[...]
        hn_ref[...] = hn
        y = hn * lax.rsqrt(jnp.mean(hn * hn, axis=-1, keepdims=True) + NORM_EPS)
        un_ref[...] = (y * nw_ref[...]).astype(un_ref.dtype)


def dense_ffn(h, u, wg, wu, wd, next_norm_w, tm=1024, tf=256):
    n, d = h.shape
    ff = wg.shape[1]
    tm = min(tm, n)
    return pl.pallas_call(
        _ffn_kernel,
        grid=(n // tm, ff // tf),
        in_specs=[pl.BlockSpec((tm, d), lambda i, f: (i, 0)), pl.BlockSpec((tm, d), lambda i, f: (i, 0)),
                  pl.BlockSpec((d, tf), lambda i, f: (0, f)), pl.BlockSpec((d, tf), lambda i, f: (0, f)),
                  pl.BlockSpec((tf, d), lambda i, f: (f, 0)), pl.BlockSpec((1, d), lambda i, f: (0, 0))],
        out_specs=[pl.BlockSpec((tm, d), lambda i, f: (i, 0)), pl.BlockSpec((tm, d), lambda i, f: (i, 0))],
        out_shape=[jax.ShapeDtypeStruct((n, d), F32), jax.ShapeDtypeStruct((n, d), BF16)],
        scratch_shapes=[pltpu.VMEM((tm, d), F32)],
        compiler_params=_cparams(("parallel", "arbitrary")),
        name="dense_ffn",
    )(h, u, wg, wu, wd, next_norm_w.reshape(1, d))


def _router_kernel(u_ref, w_ref, o_ref):
    logits = _dot_hi(u_ref[...].astype(F32), w_ref[...])
    lane = lax.broadcasted_iota(jnp.int32, logits.shape, 1)
    neg = jnp.float32(-3.0e38)
    logits = jnp.where(lane < N_EXPERTS, logits, neg)
    m1 = jnp.max(logits, axis=-1, keepdims=True)
    i1 = jnp.min(jnp.where(logits == m1, lane, 2 * N_EXPERTS), axis=-1, keepdims=True)
    rest = jnp.where(lane == i1, neg, logits)
    m2 = jnp.max(rest, axis=-1, keepdims=True)
    i2 = jnp.min(jnp.where(rest == m2, lane, 2 * N_EXPERTS), axis=-1, keepdims=True)
    e2 = jnp.exp(m2 - m1)
    g1 = 1.0 / (1.0 + e2)
    g2 = e2 / (1.0 + e2)
    out = jnp.where(lane == 0, i1.astype(F32), 0.0)
    out = jnp.where(lane == 1, i2.astype(F32), out)
    out = jnp.where(lane == 2, g1, out)
    out = jnp.where(lane == 3, g2, out)
    o_ref[...] = out


def router(u, w_pad, tm=1024):
    n, d = u.shape
    tm = min(tm, n)
    return pl.pallas_call(
        _router_kernel,
        grid=(n // tm,),
        in_specs=[pl.BlockSpec((tm, d), lambda i: (i, 0)), pl.BlockSpec((d, 128), lambda i: (0, 0))],
        out_specs=pl.BlockSpec((tm, 128), lambda i: (i, 0)),
        out_shape=jax.ShapeDtypeStruct((n, 128), F32),
        compiler_params=_cparams(("parallel",)),
        name="router",
    )(u, w_pad)


def sc_gather_rows(table, idx, chunk=SC_GATHER_CHUNK):
    v, d = table.shape
    b = idx.shape[0]
    nw = SC_CORES * SC_SUBCORES
    per_w = b // nw
    assert per_w * nw == b and per_w % chunk == 0 and chunk % 8 == 0 and chunk <= 128
    mesh = plsc.VectorSubcoreMesh(core_axis_name="c", subcore_axis_name="s")

    @functools.partial(
        pl.kernel, mesh=mesh,
        out_type=jax.ShapeDtypeStruct((b, d), table.dtype),
        scratch_types=[pltpu.VMEM((chunk,), jnp.int32), pltpu.VMEM((chunk, d), table.dtype),
                       pltpu.SemaphoreType.DMA],
    )
    def gather_kernel(table_hbm, idx_hbm, out_hbm, idx_v, rows_v, sem):
        wid = lax.axis_index("s") * SC_CORES + lax.axis_index("c")
        base = wid * per_w

        @pl.loop(0, per_w // chunk)
        def _(j):
            off = pl.multiple_of(base + j * chunk, 8)
            pltpu.sync_copy(idx_hbm.at[pl.ds(off, chunk)], idx_v)
            pltpu.async_copy(table_hbm.at[idx_v], rows_v, sem).wait()
            pltpu.sync_copy(rows_v, out_hbm.at[pl.ds(off, chunk)])

    return gather_kernel(table, idx)


def _expert_changed(be_ref, i):
    return jnp.logical_or(i == 0, be_ref[i] != be_ref[jnp.maximum(i - 1, 0)])


def _moe_up_kernel(be_ref, x_ref, wg_ref, wu_ref, o_ref, wg_bf, wu_bf):
    @pl.when(_expert_changed(be_ref, pl.program_id(1)))
    def _():
        wg_bf[...] = wg_ref[...].astype(BF16)
        wu_bf[...] = wu_ref[...].astype(BF16)

    x = x_ref[...].astype(BF16)
    hid = _silu(jnp.dot(x, wg_bf[...], preferred_element_type=F32)) * jnp.dot(
        x, wu_bf[...], preferred_element_type=F32)
    o_ref[...] = hid.astype(o_ref.dtype)


def moe_up(block_e, xb, wg, wu, tf=1792):
    ns, d = xb.shape
    ff = wg.shape[2]
    nb = ns // MOE_BLOCK
    grid_spec = pltpu.PrefetchScalarGridSpec(
        num_scalar_prefetch=1,
        grid=(ff // tf, nb),
        in_specs=[pl.BlockSpec((MOE_BLOCK, d), lambda f, i, be: (i, 0)),
                  pl.BlockSpec((None, d, tf), lambda f, i, be: (be[i], 0, f)),
                  pl.BlockSpec((None, d, tf), lambda f, i, be: (be[i], 0, f))],
        out_specs=pl.BlockSpec((MOE_BLOCK, tf), lambda f, i, be: (i, f)),
        scratch_shapes=[pltpu.VMEM((d, tf), BF16), pltpu.VMEM((d, tf), BF16)],
    )
    return pl.pallas_call(
        _moe_up_kernel,
        grid_spec=grid_spec,
        out_shape=jax.ShapeDtypeStruct((ns, ff), BF16),
        compiler_params=_cparams(("arbitrary", "arbitrary")),
        name="moe_up",
    )(block_e, xb, wg, wu)


def _moe_down_kernel(be_ref, hid_ref, wd_ref, o_ref, wd_bf):
    @pl.when(_expert_changed(be_ref, pl.program_id(0)))
    def _():
        wd_bf[...] = wd_ref[...].astype(BF16)

    o_ref[...] = jnp.dot(hid_ref[...], wd_bf[...], preferred_element_type=F32)


def moe_down(block_e, hid, wd):
    ns, ff = hid.shape
    d = wd.shape[2]
    nb = ns // MOE_BLOCK
    grid_spec = pltpu.PrefetchScalarGridSpec(
        num_scalar_prefetch=1,
        grid=(nb,),
        in_specs=[pl.BlockSpec((MOE_BLOCK, ff), lambda i, be: (i, 0)),
                  pl.BlockSpec((None, ff, d), lambda i, be: (be[i], 0, 0))],
        out_specs=pl.BlockSpec((MOE_BLOCK, d), lambda i, be: (i, 0)),
        scratch_shapes=[pltpu.VMEM((ff, d), BF16)],
    )
    return pl.pallas_call(
        _moe_down_kernel,
        grid_spec=grid_spec,
        out_shape=jax.ShapeDtypeStruct((ns, d), F32),
        compiler_params=_cparams(("arbitrary",)),
        name="moe_down",
    )(block_e, hid, wd)


def _final_kernel(h_ref, y0_ref, y1_ref, r_ref, nw_ref, o_ref):
    gates = r_ref[...]
    hn = h_ref[...] + gates[:, TOP_K:TOP_K + 1] * y0_ref[...] + gates[:, TOP_K + 1:TOP_K + 2] * y1_ref[...]
    y = hn * lax.rsqrt(jnp.mean(hn * hn, axis=-1, keepdims=True) + NORM_EPS)
    o_ref[...] = y * nw_ref[...]


def final_combine(h, yk, r, norm_w, tm=1024):
    n, d = h.shape
    tm = min(tm, n)
    nblk = n // tm
    row = pl.BlockSpec((tm, d), lambda i: (i, 0))
    return pl.pallas_call(
        _final_kernel,
        grid=(nblk,),
        in_specs=[row, row, pl.BlockSpec((tm, d), lambda i: (i + nblk, 0)),
                  pl.BlockSpec((tm, 128), lambda i: (i, 0)), pl.BlockSpec((1, d), lambda i: (0, 0))],
        out_specs=row,
        out_shape=jax.ShapeDtypeStruct((n, d), F32),
        compiler_params=_cparams(("parallel",)),
        name="final_combine",
    )(h, yk, yk, r, norm_w.reshape(1, d))


def _chunk_rows(x, bsz, seq):
    hh = x.shape[1]
    return jnp.transpose(x.reshape(bsz, seq, hh), (0, 2, 1)).reshape(bsz, hh, seq // CHUNK, CHUNK)


def hybrid_mixer_layer(h, u, bsz, seq, w_in, gdn_conv_w, gdn_A_log, gdn_dt_bias, gdn_norm_w, gdn_proj,
                       ssm_conv_w, ssm_conv_b, ssm_A_log, ssm_dt_bias, ssm_D, ssm_norm_w, ssm_proj, w_out,
                       next_norm_w, u_dtype):
    c0 = 0
    c1 = c0 + GDN_CONV_DIM
    c2 = c1 + GDN_V_DIM
    c3 = c2 + GDN_V_HEADS
    c4 = c3 + GDN_V_HEADS
    c5 = c4 + SSM_D_INNER
    c6 = c5 + SSM_CONV_DIM
    c7 = c6 + SSM_HEADS
    c8 = c7 + D_MODEL
    wb = w_in.astype(BF16)
    qkv_raw = matmul(u, wb[:, c0:c1], BF16, name="proj_qkv")
    z_a = matmul(u, wb[:, c1:c2], BF16, name="proj_za")
    z_b = matmul(u, wb[:, c4:c5], BF16, name="proj_zb")
    xbc_raw = matmul(u, wb[:, c5:c6], BF16, name="proj_xbc")
    gate_a = matmul(u, wb[:, c7:c8], BF16, name="proj_ga")
    gate_b = matmul(u, wb[:, c8:], BF16, name="proj_gb")
    n_small = 2 * GDN_V_HEADS + SSM_HEADS
    w_small = jnp.concatenate([w_in[:, c2:c4], w_in[:, c6:c7]], axis=1)
    w_small = jnp.pad(w_small, ((0, 0), (0, 128 - n_small)))
    small = matmul(u, w_small, F32, tn=128, hi=True, name="proj_small")
    a4 = _chunk_rows(small[:, :GDN_V_HEADS], bsz, seq)
    b4 = _chunk_rows(small[:, GDN_V_HEADS:2 * GDN_V_HEADS], bsz, seq)
    dt_raw = small[:, 2 * GDN_V_HEADS:n_small]

    oa = gdn_mixer(qkv_raw, z_a, a4, b4, gdn_conv_w, gdn_A_log, gdn_dt_bias, gdn_norm_w, bsz=bsz, seq=seq)
    ob = ssd_mixer(xbc_raw, z_b, dt_raw, ssm_conv_w, ssm_conv_b, ssm_A_log, ssm_dt_bias, ssm_D, ssm_norm_w,
                   bsz=bsz, seq=seq)
    return merge_out(h, oa, ob, gate_a, gate_b, gdn_proj.astype(BF16), ssm_proj.astype(BF16),
                     w_out.astype(BF16), next_norm_w, u_dtype)


def moe_layer(h, u, router_w, w_gate, w_up, w_down, final_norm_w):
    n, d = h.shape
    n_assign = n * TOP_K
    r = router(u, jnp.pad(router_w, ((0, 0), (0, 128 - N_EXPERTS))))
    top_idx = r[:, :TOP_K].astype(jnp.int32)
    flat_e = top_idx.reshape(-1)
    onehot = (flat_e[:, None] == jnp.arange(N_EXPERTS)[None, :]).astype(jnp.int32)
    csum = jnp.cumsum(onehot, axis=0)
    counts = csum[-1]
    rank = jnp.sum((csum - onehot) * onehot, axis=1)
    padded = (counts + MOE_BLOCK - 1) // MOE_BLOCK * MOE_BLOCK
    ends = jnp.cumsum(padded)
    pstart = ends - padded
    dest = (pstart[flat_e] + rank).astype(jnp.int32)
    n_blocks = -(-n_assign // MOE_BLOCK) + N_EXPERTS
    n_slots = n_blocks * MOE_BLOCK
    tok = (jnp.arange(n_assign, dtype=jnp.int32) // TOP_K)
    slot_tok = jnp.zeros((n_slots,), jnp.int32).at[dest].set(tok)
    block_start = jnp.arange(n_blocks, dtype=jnp.int32) * MOE_BLOCK
    block_e = jnp.minimum(jnp.sum(block_start[:, None] >= ends[None, :], axis=1), N_EXPERTS - 1).astype(jnp.int32)
    xb = sc_gather_rows(u, slot_tok)
    hid = moe_up(block_e, xb, w_gate, w_up)
    yb = moe_down(block_e, hid, w_down)
    yk = sc_gather_rows(yb, dest.reshape(n, TOP_K).T.reshape(-1))
    return final_combine(h, yk, r, final_norm_w)


def kernel(x, mix_norm_w, w_in, gdn_conv_w, gdn_A_log, gdn_dt_bias, gdn_norm_w, gdn_proj, ssm_conv_w, ssm_conv_b, ssm_A_log, ssm_dt_bias, ssm_D, ssm_norm_w, ssm_proj, w_out, ffn_norm_w, dense_w_gate, dense_w_up, dense_w_down, router_w, moe_w_gate, moe_w_up, moe_w_down, final_norm_w):
    bsz, seq, d = x.shape
    assert d == D_MODEL and w_in.shape[0] == 2, "dense-FFN layer followed by a final MoE layer"
    h = x.reshape(bsz * seq, d)
    u = rmsnorm(h, mix_norm_w[0], BF16)

    def mixer(layer, h, u, u_dtype):
        return hybrid_mixer_layer(
            h, u, bsz, seq, w_in[layer], gdn_conv_w[layer], gdn_A_log[layer], gdn_dt_bias[layer],
            gdn_norm_w[layer], gdn_proj[layer], ssm_conv_w[layer], ssm_conv_b[layer], ssm_A_log[layer],
            ssm_dt_bias[layer], ssm_D[layer], ssm_norm_w[layer], ssm_proj[layer], w_out[layer],
            ffn_norm_w[layer], u_dtype)

    h, u = mixer(0, h, u, BF16)
    h, u = dense_ffn(h, u, dense_w_gate[0].astype(BF16), dense_w_up[0].astype(BF16),
                     dense_w_down[0].astype(BF16), mix_norm_w[1])
    h, u = mixer(1, h, u, F32)
    out = moe_layer(h, u, router_w[0], moe_w_gate[0], moe_w_up[0], moe_w_down[0], final_norm_w)
    return out.reshape(bsz, seq, d)
```

```python
import functools

import jax
import jax.numpy as jnp
from jax import lax
from jax.experimental import pallas as pl
from jax.experimental.pallas import tpu as pltpu
from jax.experimental.pallas import tpu_sc as plsc

F32 = jnp.float32
BF16 = jnp.bfloat16

D_MODEL = 1024
CONV_K = 4
CHUNK = 64
GDN_QK_HEADS = 4
GDN_V_HEADS = 8
GDN_HEAD = 128
GDN_QK_DIM = GDN_QK_HEADS * GDN_HEAD
GDN_V_DIM = GDN_V_HEADS * GDN_HEAD
GDN_CONV_DIM = 2 * GDN_QK_DIM + GDN_V_DIM
SSM_D_INNER = 2048
SSM_HEAD_DIM = 64
SSM_HEADS = SSM_D_INNER // SSM_HEAD_DIM
SSM_GROUPS = 4
SSM_HPG = SSM_HEADS // SSM_GROUPS
SSM_GROUP_DIM = SSM_D_INNER // SSM_GROUPS
SSM_STATE = 128
SSM_CONV_DIM = SSM_D_INNER + 2 * SSM_GROUPS * SSM_STATE
N_EXPERTS = 8
TOP_K = 2
MOE_BLOCK = 512
NORM_EPS = 1e-6
SSM_NORM_EPS = 1e-5
CONV_HALO = 16

VMEM_LIMIT = 56 * 1024 * 1024
SC_CORES = 2
SC_SUBCORES = 16
SC_GATHER_CHUNK = 128


def _cparams(sem):
    return pltpu.CompilerParams(dimension_semantics=sem, vmem_limit_bytes=VMEM_LIMIT)


def _silu(x):
    h = 0.5 * x
    return h + h * jnp.tanh(h)


def _softplus(x):
    return jnp.maximum(x, 0.0) + jnp.log1p(jnp.exp(-jnp.abs(x)))


def _dot(a, b):
    return jnp.dot(a.astype(BF16), b.astype(BF16), preferred_element_type=F32)


def _dot_nt(a, b):
    return lax.dot_general(a.astype(BF16), b.astype(BF16), (((1,), (1,)), ((), ())),
                           preferred_element_type=F32)


def _dot_tn(a, b):
    return lax.dot_general(a.astype(BF16), b.astype(BF16), (((0,), (0,)), ((), ())),
                           preferred_element_type=F32)


def _dot_hi(a, b):
    return jnp.dot(a, b, preferred_element_type=F32, precision=lax.Precision.HIGHEST)


def _pack_pairs(x):
    half = x.shape[1] // 2
    bits = lax.bitcast_convert_type(x.astype(BF16).astype(F32), jnp.uint32)
    return (bits[:, :half] >> 16) | (bits[:, half:] & jnp.uint32(0xFFFF0000))


def _unpack_pairs(p):
    lo = lax.bitcast_convert_type(p << 16, F32)
    hi = lax.bitcast_convert_type(p & jnp.uint32(0xFFFF0000), F32)
    return jnp.concatenate([lo, hi], axis=1)


def _rmsnorm_kernel(x_ref, w_ref, o_ref):
    x = x_ref[...]
    y = x * lax.rsqrt(jnp.mean(x * x, axis=-1, keepdims=True) + NORM_EPS)
    o_ref[...] = (y * w_ref[...]).astype(o_ref.dtype)


def rmsnorm(x, w, out_dtype, tm=1024):
    n, d = x.shape
    tm = min(tm, n)
    return pl.pallas_call(
        _rmsnorm_kernel,
        grid=(n // tm,),
        in_specs=[pl.BlockSpec((tm, d), lambda i: (i, 0)), pl.BlockSpec((1, d), lambda i: (0, 0))],
        out_specs=pl.BlockSpec((tm, d), lambda i: (i, 0)),
        out_shape=jax.ShapeDtypeStruct((n, d), out_dtype),
        compiler_params=_cparams(("parallel",)),
        name="rmsnorm",
    )(x, w.reshape(1, d))


def _matmul_kernel(x_ref, w_ref, o_ref, *, parts):
    r = jnp.dot(x_ref[...], w_ref[...], preferred_element_type=F32)
    tn = o_ref.shape[1]
    acc = r[:, :tn]
    for p in range(1, parts):
        acc = acc + r[:, p * tn:(p + 1) * tn]
    o_ref[...] = acc.astype(o_ref.dtype)


def matmul(x, w, out_dtype, tm=1024, tn=1024, full_precision=False, name="matmul"):
    n, k = x.shape
    m = w.shape[1]
    tm = min(tm, n)
    tn = min(tn, m)
    parts = 1
    if full_precision:
        assert m == tn
        w_hi = w.astype(BF16)
        w_mid = (w - w_hi.astype(F32)).astype(BF16)
        w_lo = (w - w_hi.astype(F32) - w_mid.astype(F32)).astype(BF16)
        w = jnp.concatenate([w_hi, w_mid, w_lo], axis=1)
        parts = 3
    return pl.pallas_call(
        functools.partial(_matmul_kernel, parts=parts),
        grid=(m // tn, n // tm),
        in_specs=[pl.BlockSpec((tm, k), lambda j, i: (i, 0)),
                  pl.BlockSpec((k, parts * tn), lambda j, i: (0, j))],
        out_specs=pl.BlockSpec((tm, tn), lambda j, i: (i, j)),
        out_shape=jax.ShapeDtypeStruct((n, m), out_dtype),
        compiler_params=_cparams(("parallel", "parallel")),
        name=name,
    )(x, w)


def _causal_conv(prev_ref, cur_ref, w_ref, first):
    prev = jnp.where(first, 0.0, prev_ref[...].astype(F32))
    x = jnp.concatenate([prev, cur_ref[...].astype(F32)], axis=0)
    assert CONV_K == 4
    x1 = pltpu.roll(x, 1, 0)
    near = x * w_ref[3:4, :] + x1 * w_ref[2:3, :]
    far = x * w_ref[1:2, :] + x1 * w_ref[0:1, :]
    return (near + pltpu.roll(far, 2, 0))[CONV_HALO:]


def _col_from_row(row, eye):
    return jnp.sum(jnp.where(eye, row, 0.0), axis=-1, keepdims=True)


def _gdn_kernel(alog_ref, dtb_ref,
                qp_ref, q_ref, kp_ref, k_ref, vp_ref, v_ref, z_ref,
                wq_ref, wk_ref, wv_ref, a_ref, b_ref, nw_ref,
                o_ref,
                s_ref, *, tt, hpb):
    hblk = pl.program_id(1)
    t = pl.program_id(2)
    first = t == 0
    nck = tt // CHUNK
    rep = GDN_V_HEADS // GDN_QK_HEADS
    dh = GDN_HEAD

    @pl.when(first)
    def _():
        s_ref[...] = jnp.zeros_like(s_ref)

    q_all = _silu(_causal_conv(qp_ref, q_ref, wq_ref, first))
    k_all = _silu(_causal_conv(kp_ref, k_ref, wk_ref, first))
    v_all = _silu(_causal_conv(vp_ref, v_ref, wv_ref, first))

    ri = lax.broadcasted_iota(jnp.int32, (CHUNK, CHUNK), 0)
    ci = lax.broadcasted_iota(jnp.int32, (CHUNK, CHUNK), 1)
    upper = (ri <= ci).astype(F32)
    eye = ri == ci
    causal = ri >= ci
    strict = ri > ci
    chunks = [slice(c * CHUNK, (c + 1) * CHUNK) for c in range(nck)]
    nsteps = CHUNK.bit_length() - 1

    qs, ks = [], []
    for j in range(hpb // rep):
        q = q_all[:, j * dh:(j + 1) * dh]
        k = k_all[:, j * dh:(j + 1) * dh]
        qs.append(q * (lax.rsqrt(jnp.sum(q * q, axis=-1, keepdims=True) + 1e-6) * (dh ** -0.5)))
        ks.append(k * lax.rsqrt(jnp.sum(k * k, axis=-1, keepdims=True) + 1e-6))

    heads = range(hpb)
    g_rows, beta_rows, gc_rows = [], [], []
    for hh in heads:
        head = hblk * hpb + hh
        neg_a = -jnp.exp(jnp.full((1, CHUNK), alog_ref[head], F32))
        g_rows.append(neg_a * _softplus(a_ref[hh] + dtb_ref[head]))
        beta_rows.append(jax.nn.sigmoid(b_ref[hh]))
        gc_rows.append(_dot_hi(g_rows[hh], upper))

    ps, xs, qkk, qes, egl = {}, {}, {}, {}, {}

    def local_prep(c):
        sl = chunks[c]
        kq = [_dot_nt(jnp.concatenate([ks[j][sl], qs[j][sl]], axis=0), ks[j][sl]) for j in range(hpb // rep)]
        for hh in heads:
            qc, kc = qs[hh // rep][sl], ks[hh // rep][sl]
            vc = v_all[sl, hh * dh:(hh + 1) * dh]
            gc_row = gc_rows[hh][c:c + 1, :]
            gc_col = _col_from_row(gc_row, eye)
            beta_col = _col_from_row(beta_rows[hh][c:c + 1, :], eye)
            g_last = jnp.sum(g_rows[hh][c:c + 1, :], axis=-1, keepdims=True)
            decay = jnp.where(causal, jnp.exp(gc_col - gc_row), 0.0)
            eg_col = jnp.exp(gc_col)
            kk, qk = kq[hh // rep][:CHUNK], kq[hh // rep][CHUNK:]
            ps[hh, c] = jnp.where(strict, kk * decay, 0.0) * beta_col
            xs[hh, c] = jnp.concatenate([vc * beta_col, kc * (beta_col * eg_col)], axis=1)
            k_dec = kc * jnp.exp(g_last - gc_col)
            qkk[hh, c] = jnp.concatenate([qk * decay, k_dec.T], axis=0)
            qes[hh, c] = qc * eg_col
            egl[hh, c] = jnp.exp(g_last)

    def solve_step(c, step):
        for hh in heads:
            u = (hh, c)
            if step + 1 < nsteps:
                r = _dot(ps[u], jnp.concatenate([xs[u], ps[u]], axis=1))
                ps[u] = r[:, 2 * dh:]
                r = r[:, :2 * dh]
            else:
                r = _dot(ps[u], xs[u])
            xs[u] = xs[u] - r if step == 0 else xs[u] + r

    s = [s_ref[hh] for hh in heads]
    v_new = {}
    o_chunks = [[] for _ in heads]

    def rec_a(c):
        for hh in heads:
            x = xs[hh, c]
            wq_s = _dot(jnp.concatenate([x[:, dh:], qes[hh, c]], axis=0), s[hh])
            v_new[hh] = x[:, :dh] - wq_s[:CHUNK]
            o_chunks[hh].append(wq_s[CHUNK:])

    def rec_b(c):
        for hh in heads:
            r = _dot(qkk[hh, c], v_new[hh])
            o_chunks[hh][c] = o_chunks[hh][c] + r[:CHUNK]
            s[hh] = s[hh] * egl[hh, c] + r[CHUNK:]

    half = nsteps // 2
    local_prep(0)
    for step in range(nsteps):
        solve_step(0, step)
    for c in range(nck):
        nxt = c + 1 < nck
        if nxt:
            local_prep(c + 1)
        rec_a(c)
        if nxt:
            for step in range(half):
                solve_step(c + 1, step)
        rec_b(c)
        if nxt:
            for step in range(half, nsteps):
                solve_step(c + 1, step)
    outs = []
    for hh in range(hpb):
        s_ref[hh] = s[hh]
        o = jnp.concatenate(o_chunks[hh], axis=0)
        outs.append(o * lax.rsqrt(jnp.mean(o * o, axis=-1, keepdims=True) + NORM_EPS) * nw_ref[...])
    y = jnp.concatenate(outs, axis=1)
    o_ref[...] = (y * _silu(z_ref[...].astype(F32))).astype(o_ref.dtype)


def gdn_mixer(qkv_raw, z_a, a4, b4, conv_w, a_log, dt_bias, norm_w, *, bsz, seq, tt=256, hpb=GDN_V_HEADS):
    n = bsz * seq
    nt = seq // tt
    hb = tt // CONV_HALO
    rep = GDN_V_HEADS // GDN_QK_HEADS
    nhb = GDN_V_HEADS // hpb
    qw = hpb // rep * GDN_HEAD
    vw = hpb * GDN_HEAD
    q0, k0, v0 = 0, GDN_QK_DIM // qw, 2 * GDN_QK_DIM // vw

    def cur(width, c0):
        return pl.BlockSpec((tt, width), lambda b, h, t, *_: (b * nt + t, c0 + h))

    def prev(width, c0):
        return pl.BlockSpec((CONV_HALO, width),
                            lambda b, h, t, *_: (jnp.maximum((b * nt + t) * hb - 1, 0), c0 + h))

    def wspec(width, c0):
        return pl.BlockSpec((CONV_K, width), lambda b, h, t, *_: (0, c0 + h))

    small = pl.BlockSpec((None, hpb, None, tt // CHUNK, CHUNK), lambda b, h, t, *_: (b, h, t, 0, 0))
    a4 = a4.reshape(bsz, GDN_V_HEADS, nt, tt // CHUNK, CHUNK)
    b4 = b4.reshape(bsz, GDN_V_HEADS, nt, tt // CHUNK, CHUNK)
    grid_spec = pltpu.PrefetchScalarGridSpec(
        num_scalar_prefetch=2,
        grid=(bsz, nhb, nt),
        in_specs=[prev(qw, q0), cur(qw, q0), prev(qw, k0), cur(qw, k0), prev(vw, v0), cur(vw, v0),
                  cur(vw, 0),
                  wspec(qw, q0), wspec(qw, k0), wspec(vw, v0), small, small,
                  pl.BlockSpec((1, GDN_HEAD), lambda b, h, t, *_: (0, 0))],
        out_specs=cur(vw, 0),
        scratch_shapes=[pltpu.VMEM((hpb, GDN_HEAD, GDN_HEAD), F32)],
    )
    return pl.pallas_call(
        functools.partial(_gdn_kernel, tt=tt, hpb=hpb),
        grid_spec=grid_spec,
        out_shape=jax.ShapeDtypeStruct((n, GDN_V_DIM), BF16),
        compiler_params=_cparams(("parallel", "parallel", "arbitrary")),
        name="gdn_mixer",
    )(a_log, dt_bias, qkv_raw, qkv_raw, qkv_raw, qkv_raw, qkv_raw, qkv_raw, z_a,
      conv_w, conv_w, conv_w, a4, b4, norm_w.reshape(1, GDN_HEAD))


def _ssd_kernel(xp_ref, x_ref, bp_ref, b_ref, cp_ref, c_ref, z_ref,
                wx_ref, wb_ref, wc_ref, bx_ref, bb_ref, bc_ref, dt_ref, alog_ref, dtb_ref, dskip_ref, nw_ref,
                o_ref,
                h_ref, *, tt):
    t = pl.program_id(2)
    first = t == 0
    nck = tt // CHUNK
    pw = 2 * SSM_HEAD_DIM
    npair = SSM_GROUP_DIM // pw

    @pl.when(first)
    def _():
        h_ref[...] = jnp.zeros_like(h_ref)

    xs = _silu(_causal_conv(xp_ref, x_ref, wx_ref, first) + bx_ref[...])
    bm = _silu(_causal_conv(bp_ref, b_ref, wb_ref, first) + bb_ref[...])
    cm = _silu(_causal_conv(cp_ref, c_ref, wc_ref, first) + bc_ref[...])

    dt_rows = _softplus(dt_ref[...] + dtb_ref[...])
    adt_rows = -jnp.exp(alog_ref[...]) * dt_rows
    r2 = lax.broadcasted_iota(jnp.int32, (pw, pw), 0)
    c2 = lax.broadcasted_iota(jnp.int32, (pw, pw), 1)
    same_head = (r2 // CHUNK) == (c2 // CHUNK)
    cum_tot = jnp.concatenate([(same_head & (r2 <= c2)).astype(F32), same_head.astype(F32)], axis=1)
    adt_pc = jnp.concatenate([adt_rows[:, p * pw:(p + 1) * pw] for p in range(npair)], axis=0)
    ct = _dot_hi(adt_pc, cum_tot)
    acs_pc, tot_pc = ct[:, :pw], ct[:, pw:]

    li = lax.broadcasted_iota(jnp.int32, (CHUNK, pw), 0)
    ji = lax.broadcasted_iota(jnp.int32, (CHUNK, pw), 1)
    lo_half = ji < CHUNK
    pick_a = ji == li
    pick_b = ji == li + CHUNK
    causal2 = li >= jnp.where(lo_half, ji, ji - CHUNK)
    rb = lax.broadcasted_iota(jnp.int32, (pw, pw), 0)
    cb_ = lax.broadcasted_iota(jnp.int32, (pw, pw), 1)
    blockdiag = (rb < CHUNK) == (cb_ < CHUNK)

    def pair_col(row):
        a = jnp.sum(jnp.where(pick_a, row, 0.0), axis=-1, keepdims=True)
        b = jnp.sum(jnp.where(pick_b, row, 0.0), axis=-1, keepdims=True)
        return jnp.where(lo_half, a, b)

    hstate = h_ref[...]
    ys = []
    for c in range(nck):
        sl = slice(c * CHUNK, (c + 1) * CHUNK)
        cmc, bmc = cm[sl], bm[sl]
        cb2 = _dot_nt(cmc, jnp.concatenate([bmc, bmc], axis=0))
        yd_parts, eacs_parts, xdec_parts, eal_parts = [], [], [], []
        for p in range(npair):
            row = p * nck + c
            acs_row = acs_pc[row:row + 1, :]
            tot_row = tot_pc[row:row + 1, :]
            acs_col = pair_col(acs_row)
            dt_col = pair_col(dt_rows[c:c + 1, p * pw:(p + 1) * pw])
            lmat = jnp.where(causal2, jnp.exp(acs_col - acs_row), 0.0)
            xdt = xs[sl, p * pw:(p + 1) * pw] * dt_col
            xdt2 = jnp.where(blockdiag, jnp.concatenate([xdt, xdt], axis=0), 0.0)
            yd_parts.append(_dot(cb2 * lmat, xdt2))
            eacs_parts.append(jnp.exp(acs_col))
            xdec_parts.append(xdt * jnp.exp(tot_row - acs_col))
            eal_parts.append(jnp.exp(tot_row))
        upd = _dot_tn(bmc, jnp.concatenate(xdec_parts, axis=1))
        y_off = _dot(cmc, hstate)
        ys.append(jnp.concatenate(yd_parts, axis=1) + y_off * jnp.concatenate(eacs_parts, axis=1))
        hstate = hstate * jnp.concatenate(eal_parts, axis=1) + upd
    h_ref[...] = hstate
    y = jnp.concatenate(ys, axis=0) + xs * dskip_ref[...]
    y = y * _silu(z_ref[...].astype(F32))
    y = y * lax.rsqrt(jnp.mean(y * y, axis=-1, keepdims=True) + SSM_NORM_EPS) * nw_ref[...]
    o_ref[...] = y.astype(o_ref.dtype)


def ssd_mixer(xbc_raw, z_b, dt_raw, conv_w, conv_b, a_log, dt_bias, d_skip, norm_w, *, bsz, seq, tt=256):
    n = bsz * seq
    nt = seq // tt
    nck = tt // CHUNK
    hb = tt // CONV_HALO
    gd = SSM_GROUP_DIM
    st = SSM_STATE
    x_blocks = SSM_D_INNER // st
    dtl = jnp.transpose(dt_raw.reshape(bsz, seq // CHUNK, CHUNK, SSM_HEADS), (0, 1, 3, 2))
    dtl = dtl.reshape(bsz, nt, nck, SSM_D_INNER)
    per_head = lambda v: jnp.repeat(v, SSM_HEAD_DIM).reshape(1, SSM_D_INNER)

    def cur(width, colf):
        return pl.BlockSpec((tt, width), lambda b, g, t, *_: (b * nt + t, colf(g)))

    def prev(width, colf):
        return pl.BlockSpec((CONV_HALO, width),
                            lambda b, g, t, *_: (jnp.maximum((b * nt + t) * hb - 1, 0), colf(g)))

    def rowspec(rows, width, colf):
        return pl.BlockSpec((rows, width), lambda b, g, t, *_: (0, colf(g)))

    xcol = lambda g: g
    bcol = lambda g: x_blocks + g
    ccol = lambda g: x_blocks + SSM_GROUPS + g
    cb2 = conv_b.reshape(1, SSM_CONV_DIM)
    return pl.pallas_call(
        functools.partial(_ssd_kernel, tt=tt),
        grid=(bsz, SSM_GROUPS, nt),
        in_specs=[prev(gd, xcol), cur(gd, xcol), prev(st, bcol), cur(st, bcol), prev(st, ccol), cur(st, ccol),
                  cur(gd, xcol),
                  rowspec(CONV_K, gd, xcol), rowspec(CONV_K, st, bcol), rowspec(CONV_K, st, ccol),
                  rowspec(1, gd, xcol), rowspec(1, st, bcol), rowspec(1, st, ccol),
                  pl.BlockSpec((None, None, nck, gd), lambda b, g, t: (b, t, 0, g)),
                  rowspec(1, gd, xcol), rowspec(1, gd, xcol), rowspec(1, gd, xcol), rowspec(1, gd, xcol)],
        out_specs=cur(gd, xcol),
        out_shape=jax.ShapeDtypeStruct((n, SSM_D_INNER), BF16),
        scratch_shapes=[pltpu.VMEM((SSM_STATE, gd), F32)],
        compiler_params=_cparams(("parallel", "parallel", "arbitrary")),
        name="ssd_mixer",
    )(xbc_raw, xbc_raw, xbc_raw, xbc_raw, xbc_raw, xbc_raw, z_b,
      conv_w, conv_w, conv_w, cb2, cb2, cb2, dtl, per_head(a_log), per_head(dt_bias), per_head(d_skip),
      norm_w.reshape(1, SSM_D_INNER))


def _merge_kernel(h_ref, oa_ref, ob_ref, ga_ref, gb_ref, wa_ref, wb_ref, wo_ref, nw_ref, hn_ref, u_ref):
    a = jnp.dot(oa_ref[...], wa_ref[...], preferred_element_type=F32)
    b = jnp.dot(ob_ref[...], wb_ref[...], preferred_element_type=F32)
    mixed = jax.nn.sigmoid(ga_ref[...].astype(F32)) * a + jax.nn.sigmoid(gb_ref[...].astype(F32)) * b
    hn = h_ref[...] + jnp.dot(mixed.astype(BF16), wo_ref[...], preferred_element_type=F32)
    hn_ref[...] = hn
    y = hn * lax.rsqrt(jnp.mean(hn * hn, axis=-1, keepdims=True) + NORM_EPS) * nw_ref[...]
    u_ref[...] = _pack_pairs(y) if u_ref.dtype == jnp.uint32 else y.astype(u_ref.dtype)


def merge_out(h, oa, ob, ga, gb, wa, wb, wo, next_norm_w, packed_u, tm=512):
    n, d = h.shape
    u_sds = jax.ShapeDtypeStruct((n, d // 2), jnp.uint32) if packed_u else jax.ShapeDtypeStruct((n, d), BF16)
    tm = min(tm, n)
    row = lambda width: pl.BlockSpec((tm, width), lambda i: (i, 0))
    full = lambda r, c: pl.BlockSpec((r, c), lambda i: (0, 0))
    return pl.pallas_call(
        _merge_kernel,
        grid=(n // tm,),
        in_specs=[row(d), row(GDN_V_DIM), row(SSM_D_INNER), row(d), row(d),
                  full(GDN_V_DIM, d), full(SSM_D_INNER, d), full(d, d), full(1, d)],
        out_specs=[row(d), row(u_sds.shape[1])],
        out_shape=[jax.ShapeDtypeStruct((n, d), F32), u_sds],
        compiler_params=_cparams(("parallel",)),
        name="merge_out",
    )(h, oa, ob, ga, gb, wa, wb, wo, next_norm_w.reshape(1, d))


def _ffn_kernel(h_ref, u_ref, wg_ref, wu_ref, wd_ref, nw_ref, hn_ref, un_ref, acc_ref):
    f = pl.program_id(1)

    @pl.when(f == 0)
    def _():
        acc_ref[...] = h_ref[...]

    u = u_ref[...]
    hid = _silu(jnp.dot(u, wg_ref[...], preferred_element_type=F32)) * jnp.dot(
        u, wu_ref[...], preferred_element_type=F32)
    acc_ref[...] += jnp.dot(hid.astype(BF16), wd_ref[...], preferred_element_type=F32)

    @pl.when(f == pl.num_programs(1) - 1)
    def _():
        hn = acc_ref[...]
        hn_ref[...] = hn
        y = hn * lax.rsqrt(jnp.mean(hn * hn, axis=-1, keepdims=True) + NORM_EPS)
        un_ref[...] = (y * nw_ref[...]).astype(un_ref.dtype)


def dense_ffn(h, u, wg, wu, wd, next_norm_w, tm=1024, tf=256):
    n, d = h.shape
    ff = wg.shape[1]
    tm = min(tm, n)
    return pl.pallas_call(
        _ffn_kernel,
        grid=(n // tm, ff // tf),
        in_specs=[pl.BlockSpec((tm, d), lambda i, f: (i, 0)), pl.BlockSpec((tm, d), lambda i, f: (i, 0)),
                  pl.BlockSpec((d, tf), lambda i, f: (0, f)), pl.BlockSpec((d, tf), lambda i, f: (0, f)),
                  pl.BlockSpec((tf, d), lambda i, f: (f, 0)), pl.BlockSpec((1, d), lambda i, f: (0, 0))],
        out_specs=[pl.BlockSpec((tm, d), lambda i, f: (i, 0)), pl.BlockSpec((tm, d), lambda i, f: (i, 0))],
        out_shape=[jax.ShapeDtypeStruct((n, d), F32), jax.ShapeDtypeStruct((n, d), BF16)],
        scratch_shapes=[pltpu.VMEM((tm, d), F32)],
        compiler_params=_cparams(("parallel", "arbitrary")),
        name="dense_ffn",
    )(h, u, wg, wu, wd, next_norm_w.reshape(1, d))


def _router_kernel(u_ref, w_ref, o_ref):
    logits = _dot_hi(_unpack_pairs(u_ref[...]), w_ref[...])
    lane = lax.broadcasted_iota(jnp.int32, logits.shape, 1)
    neg = jnp.float32(-3.0e38)
    logits = jnp.where(lane < N_EXPERTS, logits, neg)
    m1 = jnp.max(logits, axis=-1, keepdims=True)
    i1 = jnp.min(jnp.where(logits == m1, lane, 2 * N_EXPERTS), axis=-1, keepdims=True)
    rest = jnp.where(lane == i1, neg, logits)
    m2 = jnp.max(rest, axis=-1, keepdims=True)
    i2 = jnp.min(jnp.where(rest == m2, lane, 2 * N_EXPERTS), axis=-1, keepdims=True)
    e2 = jnp.exp(m2 - m1)
    g1 = 1.0 / (1.0 + e2)
    g2 = e2 / (1.0 + e2)
    out = jnp.where(lane == 0, i1.astype(F32), 0.0)
    out = jnp.where(lane == 1, i2.astype(F32), out)
    out = jnp.where(lane == 2, g1, out)
    out = jnp.where(lane == 3, g2, out)
    o_ref[...] = out


def router(u, w_pad, tm=1024):
    n, half = u.shape
    tm = min(tm, n)
    return pl.pallas_call(
        _router_kernel,
        grid=(n // tm,),
        in_specs=[pl.BlockSpec((tm, half), lambda i: (i, 0)), pl.BlockSpec((2 * half, 128), lambda i: (0, 0))],
        out_specs=pl.BlockSpec((tm, 128), lambda i: (i, 0)),
        out_shape=jax.ShapeDtypeStruct((n, 128), F32),
        compiler_params=_cparams(("parallel",)),
        name="router",
    )(u, w_pad)


def sc_gather_rows(table, idx, chunk=SC_GATHER_CHUNK):
    v, d = table.shape
    b = idx.shape[0]
    nw = SC_CORES * SC_SUBCORES
    per_w = b // nw
    assert per_w * nw == b and per_w % chunk == 0 and chunk % 8 == 0 and chunk <= 128
    mesh = plsc.VectorSubcoreMesh(core_axis_name="c", subcore_axis_name="s")

    @functools.partial(
        pl.kernel, mesh=mesh,
        out_type=jax.ShapeDtypeStruct((b, d), table.dtype),
        scratch_types=[pltpu.VMEM((chunk,), jnp.int32), pltpu.VMEM((chunk, d), table.dtype),
                       pltpu.SemaphoreType.DMA],
    )
    def gather_kernel(table_hbm, idx_hbm, out_hbm, idx_v, rows_v, sem):
        wid = lax.axis_index("s") * SC_CORES + lax.axis_index("c")
        base = wid * per_w

        @pl.loop(0, per_w // chunk)
        def _(j):
            off = pl.multiple_of(base + j * chunk, 8)
            pltpu.sync_copy(idx_hbm.at[pl.ds(off, chunk)], idx_v)
            pltpu.async_copy(table_hbm.at[idx_v], rows_v, sem).wait()
            pltpu.sync_copy(rows_v, out_hbm.at[pl.ds(off, chunk)])

    return gather_kernel(table, idx)


def _expert_changed(be_ref, i):
    return jnp.logical_or(i == 0, be_ref[i] != be_ref[jnp.maximum(i - 1, 0)])


def _moe_up_kernel(be_ref, x_ref, wg_ref, wu_ref, o_ref, wg_bf, wu_bf):
    @pl.when(_expert_changed(be_ref, pl.program_id(1)))
    def _():
        wg_bf[...] = wg_ref[...].astype(BF16)
        wu_bf[...] = wu_ref[...].astype(BF16)

    x = _unpack_pairs(x_ref[...]).astype(BF16)
    hid = _silu(jnp.dot(x, wg_bf[...], preferred_element_type=F32)) * jnp.dot(
        x, wu_bf[...], preferred_element_type=F32)
    o_ref[...] = hid.astype(o_ref.dtype)


def moe_up(block_e, xb, wg, wu, tf=1792):
    ns = xb.shape[0]
    d, ff = wg.shape[1], wg.shape[2]
    nb = ns // MOE_BLOCK
    grid_spec = pltpu.PrefetchScalarGridSpec(
        num_scalar_prefetch=1,
        grid=(ff // tf, nb),
        in_specs=[pl.BlockSpec((MOE_BLOCK, d // 2), lambda f, i, be: (i, 0)),
                  pl.BlockSpec((None, d, tf), lambda f, i, be: (be[i], 0, f)),
                  pl.BlockSpec((None, d, tf), lambda f, i, be: (be[i], 0, f))],
        out_specs=pl.BlockSpec((MOE_BLOCK, tf), lambda f, i, be: (i, f)),
        scratch_shapes=[pltpu.VMEM((d, tf), BF16), pltpu.VMEM((d, tf), BF16)],
    )
    return pl.pallas_call(
        _moe_up_kernel,
        grid_spec=grid_spec,
        out_shape=jax.ShapeDtypeStruct((ns, ff), BF16),
        compiler_params=_cparams(("arbitrary", "arbitrary")),
        name="moe_up",
    )(block_e, xb, wg, wu)


def _moe_down_kernel(be_ref, hid_ref, wd_ref, o_ref, wd_bf):
    @pl.when(_expert_changed(be_ref, pl.program_id(0)))
    def _():
        wd_bf[...] = wd_ref[...].astype(BF16)

    o_ref[...] = _pack_pairs(jnp.dot(hid_ref[...], wd_bf[...], preferred_element_type=F32))


def moe_down(block_e, hid, wd):
    ns, ff = hid.shape
    d = wd.shape[2]
    nb = ns // MOE_BLOCK
    grid_spec = pltpu.PrefetchScalarGridSpec(
        num_scalar_prefetch=1,
        grid=(nb,),
        in_specs=[pl.BlockSpec((MOE_BLOCK, ff), lambda i, be: (i, 0)),
                  pl.BlockSpec((None, ff, d), lambda i, be: (be[i], 0, 0))],
        out_specs=pl.BlockSpec((MOE_BLOCK, d // 2), lambda i, be: (i, 0)),
        scratch_shapes=[pltpu.VMEM((ff, d), BF16)],
    )
    return pl.pallas_call(
        _moe_down_kernel,
        grid_spec=grid_spec,
        out_shape=jax.ShapeDtypeStruct((ns, d // 2), jnp.uint32),
        compiler_params=_cparams(("arbitrary",)),
        name="moe_down",
    )(block_e, hid, wd)


def _final_kernel(h_ref, y0_ref, y1_ref, r_ref, nw_ref, o_ref):
    gates = r_ref[...]
    hn = (h_ref[...] + gates[:, TOP_K:TOP_K + 1] * _unpack_pairs(y0_ref[...])
          + gates[:, TOP_K + 1:TOP_K + 2] * _unpack_pairs(y1_ref[...]))
    y = hn * lax.rsqrt(jnp.mean(hn * hn, axis=-1, keepdims=True) + NORM_EPS)
    o_ref[...] = y * nw_ref[...]


def final_combine(h, yk, r, norm_w, tm=1024):
    n, d = h.shape
    tm = min(tm, n)
    nblk = n // tm
    row = pl.BlockSpec((tm, d), lambda i: (i, 0))
    return pl.pallas_call(
        _final_kernel,
        grid=(nblk,),
        in_specs=[row, pl.BlockSpec((tm, d // 2), lambda i: (i, 0)),
                  pl.BlockSpec((tm, d // 2), lambda i: (i + nblk, 0)),
                  pl.BlockSpec((tm, 128), lambda i: (i, 0)), pl.BlockSpec((1, d), lambda i: (0, 0))],
        out_specs=row,
        out_shape=jax.ShapeDtypeStruct((n, d), F32),
        compiler_params=_cparams(("parallel",)),
        name="final_combine",
    )(h, yk, yk, r, norm_w.reshape(1, d))


def _chunk_rows(x, bsz, seq):
    hh = x.shape[1]
    return jnp.transpose(x.reshape(bsz, seq, hh), (0, 2, 1)).reshape(bsz, hh, seq // CHUNK, CHUNK)


def hybrid_mixer_layer(h, u, bsz, seq, w_in, gdn_conv_w, gdn_A_log, gdn_dt_bias, gdn_norm_w, gdn_proj,
                       ssm_conv_w, ssm_conv_b, ssm_A_log, ssm_dt_bias, ssm_D, ssm_norm_w, ssm_proj, w_out,
                       next_norm_w, packed_u):
    c0 = 0
    c1 = c0 + GDN_CONV_DIM
    c2 = c1 + GDN_V_DIM
    c3 = c2 + GDN_V_HEADS
    c4 = c3 + GDN_V_HEADS
    c5 = c4 + SSM_D_INNER
    c6 = c5 + SSM_CONV_DIM
    c7 = c6 + SSM_HEADS
    c8 = c7 + D_MODEL
    wb = w_in.astype(BF16)
    qkv_raw = matmul(u, wb[:, c0:c1], BF16, name="proj_qkv")
    z_a = matmul(u, wb[:, c1:c2], BF16, name="proj_za")
    z_b = matmul(u, wb[:, c4:c5], BF16, name="proj_zb")
    xbc_raw = matmul(u, wb[:, c5:c6], BF16, name="proj_xbc")
    gate_a = matmul(u, wb[:, c7:c8], BF16, name="proj_ga")
    gate_b = matmul(u, wb[:, c8:], BF16, name="proj_gb")
    n_small = 2 * GDN_V_HEADS + SSM_HEADS
    w_small = jnp.concatenate([w_in[:, c2:c4], w_in[:, c6:c7]], axis=1)
    w_small = jnp.pad(w_small, ((0, 0), (0, 128 - n_small)))
    small = matmul(u, w_small, F32, tn=128, full_precision=True, name="proj_small")
    a4 = _chunk_rows(small[:, :GDN_V_HEADS], bsz, seq)
    b4 = _chunk_rows(small[:, GDN_V_HEADS:2 * GDN_V_HEADS], bsz, seq)
    dt_raw = small[:, 2 * GDN_V_HEADS:n_small]

    oa = gdn_mixer(qkv_raw, z_a, a4, b4, gdn_conv_w, gdn_A_log, gdn_dt_bias, gdn_norm_w, bsz=bsz, seq=seq)
    ob = ssd_mixer(xbc_raw, z_b, dt_raw, ssm_conv_w, ssm_conv_b, ssm_A_log, ssm_dt_bias, ssm_D, ssm_norm_w,
                   bsz=bsz, seq=seq)
    return merge_out(h, oa, ob, gate_a, gate_b, gdn_proj.astype(BF16), ssm_proj.astype(BF16),
                     w_out.astype(BF16), next_norm_w, packed_u)


def moe_layer(h, u, router_w, w_gate, w_up, w_down, final_norm_w):
    n, d = h.shape
    n_assign = n * TOP_K
    r = router(u, jnp.pad(router_w, ((0, 0), (0, 128 - N_EXPERTS))))
    top_idx = r[:, :TOP_K].astype(jnp.int32)
    flat_e = top_idx.reshape(-1)
    onehot = (flat_e[:, None] == jnp.arange(N_EXPERTS)[None, :]).astype(jnp.int32)
    csum = jnp.cumsum(onehot, axis=0)
    counts = csum[-1]
    rank = jnp.sum((csum - onehot) * onehot, axis=1)
    padded = (counts + MOE_BLOCK - 1) // MOE_BLOCK * MOE_BLOCK
    ends = jnp.cumsum(padded)
    pstart = ends - padded
    dest = (pstart[flat_e] + rank).astype(jnp.int32)
    n_blocks = -(-n_assign // MOE_BLOCK) + N_EXPERTS
    n_slots = n_blocks * MOE_BLOCK
    tok = (jnp.arange(n_assign, dtype=jnp.int32) // TOP_K)
    slot_tok = jnp.zeros((n_slots,), jnp.int32).at[dest].set(tok, unique_indices=True)
    block_start = jnp.arange(n_blocks, dtype=jnp.int32) * MOE_BLOCK
    block_e = jnp.minimum(jnp.sum(block_start[:, None] >= ends[None, :], axis=1), N_EXPERTS - 1).astype(jnp.int32)
    xb = sc_gather_rows(u, slot_tok)
    hid = moe_up(block_e, xb, w_gate, w_up)
    yb = moe_down(block_e, hid, w_down)
    yk = sc_gather_rows(yb, dest.reshape(n, TOP_K).T.reshape(-1))
    return final_combine(h, yk, r, final_norm_w)


def kernel(x, mix_norm_w, w_in, gdn_conv_w, gdn_A_log, gdn_dt_bias, gdn_norm_w, gdn_proj, ssm_conv_w, ssm_conv_b, ssm_A_log, ssm_dt_bias, ssm_D, ssm_norm_w, ssm_proj, w_out, ffn_norm_w, dense_w_gate, dense_w_up, dense_w_down, router_w, moe_w_gate, moe_w_up, moe_w_down, final_norm_w):
    bsz, seq, d = x.shape
    assert d == D_MODEL and w_in.shape[0] == 2, "dense-FFN layer followed by a final MoE layer"
    h = x.reshape(bsz * seq, d)
    u = rmsnorm(h, mix_norm_w[0], BF16)

    def mixer(layer, h, u, packed_u):
        return hybrid_mixer_layer(
            h, u, bsz, seq, w_in[layer], gdn_conv_w[layer], gdn_A_log[layer], gdn_dt_bias[layer],
            gdn_norm_w[layer], gdn_proj[layer], ssm_conv_w[layer], ssm_conv_b[layer], ssm_A_log[layer],
            ssm_dt_bias[layer], ssm_D[layer], ssm_norm_w[layer], ssm_proj[layer], w_out[layer],
            ffn_norm_w[layer], packed_u)

    h, u = mixer(0, h, u, False)
    h, u = dense_ffn(h, u, dense_w_gate[0].astype(BF16), dense_w_up[0].astype(BF16),
                     dense_w_down[0].astype(BF16), mix_norm_w[1])
    h, u = mixer(1, h, u, True)
    out = moe_layer(h, u, router_w[0], moe_w_gate[0], moe_w_up[0], moe_w_down[0], final_norm_w)
    return out.reshape(bsz, seq, d)
```

```python
import functools

import jax
import jax.numpy as jnp
from jax import lax
from jax.experimental import pallas as pl
from jax.experimental.pallas import tpu as pltpu
from jax.experimental.pallas import tpu_sc as plsc

F32 = jnp.float32
BF16 = jnp.bfloat16

D_MODEL = 1024
CONV_K = 4
CHUNK = 64
GDN_QK_HEADS = 4
GDN_V_HEADS = 8
GDN_HEAD = 128
GDN_QK_DIM = GDN_QK_HEADS * GDN_HEAD
GDN_V_DIM = GDN_V_HEADS * GDN_HEAD
GDN_CONV_DIM = 2 * GDN_QK_DIM + GDN_V_DIM
SSM_D_INNER = 2048
SSM_HEAD_DIM = 64
SSM_HEADS = SSM_D_INNER // SSM_HEAD_DIM
SSM_GROUPS = 4
SSM_HPG = SSM_HEADS // SSM_GROUPS
SSM_GROUP_DIM = SSM_D_INNER // SSM_GROUPS
SSM_STATE = 128
SSM_CONV_DIM = SSM_D_INNER + 2 * SSM_GROUPS * SSM_STATE
N_EXPERTS = 8
TOP_K = 2
MOE_BLOCK = 512
NORM_EPS = 1e-6
SSM_NORM_EPS = 1e-5
CONV_HALO = 16

VMEM_LIMIT = 56 * 1024 * 1024
SC_CORES = 2
SC_SUBCORES = 16
SC_GATHER_CHUNK = 128


def _cparams(sem):
    return pltpu.CompilerParams(dimension_semantics=sem, vmem_limit_bytes=VMEM_LIMIT)


def _silu(x):
    h = 0.5 * x
    return h + h * jnp.tanh(h)


def _softplus(x):
    return jnp.maximum(x, 0.0) + jnp.log1p(jnp.exp(-jnp.abs(x)))


def _dot(a, b):
    return jnp.dot(a.astype(BF16), b.astype(BF16), preferred_element_type=F32)


def _dot_nt(a, b):
    return lax.dot_general(a.astype(BF16), b.astype(BF16), (((1,), (1,)), ((), ())),
                           preferred_element_type=F32)


def _dot_tn(a, b):
    return lax.dot_general(a.astype(BF16), b.astype(BF16), (((0,), (0,)), ((), ())),
                           preferred_element_type=F32)


def _dot_hi(a, b):
    return jnp.dot(a, b, preferred_element_type=F32, precision=lax.Precision.HIGHEST)


def _pack_pairs(x):
    half = x.shape[1] // 2
    bits = lax.bitcast_convert_type(x.astype(BF16).astype(F32), jnp.uint32)
    return (bits[:, :half] >> 16) | (bits[:, half:] & jnp.uint32(0xFFFF0000))


def _unpack_pairs(p):
    lo = lax.bitcast_convert_type(p << 16, F32)
    hi = lax.bitcast_convert_type(p & jnp.uint32(0xFFFF0000), F32)
    return jnp.concatenate([lo, hi], axis=1)


def _rmsnorm_kernel(x_ref, w_ref, o_ref):
    x = x_ref[...]
    y = x * lax.rsqrt(jnp.mean(x * x, axis=-1, keepdims=True) + NORM_EPS)
    o_ref[...] = (y * w_ref[...]).astype(o_ref.dtype)


def rmsnorm(x, w, out_dtype, tm=1024):
    n, d = x.shape
    tm = min(tm, n)
    return pl.pallas_call(
        _rmsnorm_kernel,
        grid=(n // tm,),
        in_specs=[pl.BlockSpec((tm, d), lambda i: (i, 0)), pl.BlockSpec((1, d), lambda i: (0, 0))],
        out_specs=pl.BlockSpec((tm, d), lambda i: (i, 0)),
        out_shape=jax.ShapeDtypeStruct((n, d), out_dtype),
        compiler_params=_cparams(("parallel",)),
        name="rmsnorm",
    )(x, w.reshape(1, d))


def _split3(w):
    w_hi = w.astype(BF16)
    w_mid = (w - w_hi.astype(F32)).astype(BF16)
    w_lo = (w - w_hi.astype(F32) - w_mid.astype(F32)).astype(BF16)
    return jnp.concatenate([w_hi, w_mid, w_lo], axis=1)


def _dot_parts(x, w_parts, parts):
    r = jnp.dot(x, w_parts, preferred_element_type=F32)
    tn = w_parts.shape[1] // parts
    acc = r[:, :tn]
    for p in range(1, parts):
        acc = acc + r[:, p * tn:(p + 1) * tn]
    return acc


def _matmul_kernel(x_ref, w_ref, o_ref, *, parts):
    o_ref[...] = _dot_parts(x_ref[...], w_ref[...], parts).astype(o_ref.dtype)


def matmul(x, w, out_dtype, tm=1024, tn=1024, full_precision=False, name="matmul"):
    n, k = x.shape
    m = w.shape[1]
    tm = min(tm, n)
    tn = min(tn, m)
    parts = 1
    if full_precision:
        assert m == tn
        w = _split3(w)
        parts = 3
    return pl.pallas_call(
        functools.partial(_matmul_kernel, parts=parts),
        grid=(m // tn, n // tm),
        in_specs=[pl.BlockSpec((tm, k), lambda j, i: (i, 0)),
                  pl.BlockSpec((k, parts * tn), lambda j, i: (0, j))],
        out_specs=pl.BlockSpec((tm, tn), lambda j, i: (i, j)),
        out_shape=jax.ShapeDtypeStruct((n, m), out_dtype),
        compiler_params=_cparams(("parallel", "parallel")),
        name=name,
    )(x, w)


def _causal_conv(prev_ref, cur_ref, w_ref, first):
    prev = jnp.where(first, 0.0, prev_ref[...].astype(F32))
    x = jnp.concatenate([prev, cur_ref[...].astype(F32)], axis=0)
    assert CONV_K == 4
    x1 = pltpu.roll(x, 1, 0)
    near = x * w_ref[3:4, :] + x1 * w_ref[2:3, :]
    far = x * w_ref[1:2, :] + x1 * w_ref[0:1, :]
    return (near + pltpu.roll(far, 2, 0))[CONV_HALO:]


def _col_from_row(row, eye):
    return jnp.sum(jnp.where(eye, row, 0.0), axis=-1, keepdims=True)


def _gdn_kernel(alog_ref, dtb_ref,
                qp_ref, q_ref, kp_ref, k_ref, vp_ref, v_ref, z_ref,
                wq_ref, wk_ref, wv_ref, a_ref, b_ref, nw_ref,
                o_ref,
                s_ref, *, tt, hpb):
    hblk = pl.program_id(1)
    t = pl.program_id(2)
    first = t == 0
    nck = tt // CHUNK
    rep = GDN_V_HEADS // GDN_QK_HEADS
    dh = GDN_HEAD

    @pl.when(first)
    def _():
        s_ref[...] = jnp.zeros_like(s_ref)

    q_all = _silu(_causal_conv(qp_ref, q_ref, wq_ref, first))
    k_all = _silu(_causal_conv(kp_ref, k_ref, wk_ref, first))
    v_all = _silu(_causal_conv(vp_ref, v_ref, wv_ref, first))

    ri = lax.broadcasted_iota(jnp.int32, (CHUNK, CHUNK), 0)
    ci = lax.broadcasted_iota(jnp.int32, (CHUNK, CHUNK), 1)
    upper = (ri <= ci).astype(F32)
    eye = ri == ci
    causal = ri >= ci
    strict = ri > ci
    chunks = [slice(c * CHUNK, (c + 1) * CHUNK) for c in range(nck)]
    nsteps = CHUNK.bit_length() - 1

    qs, ks = [], []
    for j in range(hpb // rep):
        q = q_all[:, j * dh:(j + 1) * dh]
        k = k_all[:, j * dh:(j + 1) * dh]
        qs.append(q * (lax.rsqrt(jnp.sum(q * q, axis=-1, keepdims=True) + 1e-6) * (dh ** -0.5)))
        ks.append(k * lax.rsqrt(jnp.sum(k * k, axis=-1, keepdims=True) + 1e-6))

    heads = range(hpb)
    g_rows, beta_rows, gc_rows = [], [], []
    for hh in heads:
        head = hblk * hpb + hh
        neg_a = -jnp.exp(jnp.full((1, CHUNK), alog_ref[head], F32))
        g_rows.append(neg_a * _softplus(a_ref[hh] + dtb_ref[head]))
        beta_rows.append(jax.nn.sigmoid(b_ref[hh]))
        gc_rows.append(_dot_hi(g_rows[hh], upper))

    ps, xs, qkk, qes, egl = {}, {}, {}, {}, {}

    def local_prep(c):
        sl = chunks[c]
        kq = [_dot_nt(jnp.concatenate([ks[j][sl], qs[j][sl]], axis=0), ks[j][sl]) for j in range(hpb // rep)]
        for hh in heads:
            qc, kc = qs[hh // rep][sl], ks[hh // rep][sl]
            vc = v_all[sl, hh * dh:(hh + 1) * dh]
            gc_row = gc_rows[hh][c:c + 1, :]
            gc_col = _col_from_row(gc_row, eye)
            beta_col = _col_from_row(beta_rows[hh][c:c + 1, :], eye)
            g_last = jnp.sum(g_rows[hh][c:c + 1, :], axis=-1, keepdims=True)
            decay = jnp.where(causal, jnp.exp(gc_col - gc_row), 0.0)
            eg_col = jnp.exp(gc_col)
            kk, qk = kq[hh // rep][:CHUNK], kq[hh // rep][CHUNK:]
            ps[hh, c] = jnp.where(strict, kk * decay, 0.0) * beta_col
            xs[hh, c] = jnp.concatenate([vc * beta_col, kc * (beta_col * eg_col)], axis=1)
            k_dec = kc * jnp.exp(g_last - gc_col)
            qkk[hh, c] = jnp.concatenate([qk * decay, k_dec.T], axis=0)
            qes[hh, c] = qc * eg_col
            egl[hh, c] = jnp.exp(g_last)

    def solve_step(c, step):
        for hh in heads:
            u = (hh, c)
            if step + 1 < nsteps:
                r = _dot(ps[u], jnp.concatenate([xs[u], ps[u]], axis=1))
                ps[u] = r[:, 2 * dh:]
                r = r[:, :2 * dh]
            else:
                r = _dot(ps[u], xs[u])
            xs[u] = xs[u] - r if step == 0 else xs[u] + r

    s = [s_ref[hh] for hh in heads]
    v_new = {}
    o_chunks = [[] for _ in heads]

    def rec_a(c):
        for hh in heads:
            x = xs[hh, c]
            wq_s = _dot(jnp.concatenate([x[:, dh:], qes[hh, c]], axis=0), s[hh])
            v_new[hh] = x[:, :dh] - wq_s[:CHUNK]
            o_chunks[hh].append(wq_s[CHUNK:])

    def rec_b(c):
        for hh in heads:
            r = _dot(qkk[hh, c], v_new[hh])
            o_chunks[hh][c] = o_chunks[hh][c] + r[:CHUNK]
            s[hh] = s[hh] * egl[hh, c] + r[CHUNK:]

    half = nsteps // 2
    local_prep(0)
    for step in range(nsteps):
        solve_step(0, step)
    for c in range(nck):
        nxt = c + 1 < nck
        if nxt:
            local_prep(c + 1)
        rec_a(c)
        if nxt:
            for step in range(half):
                solve_step(c + 1, step)
        rec_b(c)
        if nxt:
            for step in range(half, nsteps):
                solve_step(c + 1, step)
    outs = []
    for hh in range(hpb):
        s_ref[hh] = s[hh]
        o = jnp.concatenate(o_chunks[hh], axis=0)
        outs.append(o * lax.rsqrt(jnp.mean(o * o, axis=-1, keepdims=True) + NORM_EPS) * nw_ref[...])
    y = jnp.concatenate(outs, axis=1)
    o_ref[...] = (y * _silu(z_ref[...].astype(F32))).astype(o_ref.dtype)


def gdn_mixer(qkv_raw, z_a, a4, b4, conv_w, a_log, dt_bias, norm_w, *, bsz, seq, tt=256, hpb=GDN_V_HEADS):
    n = bsz * seq
    nt = seq // tt
    hb = tt // CONV_HALO
    rep = GDN_V_HEADS // GDN_QK_HEADS
    nhb = GDN_V_HEADS // hpb
    qw = hpb // rep * GDN_HEAD
    vw = hpb * GDN_HEAD
    q0, k0, v0 = 0, GDN_QK_DIM // qw, 2 * GDN_QK_DIM // vw

    def cur(width, c0):
        return pl.BlockSpec((tt, width), lambda b, h, t, *_: (b * nt + t, c0 + h))

    def prev(width, c0):
        return pl.BlockSpec((CONV_HALO, width),
                            lambda b, h, t, *_: (jnp.maximum((b * nt + t) * hb - 1, 0), c0 + h))

    def wspec(width, c0):
        return pl.BlockSpec((CONV_K, width), lambda b, h, t, *_: (0, c0 + h))

    small = pl.BlockSpec((None, hpb, None, tt // CHUNK, CHUNK), lambda b, h, t, *_: (b, h, t, 0, 0))
    a4 = a4.reshape(bsz, GDN_V_HEADS, nt, tt // CHUNK, CHUNK)
    b4 = b4.reshape(bsz, GDN_V_HEADS, nt, tt // CHUNK, CHUNK)
    grid_spec = pltpu.PrefetchScalarGridSpec(
        num_scalar_prefetch=2,
        grid=(bsz, nhb, nt),
        in_specs=[prev(qw, q0), cur(qw, q0), prev(qw, k0), cur(qw, k0), prev(vw, v0), cur(vw, v0),
                  cur(vw, 0),
                  wspec(qw, q0), wspec(qw, k0), wspec(vw, v0), small, small,
                  pl.BlockSpec((1, GDN_HEAD), lambda b, h, t, *_: (0, 0))],
        out_specs=cur(vw, 0),
        scratch_shapes=[pltpu.VMEM((hpb, GDN_HEAD, GDN_HEAD), F32)],
    )
    return pl.pallas_call(
        functools.partial(_gdn_kernel, tt=tt, hpb=hpb),
        grid_spec=grid_spec,
        out_shape=jax.ShapeDtypeStruct((n, GDN_V_DIM), BF16),
        compiler_params=_cparams(("parallel", "parallel", "arbitrary")),
        name="gdn_mixer",
    )(a_log, dt_bias, qkv_raw, qkv_raw, qkv_raw, qkv_raw, qkv_raw, qkv_raw, z_a,
      conv_w, conv_w, conv_w, a4, b4, norm_w.reshape(1, GDN_HEAD))


def _ssd_kernel(xp_ref, x_ref, bp_ref, b_ref, cp_ref, c_ref, z_ref,
                wx_ref, wb_ref, wc_ref, bx_ref, bb_ref, bc_ref, dt_ref, alog_ref, dtb_ref, dskip_ref, nw_ref,
                o_ref,
                h_ref, *, tt):
    t = pl.program_id(2)
    first = t == 0
    nck = tt // CHUNK
    pw = 2 * SSM_HEAD_DIM
    npair = SSM_GROUP_DIM // pw

    @pl.when(first)
    def _():
        h_ref[...] = jnp.zeros_like(h_ref)

    xs = _silu(_causal_conv(xp_ref, x_ref, wx_ref, first) + bx_ref[...])
    bm = _silu(_causal_conv(bp_ref, b_ref, wb_ref, first) + bb_ref[...])
    cm = _silu(_causal_conv(cp_ref, c_ref, wc_ref, first) + bc_ref[...])

    dt_rows = _softplus(dt_ref[...] + dtb_ref[...])
    adt_rows = -jnp.exp(alog_ref[...]) * dt_rows
    r2 = lax.broadcasted_iota(jnp.int32, (pw, pw), 0)
    c2 = lax.broadcasted_iota(jnp.int32, (pw, pw), 1)
    same_head = (r2 // CHUNK) == (c2 // CHUNK)
    cum_tot = jnp.concatenate([(same_head & (r2 <= c2)).astype(F32), same_head.astype(F32)], axis=1)
    adt_pc = jnp.concatenate([adt_rows[:, p * pw:(p + 1) * pw] for p in range(npair)], axis=0)
    ct = _dot_hi(adt_pc, cum_tot)
    acs_pc, tot_pc = ct[:, :pw], ct[:, pw:]

    li = lax.broadcasted_iota(jnp.int32, (CHUNK, pw), 0)
    ji = lax.broadcasted_iota(jnp.int32, (CHUNK, pw), 1)
    lo_half = ji < CHUNK
    pick_a = ji == li
    pick_b = ji == li + CHUNK
    causal2 = li >= jnp.where(lo_half, ji, ji - CHUNK)
    rb = lax.broadcasted_iota(jnp.int32, (pw, pw), 0)
    cb_ = lax.broadcasted_iota(jnp.int32, (pw, pw), 1)
    blockdiag = (rb < CHUNK) == (cb_ < CHUNK)

    def pair_col(row):
        a = jnp.sum(jnp.where(pick_a, row, 0.0), axis=-1, keepdims=True)
        b = jnp.sum(jnp.where(pick_b, row, 0.0), axis=-1, keepdims=True)
        return jnp.where(lo_half, a, b)

    hstate = h_ref[...]
    ys = []
    for c in range(nck):
        sl = slice(c * CHUNK, (c + 1) * CHUNK)
        cmc, bmc = cm[sl], bm[sl]
        cb2 = _dot_nt(cmc, jnp.concatenate([bmc, bmc], axis=0))
        yd_parts, eacs_parts, xdec_parts, eal_parts = [], [], [], []
        for p in range(npair):
            row = p * nck + c
            acs_row = acs_pc[row:row + 1, :]
            tot_row = tot_pc[row:row + 1, :]
            acs_col = pair_col(acs_row)
            dt_col = pair_col(dt_rows[c:c + 1, p * pw:(p + 1) * pw])
            lmat = jnp.where(causal2, jnp.exp(acs_col - acs_row), 0.0)
            xdt = xs[sl, p * pw:(p + 1) * pw] * dt_col
            xdt2 = jnp.where(blockdiag, jnp.concatenate([xdt, xdt], axis=0), 0.0)
            yd_parts.append(_dot(cb2 * lmat, xdt2))
            eacs_parts.append(jnp.exp(acs_col))
            xdec_parts.append(xdt * jnp.exp(tot_row - acs_col))
            eal_parts.append(jnp.exp(tot_row))
        upd = _dot_tn(bmc, jnp.concatenate(xdec_parts, axis=1))
        y_off = _dot(cmc, hstate)
        ys.append(jnp.concatenate(yd_parts, axis=1) + y_off * jnp.concatenate(eacs_parts, axis=1))
        hstate = hstate * jnp.concatenate(eal_parts, axis=1) + upd
    h_ref[...] = hstate
    y = jnp.concatenate(ys, axis=0) + xs * dskip_ref[...]
    y = y * _silu(z_ref[...].astype(F32))
    y = y * lax.rsqrt(jnp.mean(y * y, axis=-1, keepdims=True) + SSM_NORM_EPS) * nw_ref[...]
    o_ref[...] = y.astype(o_ref.dtype)


def ssd_mixer(xbc_raw, z_b, dt_raw, conv_w, conv_b, a_log, dt_bias, d_skip, norm_w, *, bsz, seq, tt=256):
    n = bsz * seq
    nt = seq // tt
    nck = tt // CHUNK
    hb = tt // CONV_HALO
    gd = SSM_GROUP_DIM
    st = SSM_STATE
    x_blocks = SSM_D_INNER // st
    dtl = jnp.transpose(dt_raw.reshape(bsz, seq // CHUNK, CHUNK, SSM_HEADS), (0, 1, 3, 2))
    dtl = dtl.reshape(bsz, nt, nck, SSM_D_INNER)
    per_head = lambda v: jnp.repeat(v, SSM_HEAD_DIM).reshape(1, SSM_D_INNER)

    def cur(width, colf):
        return pl.BlockSpec((tt, width), lambda b, g, t, *_: (b * nt + t, colf(g)))

    def prev(width, colf):
        return pl.BlockSpec((CONV_HALO, width),
                            lambda b, g, t, *_: (jnp.maximum((b * nt + t) * hb - 1, 0), colf(g)))

    def rowspec(rows, width, colf):
        return pl.BlockSpec((rows, width), lambda b, g, t, *_: (0, colf(g)))

    xcol = lambda g: g
    bcol = lambda g: x_blocks + g
    ccol = lambda g: x_blocks + SSM_GROUPS + g
    cb2 = conv_b.reshape(1, SSM_CONV_DIM)
    return pl.pallas_call(
        functools.partial(_ssd_kernel, tt=tt),
        grid=(bsz, SSM_GROUPS, nt),
        in_specs=[prev(gd, xcol), cur(gd, xcol), prev(st, bcol), cur(st, bcol), prev(st, ccol), cur(st, ccol),
                  cur(gd, xcol),
                  rowspec(CONV_K, gd, xcol), rowspec(CONV_K, st, bcol), rowspec(CONV_K, st, ccol),
                  rowspec(1, gd, xcol), rowspec(1, st, bcol), rowspec(1, st, ccol),
                  pl.BlockSpec((None, None, nck, gd), lambda b, g, t: (b, t, 0, g)),
                  rowspec(1, gd, xcol), rowspec(1, gd, xcol), rowspec(1, gd, xcol), rowspec(1, gd, xcol)],
        out_specs=cur(gd, xcol),
        out_shape=jax.ShapeDtypeStruct((n, SSM_D_INNER), BF16),
        scratch_shapes=[pltpu.VMEM((SSM_STATE, gd), F32)],
        compiler_params=_cparams(("parallel", "parallel", "arbitrary")),
        name="ssd_mixer",
    )(xbc_raw, xbc_raw, xbc_raw, xbc_raw, xbc_raw, xbc_raw, z_b,
      conv_w, conv_w, conv_w, cb2, cb2, cb2, dtl, per_head(a_log), per_head(dt_bias), per_head(d_skip),
      norm_w.reshape(1, SSM_D_INNER))


def _merge_kernel(h_ref, oa_ref, ob_ref, ga_ref, gb_ref, wa_ref, wb_ref, wo_ref, nw_ref, hn_ref, u_ref):
    a = jnp.dot(oa_ref[...], wa_ref[...], preferred_element_type=F32)
    b = jnp.dot(ob_ref[...], wb_ref[...], preferred_element_type=F32)
    mixed = jax.nn.sigmoid(ga_ref[...].astype(F32)) * a + jax.nn.sigmoid(gb_ref[...].astype(F32)) * b
    hn = h_ref[...] + jnp.dot(mixed.astype(BF16), wo_ref[...], preferred_element_type=F32)
    hn_ref[...] = hn
    y = hn * lax.rsqrt(jnp.mean(hn * hn, axis=-1, keepdims=True) + NORM_EPS) * nw_ref[...]
    u_ref[...] = _pack_pairs(y) if u_ref.dtype == jnp.uint32 else y.astype(u_ref.dtype)


def merge_out(h, oa, ob, ga, gb, wa, wb, wo, next_norm_w, packed_u, tm=512):
    n, d = h.shape
    u_sds = jax.ShapeDtypeStruct((n, d // 2), jnp.uint32) if packed_u else jax.ShapeDtypeStruct((n, d), BF16)
    tm = min(tm, n)
    row = lambda width: pl.BlockSpec((tm, width), lambda i: (i, 0))
    full = lambda r, c: pl.BlockSpec((r, c), lambda i: (0, 0))
    return pl.pallas_call(
        _merge_kernel,
        grid=(n // tm,),
        in_specs=[row(d), row(GDN_V_DIM), row(SSM_D_INNER), row(d), row(d),
                  full(GDN_V_DIM, d), full(SSM_D_INNER, d), full(d, d), full(1, d)],
        out_specs=[row(d), row(u_sds.shape[1])],
        out_shape=[jax.ShapeDtypeStruct((n, d), F32), u_sds],
        compiler_params=_cparams(("parallel",)),
        name="merge_out",
    )(h, oa, ob, ga, gb, wa, wb, wo, next_norm_w.reshape(1, d))


def _ffn_kernel(h_ref, u_ref, wg_ref, wu_ref, wd_ref, nw_ref, hn_ref, un_ref, acc_ref):
    f = pl.program_id(1)

    @pl.when(f == 0)
    def _():
        acc_ref[...] = h_ref[...]

    u = u_ref[...]
    hid = _silu(jnp.dot(u, wg_ref[...], preferred_element_type=F32)) * jnp.dot(
        u, wu_ref[...], preferred_element_type=F32)
    acc_ref[...] += jnp.dot(hid.astype(BF16), wd_ref[...], preferred_element_type=F32)

    @pl.when(f == pl.num_programs(1) - 1)
    def _():
        hn = acc_ref[...]
        hn_ref[...] = hn
        y = hn * lax.rsqrt(jnp.mean(hn * hn, axis=-1, keepdims=True) + NORM_EPS)
        un_ref[...] = (y * nw_ref[...]).astype(un_ref.dtype)


def dense_ffn(h, u, wg, wu, wd, next_norm_w, tm=1024, tf=256):
    n, d = h.shape
    ff = wg.shape[1]
    tm = min(tm, n)
    return pl.pallas_call(
        _ffn_kernel,
        grid=(n // tm, ff // tf),
        in_specs=[pl.BlockSpec((tm, d), lambda i, f: (i, 0)), pl.BlockSpec((tm, d), lambda i, f: (i, 0)),
                  pl.BlockSpec((d, tf), lambda i, f: (0, f)), pl.BlockSpec((d, tf), lambda i, f: (0, f)),
                  pl.BlockSpec((tf, d), lambda i, f: (f, 0)), pl.BlockSpec((1, d), lambda i, f: (0, 0))],
        out_specs=[pl.BlockSpec((tm, d), lambda i, f: (i, 0)), pl.BlockSpec((tm, d), lambda i, f: (i, 0))],
        out_shape=[jax.ShapeDtypeStruct((n, d), F32), jax.ShapeDtypeStruct((n, d), BF16)],
        scratch_shapes=[pltpu.VMEM((tm, d), F32)],
        compiler_params=_cparams(("parallel", "arbitrary")),
        name="dense_ffn",
    )(h, u, wg, wu, wd, next_norm_w.reshape(1, d))


def _router_kernel(u_ref, w_ref, o_ref):
    logits = _dot_parts(_unpack_pairs(u_ref[...]).astype(BF16), w_ref[...], 3)
    lane = lax.broadcasted_iota(jnp.int32, logits.shape, 1)
    neg = jnp.float32(-3.0e38)
    logits = jnp.where(lane < N_EXPERTS, logits, neg)
    m1 = jnp.max(logits, axis=-1, keepdims=True)
    i1 = jnp.min(jnp.where(logits == m1, lane, 2 * N_EXPERTS), axis=-1, keepdims=True)
    rest = jnp.where(lane == i1, neg, logits)
    m2 = jnp.max(rest, axis=-1, keepdims=True)
    i2 = jnp.min(jnp.where(rest == m2, lane, 2 * N_EXPERTS), axis=-1, keepdims=True)
    e2 = jnp.exp(m2 - m1)
    g1 = 1.0 / (1.0 + e2)
    g2 = e2 / (1.0 + e2)
    out = jnp.where(lane == 0, i1.astype(F32), 0.0)
    out = jnp.where(lane == 1, i2.astype(F32), out)
    out = jnp.where(lane == 2, g1, out)
    out = jnp.where(lane == 3, g2, out)
    o_ref[...] = out


def router(u, w_pad, tm=1024):
    n, half = u.shape
    tm = min(tm, n)
    return pl.pallas_call(
        _router_kernel,
        grid=(n // tm,),
        in_specs=[pl.BlockSpec((tm, half), lambda i: (i, 0)), pl.BlockSpec((2 * half, 3 * 128), lambda i: (0, 0))],
        out_specs=pl.BlockSpec((tm, 128), lambda i: (i, 0)),
        out_shape=jax.ShapeDtypeStruct((n, 128), F32),
        compiler_params=_cparams(("parallel",)),
        name="router",
    )(u, _split3(w_pad))


def sc_gather_rows(table, idx, chunk=SC_GATHER_CHUNK):
    v, d = table.shape
    b = idx.shape[0]
    nw = SC_CORES * SC_SUBCORES
    per_w = b // nw
    assert per_w * nw == b and per_w % chunk == 0 and chunk % 8 == 0 and chunk <= 128
    mesh = plsc.VectorSubcoreMesh(core_axis_name="c", subcore_axis_name="s")

    @functools.partial(
        pl.kernel, mesh=mesh,
        out_type=jax.ShapeDtypeStruct((b, d), table.dtype),
        scratch_types=[pltpu.VMEM((chunk,), jnp.int32), pltpu.VMEM((chunk, d), table.dtype),
                       pltpu.SemaphoreType.DMA],
    )
    def gather_kernel(table_hbm, idx_hbm, out_hbm, idx_v, rows_v, sem):
        wid = lax.axis_index("s") * SC_CORES + lax.axis_index("c")
        base = wid * per_w

        @pl.loop(0, per_w // chunk)
        def _(j):
            off = pl.multiple_of(base + j * chunk, 8)
            pltpu.sync_copy(idx_hbm.at[pl.ds(off, chunk)], idx_v)
            pltpu.async_copy(table_hbm.at[idx_v], rows_v, sem).wait()
            pltpu.sync_copy(rows_v, out_hbm.at[pl.ds(off, chunk)])

    return gather_kernel(table, idx)


def sc_scatter_rows(src, dest, n_out, chunk=SC_GATHER_CHUNK):
    n, d = src.shape
    nk = dest.shape[0]
    nw = SC_CORES * SC_SUBCORES
    per_w = n // nw
    assert per_w * nw == n and per_w % chunk == 0 and chunk % 8 == 0 and chunk <= 128
    mesh = plsc.VectorSubcoreMesh(core_axis_name="c", subcore_axis_name="s")

    @functools.partial(
        pl.kernel, mesh=mesh,
        out_type=jax.ShapeDtypeStruct((n_out, d), src.dtype),
        scratch_types=[pltpu.VMEM((nk, chunk), jnp.int32), pltpu.VMEM((chunk, d), src.dtype)],
    )
    def scatter_kernel(src_hbm, dest_hbm, out_hbm, idx_v, rows_v):
        wid = lax.axis_index("s") * SC_CORES + lax.axis_index("c")
        base = wid * per_w

        @pl.loop(0, per_w // chunk)
        def _(j):
            off = pl.multiple_of(base + j * chunk, 8)
            pltpu.sync_copy(src_hbm.at[pl.ds(off, chunk)], rows_v)
            for k in range(nk):
                pltpu.sync_copy(dest_hbm.at[k, pl.ds(off, chunk)], idx_v.at[k])
            for k in range(nk):
                pltpu.sync_copy(rows_v, out_hbm.at[idx_v.at[k]])

    return scatter_kernel(src, dest)


def _expert_changed(be_ref, i):
    return jnp.logical_or(i == 0, be_ref[i] != be_ref[jnp.maximum(i - 1, 0)])


def _moe_up_kernel(be_ref, nv_ref, x_ref, wg_ref, wu_ref, o_ref, wg_bf, wu_bf):
    i = pl.program_id(1)

    @pl.when(_expert_changed(be_ref, i))
    def _():
        wg_bf[...] = wg_ref[...].astype(BF16)
        wu_bf[...] = wu_ref[...].astype(BF16)

    xp = x_ref[...]
    xp = jnp.where(lax.broadcasted_iota(jnp.int32, xp.shape, 0) < nv_ref[i], xp, jnp.uint32(0))
    x = _unpack_pairs(xp).astype(BF16)
    hid = _silu(jnp.dot(x, wg_bf[...], preferred_element_type=F32)) * jnp.dot(
        x, wu_bf[...], preferred_element_type=F32)
    o_ref[...] = hid.astype(o_ref.dtype)


def moe_up(block_e, block_valid, xb, wg, wu, tf=1792):
    ns = xb.shape[0]
    d, ff = wg.shape[1], wg.shape[2]
    nb = ns // MOE_BLOCK
    grid_spec = pltpu.PrefetchScalarGridSpec(
        num_scalar_prefetch=2,
        grid=(ff // tf, nb),
        in_specs=[pl.BlockSpec((MOE_BLOCK, d // 2), lambda f, i, be, nv: (i, 0)),
                  pl.BlockSpec((None, d, tf), lambda f, i, be, nv: (be[i], 0, f)),
                  pl.BlockSpec((None, d, tf), lambda f, i, be, nv: (be[i], 0, f))],
        out_specs=pl.BlockSpec((MOE_BLOCK, tf), lambda f, i, be, nv: (i, f)),
        scratch_shapes=[pltpu.VMEM((d, tf), BF16), pltpu.VMEM((d, tf), BF16)],
    )
    return pl.pallas_call(
        _moe_up_kernel,
        grid_spec=grid_spec,
        out_shape=jax.ShapeDtypeStruct((ns, ff), BF16),
        compiler_params=_cparams(("arbitrary", "arbitrary")),
        name="moe_up",
    )(block_e, block_valid, xb, wg, wu)


def _moe_down_kernel(be_ref, hid_ref, wd_ref, o_ref, wd_bf):
    @pl.when(_expert_changed(be_ref, pl.program_id(0)))
    def _():
        wd_bf[...] = wd_ref[...].astype(BF16)

    o_ref[...] = _pack_pairs(jnp.dot(hid_ref[...], wd_bf[...], preferred_element_type=F32))


def moe_down(block_e, hid, wd):
    ns, ff = hid.shape
    d = wd.shape[2]
    nb = ns // MOE_BLOCK
    grid_spec = pltpu.PrefetchScalarGridSpec(
        num_scalar_prefetch=1,
        grid=(nb,),
        in_specs=[pl.BlockSpec((MOE_BLOCK, ff), lambda i, be: (i, 0)),
                  pl.BlockSpec((None, ff, d), lambda i, be: (be[i], 0, 0))],
        out_specs=pl.BlockSpec((MOE_BLOCK, d // 2), lambda i, be: (i, 0)),
        scratch_shapes=[pltpu.VMEM((ff, d), BF16)],
    )
    return pl.pallas_call(
        _moe_down_kernel,
        grid_spec=grid_spec,
        out_shape=jax.ShapeDtypeStruct((ns, d // 2), jnp.uint32),
        compiler_params=_cparams(("arbitrary",)),
        name="moe_down",
    )(block_e, hid, wd)


def _final_kernel(h_ref, y0_ref, y1_ref, r_ref, nw_ref, o_ref):
    gates = r_ref[...]
    hn = (h_ref[...] + gates[:, TOP_K:TOP_K + 1] * _unpack_pairs(y0_ref[...])
          + gates[:, TOP_K + 1:TOP_K + 2] * _unpack_pairs(y1_ref[...]))
    y = hn * lax.rsqrt(jnp.mean(hn * hn, axis=-1, keepdims=True) + NORM_EPS)
    o_ref[...] = y * nw_ref[...]


def final_combine(h, yk, r, norm_w, tm=1024):
    n, d = h.shape
    tm = min(tm, n)
    nblk = n // tm
    row = pl.BlockSpec((tm, d), lambda i: (i, 0))
    return pl.pallas_call(
        _final_kernel,
        grid=(nblk,),
        in_specs=[row, pl.BlockSpec((tm, d // 2), lambda i: (i, 0)),
                  pl.BlockSpec((tm, d // 2), lambda i: (i + nblk, 0)),
                  pl.BlockSpec((tm, 128), lambda i: (i, 0)), pl.BlockSpec((1, d), lambda i: (0, 0))],
        out_specs=row,
        out_shape=jax.ShapeDtypeStruct((n, d), F32),
        compiler_params=_cparams(("parallel",)),
        name="final_combine",
    )(h, yk, yk, r, norm_w.reshape(1, d))


def _chunk_rows(x, bsz, seq):
    hh = x.shape[1]
    return jnp.transpose(x.reshape(bsz, seq, hh), (0, 2, 1)).reshape(bsz, hh, seq // CHUNK, CHUNK)


def hybrid_mixer_layer(h, u, bsz, seq, w_in, gdn_conv_w, gdn_A_log, gdn_dt_bias, gdn_norm_w, gdn_proj,
                       ssm_conv_w, ssm_conv_b, ssm_A_log, ssm_dt_bias, ssm_D, ssm_norm_w, ssm_proj, w_out,
                       next_norm_w, packed_u):
    c0 = 0
    c1 = c0 + GDN_CONV_DIM
    c2 = c1 + GDN_V_DIM
    c3 = c2 + GDN_V_HEADS
    c4 = c3 + GDN_V_HEADS
    c5 = c4 + SSM_D_INNER
    c6 = c5 + SSM_CONV_DIM
    c7 = c6 + SSM_HEADS
    c8 = c7 + D_MODEL
    wb = w_in.astype(BF16)
    qkv_raw = matmul(u, wb[:, c0:c1], BF16, name="proj_qkv")
    z_a = matmul(u, wb[:, c1:c2], BF16, name="proj_za")
    z_b = matmul(u, wb[:, c4:c5], BF16, name="proj_zb")
    xbc_raw = matmul(u, wb[:, c5:c6], BF16, name="proj_xbc")
    gate_a = matmul(u, wb[:, c7:c8], BF16, name="proj_ga")
    gate_b = matmul(u, wb[:, c8:], BF16, name="proj_gb")
    n_small = 2 * GDN_V_HEADS + SSM_HEADS
    w_small = jnp.concatenate([w_in[:, c2:c4], w_in[:, c6:c7]], axis=1)
    w_small = jnp.pad(w_small, ((0, 0), (0, 128 - n_small)))
    small = matmul(u, w_small, F32, tn=128, full_precision=True, name="proj_small")
    a4 = _chunk_rows(small[:, :GDN_V_HEADS], bsz, seq)
    b4 = _chunk_rows(small[:, GDN_V_HEADS:2 * GDN_V_HEADS], bsz, seq)
    dt_raw = small[:, 2 * GDN_V_HEADS:n_small]

    oa = gdn_mixer(qkv_raw, z_a, a4, b4, gdn_conv_w, gdn_A_log, gdn_dt_bias, gdn_norm_w, bsz=bsz, seq=seq)
    ob = ssd_mixer(xbc_raw, z_b, dt_raw, ssm_conv_w, ssm_conv_b, ssm_A_log, ssm_dt_bias, ssm_D, ssm_norm_w,
                   bsz=bsz, seq=seq)
    return merge_out(h, oa, ob, gate_a, gate_b, gdn_proj.astype(BF16), ssm_proj.astype(BF16),
                     w_out.astype(BF16), next_norm_w, packed_u)


def moe_layer(h, u, router_w, w_gate, w_up, w_down, final_norm_w):
    n, d = h.shape
    n_assign = n * TOP_K
    r = router(u, jnp.pad(router_w, ((0, 0), (0, 128 - N_EXPERTS))))
    top_idx = r[:, :TOP_K].astype(jnp.int32)
    flat_e = top_idx.reshape(-1)
    onehot = (flat_e[:, None] == jnp.arange(N_EXPERTS)[None, :]).astype(jnp.int32)
    csum = jnp.cumsum(onehot, axis=0)
    counts = csum[-1]
    rank = jnp.sum((csum - onehot) * onehot, axis=1)
    padded = (counts + MOE_BLOCK - 1) // MOE_BLOCK * MOE_BLOCK
    ends = jnp.cumsum(padded)
    pstart = ends - padded
    dest = (pstart[flat_e] + rank).astype(jnp.int32)
    n_blocks = -(-n_assign // MOE_BLOCK) + N_EXPERTS
    n_slots = n_blocks * MOE_BLOCK
    block_start = jnp.arange(n_blocks, dtype=jnp.int32) * MOE_BLOCK
    block_e = jnp.minimum(jnp.sum(block_start[:, None] >= ends[None, :], axis=1), N_EXPERTS - 1).astype(jnp.int32)
    block_valid = jnp.clip((pstart + counts)[block_e] - block_start, 0, MOE_BLOCK).astype(jnp.int32)
    dest_km = dest.reshape(n, TOP_K).T
    xb = sc_scatter_rows(u, dest_km, n_slots)
    hid = moe_up(block_e, block_valid, xb, w_gate, w_up)
    yb = moe_down(block_e, hid, w_down)
    yk = sc_gather_rows(yb, dest_km.reshape(-1))
    return final_combine(h, yk, r, final_norm_w)


def kernel(x, mix_norm_w, w_in, gdn_conv_w, gdn_A_log, gdn_dt_bias, gdn_norm_w, gdn_proj, ssm_conv_w, ssm_conv_b, ssm_A_log, ssm_dt_bias, ssm_D, ssm_norm_w, ssm_proj, w_out, ffn_norm_w, dense_w_gate, dense_w_up, dense_w_down, router_w, moe_w_gate, moe_w_up, moe_w_down, final_norm_w):
    bsz, seq, d = x.shape
    assert d == D_MODEL and w_in.shape[0] == 2, "dense-FFN layer followed by a final MoE layer"
    h = x.reshape(bsz * seq, d)
    u = rmsnorm(h, mix_norm_w[0], BF16)

    def mixer(layer, h, u, packed_u):
        return hybrid_mixer_layer(
            h, u, bsz, seq, w_in[layer], gdn_conv_w[layer], gdn_A_log[layer], gdn_dt_bias[layer],
            gdn_norm_w[layer], gdn_proj[layer], ssm_conv_w[layer], ssm_conv_b[layer], ssm_A_log[layer],
            ssm_dt_bias[layer], ssm_D[layer], ssm_norm_w[layer], ssm_proj[layer], w_out[layer],
            ffn_norm_w[layer], packed_u)

    h, u = mixer(0, h, u, False)
    h, u = dense_ffn(h, u, dense_w_gate[0].astype(BF16), dense_w_up[0].astype(BF16),
                     dense_w_down[0].astype(BF16), mix_norm_w[1])
    h, u = mixer(1, h, u, True)
    out = moe_layer(h, u, router_w[0], moe_w_gate[0], moe_w_up[0], moe_w_down[0], final_norm_w)
    return out.reshape(bsz, seq, d)
```

```python
import functools

import jax
import jax.numpy as jnp
from jax import lax
from jax.experimental import pallas as pl
from jax.experimental.pallas import tpu as pltpu
from jax.experimental.pallas import tpu_sc as plsc

F32 = jnp.float32
BF16 = jnp.bfloat16

D_MODEL = 1024
CONV_K = 4
CHUNK = 64
GDN_QK_HEADS = 4
GDN_V_HEADS = 8
GDN_HEAD = 128
GDN_QK_DIM = GDN_QK_HEADS * GDN_HEAD
GDN_V_DIM = GDN_V_HEADS * GDN_HEAD
GDN_CONV_DIM = 2 * GDN_QK_DIM + GDN_V_DIM
SSM_D_INNER = 2048
SSM_HEAD_DIM = 64
SSM_HEADS = SSM_D_INNER // SSM_HEAD_DIM
SSM_GROUPS = 4
SSM_HPG = SSM_HEADS // SSM_GROUPS
SSM_GROUP_DIM = SSM_D_INNER // SSM_GROUPS
SSM_STATE = 128
SSM_CONV_DIM = SSM_D_INNER + 2 * SSM_GROUPS * SSM_STATE
N_EXPERTS = 8
TOP_K = 2
MOE_BLOCK = 512
NORM_EPS = 1e-6
SSM_NORM_EPS = 1e-5
CONV_HALO = 8

VMEM_LIMIT = 56 * 1024 * 1024
PIPE_COLS = 512
SC_CORES = 2
SC_SUBCORES = 16
SC_GATHER_CHUNK = 128


def _cparams(sem):
    return pltpu.CompilerParams(dimension_semantics=sem, vmem_limit_bytes=VMEM_LIMIT)


def _silu(x):
    h = 0.5 * x
    return h + h * jnp.tanh(h)


def _softplus(x):
    return jnp.maximum(x, 0.0) + jnp.log1p(jnp.exp(-jnp.abs(x)))


def _dot(a, b):
    return jnp.dot(a.astype(BF16), b.astype(BF16), preferred_element_type=F32)


def _dot_nt(a, b):
    return lax.dot_general(a.astype(BF16), b.astype(BF16), (((1,), (1,)), ((), ())),
                           preferred_element_type=F32)


def _dot_tn(a, b):
    return lax.dot_general(a.astype(BF16), b.astype(BF16), (((0,), (0,)), ((), ())),
                           preferred_element_type=F32)


def _dot_hi(a, b):
    return jnp.dot(a, b, preferred_element_type=F32, precision=lax.Precision.HIGHEST)


def _pack_pairs(x):
    half = x.shape[1] // 2
    bits = lax.bitcast_convert_type(x.astype(BF16).astype(F32), jnp.uint32)
    return (bits[:, :half] >> 16) | (bits[:, half:] & jnp.uint32(0xFFFF0000))


def _unpack_pairs(p):
    lo = lax.bitcast_convert_type(p << 16, F32)
    hi = lax.bitcast_convert_type(p & jnp.uint32(0xFFFF0000), F32)
    return jnp.concatenate([lo, hi], axis=1)


def _rmsnorm_kernel(x_ref, w_ref, o_ref):
    x = x_ref[...]
    y = x * lax.rsqrt(jnp.mean(x * x, axis=-1, keepdims=True) + NORM_EPS)
    o_ref[...] = (y * w_ref[...]).astype(o_ref.dtype)


def rmsnorm(x, w, out_dtype, tm=1024):
    n, d = x.shape
    tm = min(tm, n)
    return pl.pallas_call(
        _rmsnorm_kernel,
        grid=(n // tm,),
        in_specs=[pl.BlockSpec((tm, d), lambda i: (i, 0)), pl.BlockSpec((1, d), lambda i: (0, 0))],
        out_specs=pl.BlockSpec((tm, d), lambda i: (i, 0)),
        out_shape=jax.ShapeDtypeStruct((n, d), out_dtype),
        compiler_params=_cparams(("parallel",)),
        name="rmsnorm",
    )(x, w.reshape(1, d))


def _split3(w):
    w_hi = w.astype(BF16)
    w_mid = (w - w_hi.astype(F32)).astype(BF16)
    w_lo = (w - w_hi.astype(F32) - w_mid.astype(F32)).astype(BF16)
    return jnp.concatenate([w_hi, w_mid, w_lo], axis=1)


def _dot_parts(x, w_parts, parts):
    r = jnp.dot(x, w_parts, preferred_element_type=F32)
    tn = w_parts.shape[1] // parts
    acc = r[:, :tn]
    for p in range(1, parts):
        acc = acc + r[:, p * tn:(p + 1) * tn]
    return acc


def _matmul_kernel(x_ref, w_ref, o_ref, *, parts):
    o_ref[...] = _dot_parts(x_ref[...], w_ref[...], parts).astype(o_ref.dtype)


def matmul(x, w, out_dtype, tm=1024, tn=1024, full_precision=False, name="matmul"):
    n, k = x.shape
    m = w.shape[1]
    tm = min(tm, n)
    tn = min(tn, m)
    parts = 1
    if full_precision:
        assert m == tn
        w = _split3(w)
        parts = 3
    return pl.pallas_call(
        functools.partial(_matmul_kernel, parts=parts),
        grid=(m // tn, n // tm),
        in_specs=[pl.BlockSpec((tm, k), lambda j, i: (i, 0)),
                  pl.BlockSpec((k, parts * tn), lambda j, i: (0, j))],
        out_specs=pl.BlockSpec((tm, tn), lambda j, i: (i, j)),
        out_shape=jax.ShapeDtypeStruct((n, m), out_dtype),
        compiler_params=_cparams(("parallel", "parallel")),
        name=name,
    )(x, w)


def _causal_conv(prev, cur, w_ref):
    x = jnp.concatenate([prev, cur], axis=0)
    assert CONV_K == 4
    x1 = pltpu.roll(x, 1, 0)
    near = x * w_ref[3:4, :] + x1 * w_ref[2:3, :]
    far = x * w_ref[1:2, :] + x1 * w_ref[0:1, :]
    return (near + pltpu.roll(far, 2, 0))[CONV_HALO:]


def _pipelined(n, produce, consume):
    cur = produce(0)
    for i in range(n):
        nxt = produce(i + 1) if i + 1 < n else None
        consume(i, cur)
        cur = nxt


def _proj_silu_kernel(x_ref, w_ref, o_ref):
    x = x_ref[...]
    cols = lambda c: slice(c * PIPE_COLS, (c + 1) * PIPE_COLS)

    def consume(c, raw):
        o_ref[:, cols(c)] = _silu(raw).astype(o_ref.dtype)

    _pipelined(o_ref.shape[1] // PIPE_COLS,
               lambda c: jnp.dot(x, w_ref[:, cols(c)], preferred_element_type=F32), consume)


def _proj_conv_silu_kernel(x_ref, w_ref, cw_ref, cb_ref, o_ref, carry_ref, *, tiles_per_seq):
    x = x_ref[...]
    first = pl.program_id(1) % tiles_per_seq == 0
    cols = lambda c: slice(c * PIPE_COLS, (c + 1) * PIPE_COLS)

    def consume(c, raw):
        prev = jnp.where(first, 0.0, carry_ref[:, cols(c)])
        carry_ref[:, cols(c)] = raw[-CONV_HALO:]
        y = _causal_conv(prev, raw, cw_ref.at[:, cols(c)]) + cb_ref[:, cols(c)]
        o_ref[:, cols(c)] = _silu(y).astype(o_ref.dtype)

    _pipelined(o_ref.shape[1] // PIPE_COLS,
               lambda c: jnp.dot(x, w_ref[:, cols(c)], preferred_element_type=F32), consume)


def proj_act(x, w, conv_w, conv_b, seq, tm=1024, tn=None, name="proj_act"):
    n, k = x.shape
    m = w.shape[1]
    tm = min(tm, seq)
    tn = m if tn is None else tn
    assert seq % tm == 0 and n % seq == 0 and m % tn == 0
    xw_specs = [pl.BlockSpec((tm, k), lambda j, i: (i, 0)), pl.BlockSpec((k, tn), lambda j, i: (0, j))]
    common = dict(
        grid=(m // tn, n // tm),
        out_specs=pl.BlockSpec((tm, tn), lambda j, i: (i, j)),
        out_shape=jax.ShapeDtypeStruct((n, m), BF16),
        name=name,
    )
    if conv_w is None:
        return pl.pallas_call(_proj_silu_kernel, in_specs=xw_specs,
                              compiler_params=_cparams(("parallel", "parallel")), **common)(x, w)
    return pl.pallas_call(
        functools.partial(_proj_conv_silu_kernel, tiles_per_seq=seq // tm),
        in_specs=xw_specs + [pl.BlockSpec((CONV_K, tn), lambda j, i: (0, j)),
                             pl.BlockSpec((1, tn), lambda j, i: (0, j))],
        scratch_shapes=[pltpu.VMEM((CONV_HALO, tn), F32)],
        compiler_params=_cparams(("parallel", "arbitrary")),
        **common,
    )(x, w, conv_w, conv_b.reshape(1, m))


def _col_from_row(row, eye):
    return jnp.sum(jnp.where(eye, row, 0.0), axis=-1, keepdims=True)


def _gdn_kernel(alog_ref, dtb_ref,
                q_ref, k_ref, v_ref, z_ref, a_ref, b_ref, nw_ref,
                o_ref,
                s_ref, *, tt, hpb):
    hblk = pl.program_id(1)
    nck = tt // CHUNK
    rep = GDN_V_HEADS // GDN_QK_HEADS
    dh = GDN_HEAD

    @pl.when(pl.program_id(2) == 0)
    def _():
        s_ref[...] = jnp.zeros_like(s_ref)

    q_all = q_ref[...].astype(F32)
    k_all = k_ref[...].astype(F32)
    v_all = v_ref[...].astype(F32)

    ri = lax.broadcasted_iota(jnp.int32, (CHUNK, CHUNK), 0)
    ci = lax.broadcasted_iota(jnp.int32, (CHUNK, CHUNK), 1)
    upper = (ri <= ci).astype(F32)
    eye = ri == ci
    causal = ri >= ci
    strict = ri > ci
    chunks = [slice(c * CHUNK, (c + 1) * CHUNK) for c in range(nck)]
    nsteps = CHUNK.bit_length() - 1

    qs, ks = [], []
    for j in range(hpb // rep):
        q = q_all[:, j * dh:(j + 1) * dh]
        k = k_all[:, j * dh:(j + 1) * dh]
        qs.append(q * (lax.rsqrt(jnp.sum(q * q, axis=-1, keepdims=True) + 1e-6) * (dh ** -0.5)))
        ks.append(k * lax.rsqrt(jnp.sum(k * k, axis=-1, keepdims=True) + 1e-6))

    heads = range(hpb)
    g_rows, beta_rows, gc_rows = [], [], []
    for hh in heads:
        head = hblk * hpb + hh
        neg_a = -jnp.exp(jnp.full((1, CHUNK), alog_ref[head], F32))
        g_rows.append(neg_a * _softplus(a_ref[hh] + dtb_ref[head]))
        beta_rows.append(jax.nn.sigmoid(b_ref[hh]))
        gc_rows.append(_dot_hi(g_rows[hh], upper))

    ps, xs, qkk, qes, egl = {}, {}, {}, {}, {}

    def local_prep(c):
        sl = chunks[c]
        kq = [_dot_nt(jnp.concatenate([ks[j][sl], qs[j][sl]], axis=0), ks[j][sl]) for j in range(hpb // rep)]
        for hh in heads:
            qc, kc = qs[hh // rep][sl], ks[hh // rep][sl]
            vc = v_all[sl, hh * dh:(hh + 1) * dh]
            gc_row = gc_rows[hh][c:c + 1, :]
            gc_col = _col_from_row(gc_row, eye)
            beta_col = _col_from_row(beta_rows[hh][c:c + 1, :], eye)
            g_last = jnp.sum(g_rows[hh][c:c + 1, :], axis=-1, keepdims=True)
            decay = jnp.where(causal, jnp.exp(gc_col - gc_row), 0.0)
            eg_col = jnp.exp(gc_col)
            kk, qk = kq[hh // rep][:CHUNK], kq[hh // rep][CHUNK:]
            ps[hh, c] = jnp.where(strict, kk * decay, 0.0) * beta_col
            xs[hh, c] = jnp.concatenate([vc * beta_col, kc * (beta_col * eg_col)], axis=1)
            k_dec = kc * jnp.exp(g_last - gc_col)
            qkk[hh, c] = jnp.concatenate([qk * decay, k_dec.T], axis=0)
            qes[hh, c] = qc * eg_col
            egl[hh, c] = jnp.exp(g_last)

    def solve_step(c, step):
        for hh in heads:
            u = (hh, c)
            if step + 1 < nsteps:
                r = _dot(ps[u], jnp.concatenate([xs[u], ps[u]], axis=1))
                ps[u] = r[:, 2 * dh:]
                r = r[:, :2 * dh]
            else:
                r = _dot(ps[u], xs[u])
            xs[u] = xs[u] - r if step == 0 else xs[u] + r

    s = [s_ref[hh] for hh in heads]
    v_new = {}
    o_chunks = [[] for _ in heads]

    def rec_a(c):
        for hh in heads:
            x = xs[hh, c]
            wq_s = _dot(jnp.concatenate([x[:, dh:], qes[hh, c]], axis=0), s[hh])
            v_new[hh] = x[:, :dh] - wq_s[:CHUNK]
            o_chunks[hh].append(wq_s[CHUNK:])

    def rec_b(c):
        for hh in heads:
            r = _dot(qkk[hh, c], v_new[hh])
            o_chunks[hh][c] = o_chunks[hh][c] + r[:CHUNK]
            s[hh] = s[hh] * egl[hh, c] + r[CHUNK:]

    half = nsteps // 2
    local_prep(0)
    for step in range(nsteps):
        solve_step(0, step)
    for c in range(nck):
        nxt = c + 1 < nck
        if nxt:
            local_prep(c + 1)
        rec_a(c)
        if nxt:
            for step in range(half):
                solve_step(c + 1, step)
        rec_b(c)
        if nxt:
            for step in range(half, nsteps):
                solve_step(c + 1, step)
    outs = []
    for hh in range(hpb):
        s_ref[hh] = s[hh]
        o = jnp.concatenate(o_chunks[hh], axis=0)
        outs.append(o * lax.rsqrt(jnp.mean(o * o, axis=-1, keepdims=True) + NORM_EPS) * nw_ref[...])
    y = jnp.concatenate(outs, axis=1)
    o_ref[...] = (y * z_ref[...].astype(F32)).astype(o_ref.dtype)


def gdn_mixer(qkv, z_a, a4, b4, a_log, dt_bias, norm_w, *, bsz, seq, tt=256, hpb=GDN_V_HEADS):
    n = bsz * seq
    nt = seq // tt
    rep = GDN_V_HEADS // GDN_QK_HEADS
    nhb = GDN_V_HEADS // hpb
    qw = hpb // rep * GDN_HEAD
    vw = hpb * GDN_HEAD
    q0, k0, v0 = 0, GDN_QK_DIM // qw, 2 * GDN_QK_DIM // vw

    def cur(width, c0):
        return pl.BlockSpec((tt, width), lambda b, h, t, *_: (b * nt + t, c0 + h))

    small = pl.BlockSpec((None, hpb, None, tt // CHUNK, CHUNK), lambda b, h, t, *_: (b, h, t, 0, 0))
    a4 = a4.reshape(bsz, GDN_V_HEADS, nt, tt // CHUNK, CHUNK)
    b4 = b4.reshape(bsz, GDN_V_HEADS, nt, tt // CHUNK, CHUNK)
    grid_spec = pltpu.PrefetchScalarGridSpec(
        num_scalar_prefetch=2,
        grid=(bsz, nhb, nt),
        in_specs=[cur(qw, q0), cur(qw, k0), cur(vw, v0), cur(vw, 0), small, small,
                  pl.BlockSpec((1, GDN_HEAD), lambda b, h, t, *_: (0, 0))],
        out_specs=cur(vw, 0),
        scratch_shapes=[pltpu.VMEM((hpb, GDN_HEAD, GDN_HEAD), F32)],
    )
    return pl.pallas_call(
        functools.partial(_gdn_kernel, tt=tt, hpb=hpb),
        grid_spec=grid_spec,
        out_shape=jax.ShapeDtypeStruct((n, GDN_V_DIM), BF16),
        compiler_params=_cparams(("parallel", "parallel", "arbitrary")),
        name="gdn_mixer",
    )(a_log, dt_bias, qkv, qkv, qkv, z_a, a4, b4, norm_w.reshape(1, GDN_HEAD))


def _ssd_kernel(x_ref, b_ref, c_ref, z_ref, dt_ref, alog_ref, dtb_ref, dskip_ref, nw_ref,
                o_ref,
                h_ref, *, tt):
    nck = tt // CHUNK
    pw = 2 * SSM_HEAD_DIM
    npair = SSM_GROUP_DIM // pw

    @pl.when(pl.program_id(2) == 0)
    def _():
        h_ref[...] = jnp.zeros_like(h_ref)

    xs = x_ref[...].astype(F32)
    bm = b_ref[...]
    cm = c_ref[...]

    dt_rows = _softplus(dt_ref[...] + dtb_ref[...])
    adt_rows = -jnp.exp(alog_ref[...]) * dt_rows
    r2 = lax.broadcasted_iota(jnp.int32, (pw, pw), 0)
    c2 = lax.broadcasted_iota(jnp.int32, (pw, pw), 1)
    same_head = (r2 // CHUNK) == (c2 // CHUNK)
    cum_tot = jnp.concatenate([(same_head & (r2 <= c2)).astype(F32), same_head.astype(F32)], axis=1)
    adt_pc = jnp.concatenate([adt_rows[:, p * pw:(p + 1) * pw] for p in range(npair)], axis=0)
    ct = _dot_hi(adt_pc, cum_tot)
    acs_pc, tot_pc = ct[:, :pw], ct[:, pw:]

    li = lax.broadcasted_iota(jnp.int32, (CHUNK, pw), 0)
    ji = lax.broadcasted_iota(jnp.int32, (CHUNK, pw), 1)
    lo_half = ji < CHUNK
    pick_a = ji == li
    pick_b = ji == li + CHUNK
    causal2 = li >= jnp.where(lo_half, ji, ji - CHUNK)
    rb = lax.broadcasted_iota(jnp.int32, (pw, pw), 0)
    cb_ = lax.broadcasted_iota(jnp.int32, (pw, pw), 1)
    blockdiag = (rb < CHUNK) == (cb_ < CHUNK)

    def pair_col(row):
        a = jnp.sum(jnp.where(pick_a, row, 0.0), axis=-1, keepdims=True)
        b = jnp.sum(jnp.where(pick_b, row, 0.0), axis=-1, keepdims=True)
        return jnp.where(lo_half, a, b)

    hstate = h_ref[...]
    ys = []
    for c in range(nck):
        sl = slice(c * CHUNK, (c + 1) * CHUNK)
        cmc, bmc = cm[sl], bm[sl]
        cb2 = _dot_nt(cmc, jnp.concatenate([bmc, bmc], axis=0))
        yd_parts, eacs_parts, xdec_parts, eal_parts = [], [], [], []
        for p in range(npair):
            row = p * nck + c
            acs_row = acs_pc[row:row + 1, :]
            tot_row = tot_pc[row:row + 1, :]
            acs_col = pair_col(acs_row)
            dt_col = pair_col(dt_rows[c:c + 1, p * pw:(p + 1) * pw])
            lmat = jnp.where(causal2, jnp.exp(acs_col - acs_row), 0.0)
            xdt = xs[sl, p * pw:(p + 1) * pw] * dt_col
            xdt2 = jnp.where(blockdiag, jnp.concatenate([xdt, xdt], axis=0), 0.0)
            yd_parts.append(_dot(cb2 * lmat, xdt2))
            eacs_parts.append(jnp.exp(acs_col))
            xdec_parts.append(xdt * jnp.exp(tot_row - acs_col))
            eal_parts.append(jnp.exp(tot_row))
        upd = _dot_tn(bmc, jnp.concatenate(xdec_parts, axis=1))
        y_off = _dot(cmc, hstate)
        ys.append(jnp.concatenate(yd_parts, axis=1) + y_off * jnp.concatenate(eacs_parts, axis=1))
        hstate = hstate * jnp.concatenate(eal_parts, axis=1) + upd
    h_ref[...] = hstate
    y = jnp.concatenate(ys, axis=0) + xs * dskip_ref[...]
    y = y * z_ref[...].astype(F32)
    y = y * lax.rsqrt(jnp.mean(y * y, axis=-1, keepdims=True) + SSM_NORM_EPS) * nw_ref[...]
    o_ref[...] = y.astype(o_ref.dtype)


def ssd_mixer(xbc, z_b, dt_raw, a_log, dt_bias, d_skip, norm_w, *, bsz, seq, tt=256):
    n = bsz * seq
    nt = seq // tt
    nck = tt // CHUNK
    gd = SSM_GROUP_DIM
    st = SSM_STATE
    x_blocks = SSM_D_INNER // st
    dtl = jnp.transpose(dt_raw.reshape(bsz, seq // CHUNK, CHUNK, SSM_HEADS), (0, 1, 3, 2))
    dtl = dtl.reshape(bsz, nt, nck, SSM_D_INNER)
    per_head = lambda v: jnp.repeat(v, SSM_HEAD_DIM).reshape(1, SSM_D_INNER)

    def cur(width, colf):
        return pl.BlockSpec((tt, width), lambda b, g, t, *_: (b * nt + t, colf(g)))

    def rowspec(rows, width, colf):
        return pl.BlockSpec((rows, width), lambda b, g, t, *_: (0, colf(g)))

    xcol = lambda g: g
    bcol = lambda g: x_blocks + g
    ccol = lambda g: x_blocks + SSM_GROUPS + g
    return pl.pallas_call(
        functools.partial(_ssd_kernel, tt=tt),
        grid=(bsz, SSM_GROUPS, nt),
        in_specs=[cur(gd, xcol), cur(st, bcol), cur(st, ccol), cur(gd, xcol),
                  pl.BlockSpec((None, None, nck, gd), lambda b, g, t: (b, t, 0, g)),
                  rowspec(1, gd, xcol), rowspec(1, gd, xcol), rowspec(1, gd, xcol), rowspec(1, gd, xcol)],
        out_specs=cur(gd, xcol),
        out_shape=jax.ShapeDtypeStruct((n, SSM_D_INNER), BF16),
        scratch_shapes=[pltpu.VMEM((SSM_STATE, gd), F32)],
        compiler_params=_cparams(("parallel", "parallel", "arbitrary")),
        name="ssd_mixer",
    )(xbc, xbc, xbc, z_b, dtl, per_head(a_log), per_head(dt_bias), per_head(d_skip),
      norm_w.reshape(1, SSM_D_INNER))


def _merge_kernel(h_ref, oa_ref, ob_ref, ga_ref, gb_ref, wa_ref, wb_ref, wo_ref, nw_ref, hn_ref, u_ref):
    half = h_ref.shape[0] // 2
    rows = lambda r: slice(r * half, (r + 1) * half)

    def produce(r):
        return (jnp.dot(oa_ref[rows(r), :], wa_ref[...], preferred_element_type=F32),
                jnp.dot(ob_ref[rows(r), :], wb_ref[...], preferred_element_type=F32))

    def consume(r, ab):
        mixed = (jax.nn.sigmoid(ga_ref[rows(r), :].astype(F32)) * ab[0]
                 + jax.nn.sigmoid(gb_ref[rows(r), :].astype(F32)) * ab[1])
        hn = h_ref[rows(r), :] + jnp.dot(mixed.astype(BF16), wo_ref[...], preferred_element_type=F32)
        hn_ref[rows(r), :] = hn
        y = hn * lax.rsqrt(jnp.mean(hn * hn, axis=-1, keepdims=True) + NORM_EPS) * nw_ref[...]
        u_ref[rows(r), :] = _pack_pairs(y) if u_ref.dtype == jnp.uint32 else y.astype(u_ref.dtype)

    _pipelined(2, produce, consume)


def merge_out(h, oa, ob, ga, gb, wa, wb, wo, next_norm_w, packed_u, tm=512):
    n, d = h.shape
    u_sds = jax.ShapeDtypeStruct((n, d // 2), jnp.uint32) if packed_u else jax.ShapeDtypeStruct((n, d), BF16)
    tm = min(tm, n)
    row = lambda width: pl.BlockSpec((tm, width), lambda i: (i, 0))
    full = lambda r, c: pl.BlockSpec((r, c), lambda i: (0, 0))
    return pl.pallas_call(
        _merge_kernel,
        grid=(n // tm,),
        in_specs=[row(d), row(GDN_V_DIM), row(SSM_D_INNER), row(d), row(d),
                  full(GDN_V_DIM, d), full(SSM_D_INNER, d), full(d, d), full(1, d)],
        out_specs=[row(d), row(u_sds.shape[1])],
        out_shape=[jax.ShapeDtypeStruct((n, d), F32), u_sds],
        compiler_params=_cparams(("parallel",)),
        name="merge_out",
    )(h, oa, ob, ga, gb, wa, wb, wo, next_norm_w.reshape(1, d))


def _ffn_kernel(h_ref, u_ref, wg_ref, wu_ref, wd_ref, nw_ref, hn_ref, un_ref, acc_ref):
    f = pl.program_id(1)

    @pl.when(f == 0)
    def _():
        acc_ref[...] = h_ref[...]

    half = u_ref.shape[0] // 2
    rows = lambda r: slice(r * half, (r + 1) * half)

    def produce(r):
        u = u_ref[rows(r), :]
        return (jnp.dot(u, wg_ref[...], preferred_element_type=F32),
                jnp.dot(u, wu_ref[...], preferred_element_type=F32))

    def consume(r, gu):
        hid = (_silu(gu[0]) * gu[1]).astype(BF16)
        acc_ref[rows(r), :] += jnp.dot(hid, wd_ref[...], preferred_element_type=F32)

    _pipelined(2, produce, consume)

    @pl.when(f == pl.num_programs(1) - 1)
    def _():
        hn = acc_ref[...]
        hn_ref[...] = hn
        y = hn * lax.rsqrt(jnp.mean(hn * hn, axis=-1, keepdims=True) + NORM_EPS)
        un_ref[...] = (y * nw_ref[...]).astype(un_ref.dtype)


def dense_ffn(h, u, wg, wu, wd, next_norm_w, tm=1024, tf=256):
    n, d = h.shape
    ff = wg.shape[1]
    tm = min(tm, n)
    return pl.pallas_call(
        _ffn_kernel,
        grid=(n // tm, ff // tf),
        in_specs=[pl.BlockSpec((tm, d), lambda i, f: (i, 0)), pl.BlockSpec((tm, d), lambda i, f: (i, 0)),
                  pl.BlockSpec((d, tf), lambda i, f: (0, f)), pl.BlockSpec((d, tf), lambda i, f: (0, f)),
                  pl.BlockSpec((tf, d), lambda i, f: (f, 0)), pl.BlockSpec((1, d), lambda i, f: (0, 0))],
        out_specs=[pl.BlockSpec((tm, d), lambda i, f: (i, 0)), pl.BlockSpec((tm, d), lambda i, f: (i, 0))],
        out_shape=[jax.ShapeDtypeStruct((n, d), F32), jax.ShapeDtypeStruct((n, d), BF16)],
        scratch_shapes=[pltpu.VMEM((tm, d), F32)],
        compiler_params=_cparams(("parallel", "arbitrary")),
        name="dense_ffn",
    )(h, u, wg, wu, wd, next_norm_w.reshape(1, d))


def _router_kernel(u_ref, w_ref, o_ref):
    logits = _dot_parts(_unpack_pairs(u_ref[...]).astype(BF16), w_ref[...], 3)
    lane = lax.broadcasted_iota(jnp.int32, logits.shape, 1)
    neg = jnp.float32(-3.0e38)
    logits = jnp.where(lane < N_EXPERTS, logits, neg)
    m1 = jnp.max(logits, axis=-1, keepdims=True)
    i1 = jnp.min(jnp.where(logits == m1, lane, 2 * N_EXPERTS), axis=-1, keepdims=True)
    rest = jnp.where(lane == i1, neg, logits)
    m2 = jnp.max(rest, axis=-1, keepdims=True)
    i2 = jnp.min(jnp.where(rest == m2, lane, 2 * N_EXPERTS), axis=-1, keepdims=True)
    e2 = jnp.exp(m2 - m1)
    g1 = 1.0 / (1.0 + e2)
    g2 = e2 / (1.0 + e2)
    out = jnp.where(lane == 0, i1.astype(F32), 0.0)
    out = jnp.where(lane == 1, i2.astype(F32), out)
    out = jnp.where(lane == 2, g1, out)
    out = jnp.where(lane == 3, g2, out)
    o_ref[...] = out


def router(u, w_pad, tm=1024):
    n, half = u.shape
    tm = min(tm, n)
    return pl.pallas_call(
        _router_kernel,
        grid=(n // tm,),
        in_specs=[pl.BlockSpec((tm, half), lambda i: (i, 0)), pl.BlockSpec((2 * half, 3 * 128), lambda i: (0, 0))],
        out_specs=pl.BlockSpec((tm, 128), lambda i: (i, 0)),
        out_shape=jax.ShapeDtypeStruct((n, 128), F32),
        compiler_params=_cparams(("parallel",)),
        name="router",
    )(u, _split3(w_pad))


def sc_gather_rows(table, idx, chunk=SC_GATHER_CHUNK):
    v, d = table.shape
    b = idx.shape[0]
    nw = SC_CORES * SC_SUBCORES
    per_w = b // nw
    assert per_w * nw == b and per_w % chunk == 0 and chunk % 8 == 0 and chunk <= 128
    mesh = plsc.VectorSubcoreMesh(core_axis_name="c", subcore_axis_name="s")

    @functools.partial(
        pl.kernel, mesh=mesh,
        out_type=jax.ShapeDtypeStruct((b, d), table.dtype),
        scratch_types=[pltpu.VMEM((chunk,), jnp.int32), pltpu.VMEM((chunk, d), table.dtype),
                       pltpu.SemaphoreType.DMA],
    )
    def gather_kernel(table_hbm, idx_hbm, out_hbm, idx_v, rows_v, sem):
        wid = lax.axis_index("s") * SC_CORES + lax.axis_index("c")
        base = wid * per_w

        @pl.loop(0, per_w // chunk)
        def _(j):
            off = pl.multiple_of(base + j * chunk, 8)
            pltpu.sync_copy(idx_hbm.at[pl.ds(off, chunk)], idx_v)
            pltpu.async_copy(table_hbm.at[idx_v], rows_v, sem).wait()
            pltpu.sync_copy(rows_v, out_hbm.at[pl.ds(off, chunk)])

    return gather_kernel(table, idx)


def sc_scatter_rows(src, dest, n_out, chunk=SC_GATHER_CHUNK):
    n, d = src.shape
    nk = dest.shape[0]
    nw = SC_CORES * SC_SUBCORES
    per_w = n // nw
    assert per_w * nw == n and per_w % chunk == 0 and chunk % 8 == 0 and chunk <= 128
    mesh = plsc.VectorSubcoreMesh(core_axis_name="c", subcore_axis_name="s")

    @functools.partial(
        pl.kernel, mesh=mesh,
        out_type=jax.ShapeDtypeStruct((n_out, d), src.dtype),
        scratch_types=[pltpu.VMEM((nk, chunk), jnp.int32), pltpu.VMEM((chunk, d), src.dtype)],
    )
    def scatter_kernel(src_hbm, dest_hbm, out_hbm, idx_v, rows_v):
        wid = lax.axis_index("s") * SC_CORES + lax.axis_index("c")
        base = wid * per_w

        @pl.loop(0, per_w // chunk)
        def _(j):
            off = pl.multiple_of(base + j * chunk, 8)
            pltpu.sync_copy(src_hbm.at[pl.ds(off, chunk)], rows_v)
            for k in range(nk):
                pltpu.sync_copy(dest_hbm.at[k, pl.ds(off, chunk)], idx_v.at[k])
            for k in range(nk):
                pltpu.sync_copy(rows_v, out_hbm.at[idx_v.at[k]])

    return scatter_kernel(src, dest)


def _expert_changed(be_ref, i):
    return jnp.logical_or(i == 0, be_ref[i] != be_ref[jnp.maximum(i - 1, 0)])


def _moe_up_kernel(be_ref, nv_ref, x_ref, wg_ref, wu_ref, o_ref, wg_bf, wu_bf):
    i = pl.program_id(1)

    @pl.when(_expert_changed(be_ref, i))
    def _():
        wg_bf[...] = wg_ref[...].astype(BF16)
        wu_bf[...] = wu_ref[...].astype(BF16)

    xp = x_ref[...]
    xp = jnp.where(lax.broadcasted_iota(jnp.int32, xp.shape, 0) < nv_ref[i], xp, jnp.uint32(0))
    x = _unpack_pairs(xp).astype(BF16)
    half = PIPE_COLS // 2
    cols = lambda c: slice(c * half, (c + 1) * half)

    def produce(c):
        return (jnp.dot(x, wg_bf[:, cols(c)], preferred_element_type=F32),
                jnp.dot(x, wu_bf[:, cols(c)], preferred_element_type=F32))

    def consume(c, gu):
        o_ref[:, cols(c)] = (_silu(gu[0]) * gu[1]).astype(o_ref.dtype)

    _pipelined(o_ref.shape[1] // half, produce, consume)


def moe_up(block_e, block_valid, xb, wg, wu, tf=1792):
    ns = xb.shape[0]
    d, ff = wg.shape[1], wg.shape[2]
    nb = ns // MOE_BLOCK
    grid_spec = pltpu.PrefetchScalarGridSpec(
        num_scalar_prefetch=2,
        grid=(ff // tf, nb),
        in_specs=[pl.BlockSpec((MOE_BLOCK, d // 2), lambda f, i, be, nv: (i, 0)),
                  pl.BlockSpec((None, d, tf), lambda f, i, be, nv: (be[i], 0, f)),
                  pl.BlockSpec((None, d, tf), lambda f, i, be, nv: (be[i], 0, f))],
        out_specs=pl.BlockSpec((MOE_BLOCK, tf), lambda f, i, be, nv: (i, f)),
        scratch_shapes=[pltpu.VMEM((d, tf), BF16), pltpu.VMEM((d, tf), BF16)],
    )
    return pl.pallas_call(
        _moe_up_kernel,
        grid_spec=grid_spec,
        out_shape=jax.ShapeDtypeStruct((ns, ff), BF16),
        compiler_params=_cparams(("arbitrary", "arbitrary")),
        name="moe_up",
    )(block_e, block_valid, xb, wg, wu)


def _moe_down_kernel(be_ref, hid_ref, wd_ref, o_ref, wd_bf):
    @pl.when(_expert_changed(be_ref, pl.program_id(0)))
    def _():
        wd_bf[...] = wd_ref[...].astype(BF16)

    o_ref[...] = _pack_pairs(jnp.dot(hid_ref[...], wd_bf[...], preferred_element_type=F32))


def moe_down(block_e, hid, wd):
    ns, ff = hid.shape
    d = wd.shape[2]
    nb = ns // MOE_BLOCK
    grid_spec = pltpu.PrefetchScalarGridSpec(
        num_scalar_prefetch=1,
        grid=(nb,),
        in_specs=[pl.BlockSpec((MOE_BLOCK, ff), lambda i, be: (i, 0)),
                  pl.BlockSpec((None, ff, d), lambda i, be: (be[i], 0, 0))],
        out_specs=pl.BlockSpec((MOE_BLOCK, d // 2), lambda i, be: (i, 0)),
        scratch_shapes=[pltpu.VMEM((ff, d), BF16)],
    )
    return pl.pallas_call(
        _moe_down_kernel,
        grid_spec=grid_spec,
        out_shape=jax.ShapeDtypeStruct((ns, d // 2), jnp.uint32),
        compiler_params=_cparams(("arbitrary",)),
        name="moe_down",
    )(block_e, hid, wd)


def _final_kernel(h_ref, y0_ref, y1_ref, r_ref, nw_ref, o_ref):
    gates = r_ref[...]
    hn = (h_ref[...] + gates[:, TOP_K:TOP_K + 1] * _unpack_pairs(y0_ref[...])
          + gates[:, TOP_K + 1:TOP_K + 2] * _unpack_pairs(y1_ref[...]))
    y = hn * lax.rsqrt(jnp.mean(hn * hn, axis=-1, keepdims=True) + NORM_EPS)
    o_ref[...] = y * nw_ref[...]


def final_combine(h, yk, r, norm_w, tm=1024):
    n, d = h.shape
    tm = min(tm, n)
    nblk = n // tm
    row = pl.BlockSpec((tm, d), lambda i: (i, 0))
    return pl.pallas_call(
        _final_kernel,
        grid=(nblk,),
        in_specs=[row, pl.BlockSpec((tm, d // 2), lambda i: (i, 0)),
                  pl.BlockSpec((tm, d // 2), lambda i: (i + nblk, 0)),
                  pl.BlockSpec((tm, 128), lambda i: (i, 0)), pl.BlockSpec((1, d), lambda i: (0, 0))],
        out_specs=row,
        out_shape=jax.ShapeDtypeStruct((n, d), F32),
        compiler_params=_cparams(("parallel",)),
        name="final_combine",
    )(h, yk, yk, r, norm_w.reshape(1, d))


def _chunk_rows(x, bsz, seq):
    hh = x.shape[1]
    return jnp.transpose(x.reshape(bsz, seq, hh), (0, 2, 1)).reshape(bsz, hh, seq // CHUNK, CHUNK)


def hybrid_mixer_layer(h, u, bsz, seq, w_in, gdn_conv_w, gdn_A_log, gdn_dt_bias, gdn_norm_w, gdn_proj,
                       ssm_conv_w, ssm_conv_b, ssm_A_log, ssm_dt_bias, ssm_D, ssm_norm_w, ssm_proj, w_out,
                       next_norm_w, packed_u):
    c0 = 0
    c1 = c0 + GDN_CONV_DIM
    c2 = c1 + GDN_V_DIM
    c3 = c2 + GDN_V_HEADS
    c4 = c3 + GDN_V_HEADS
    c5 = c4 + SSM_D_INNER
    c6 = c5 + SSM_CONV_DIM
    c7 = c6 + SSM_HEADS
    c8 = c7 + D_MODEL
    wb = w_in.astype(BF16)
    qkv = proj_act(u, wb[:, c0:c1], gdn_conv_w, jnp.zeros((GDN_CONV_DIM,), F32), seq, name="proj_qkv")
    z_a = proj_act(u, wb[:, c1:c2], None, None, seq, name="proj_za")
    z_b = proj_act(u, wb[:, c4:c5], None, None, seq, name="proj_zb")
    xbc = proj_act(u, wb[:, c5:c6], ssm_conv_w, ssm_conv_b, seq, name="proj_xbc")
    gate_a = matmul(u, wb[:, c7:c8], BF16, name="proj_ga")
    gate_b = matmul(u, wb[:, c8:], BF16, name="proj_gb")
    n_small = 2 * GDN_V_HEADS + SSM_HEADS
    w_small = jnp.concatenate([w_in[:, c2:c4], w_in[:, c6:c7]], axis=1)
    w_small = jnp.pad(w_small, ((0, 0), (0, 128 - n_small)))
    small = matmul(u, w_small, F32, tn=128, full_precision=True, name="proj_small")
    a4 = _chunk_rows(small[:, :GDN_V_HEADS], bsz, seq)
    b4 = _chunk_rows(small[:, GDN_V_HEADS:2 * GDN_V_HEADS], bsz, seq)
    dt_raw = small[:, 2 * GDN_V_HEADS:n_small]

    oa = gdn_mixer(qkv, z_a, a4, b4, gdn_A_log, gdn_dt_bias, gdn_norm_w, bsz=bsz, seq=seq)
    ob = ssd_mixer(xbc, z_b, dt_raw, ssm_A_log, ssm_dt_bias, ssm_D, ssm_norm_w, bsz=bsz, seq=seq)
    return merge_out(h, oa, ob, gate_a, gate_b, gdn_proj.astype(BF16), ssm_proj.astype(BF16),
                     w_out.astype(BF16), next_norm_w, packed_u)


def moe_layer(h, u, router_w, w_gate, w_up, w_down, final_norm_w):
    n, d = h.shape
    n_assign = n * TOP_K
    r = router(u, jnp.pad(router_w, ((0, 0), (0, 128 - N_EXPERTS))))
    top_idx = r[:, :TOP_K].astype(jnp.int32)
    flat_e = top_idx.reshape(-1)
    onehot = (flat_e[:, None] == jnp.arange(N_EXPERTS)[None, :]).astype(jnp.int32)
    csum = jnp.cumsum(onehot, axis=0)
    counts = csum[-1]
    rank = jnp.sum((csum - onehot) * onehot, axis=1)
    padded = (counts + MOE_BLOCK - 1) // MOE_BLOCK * MOE_BLOCK
    ends = jnp.cumsum(padded)
    pstart = ends - padded
    dest = (pstart[flat_e] + rank).astype(jnp.int32)
    n_blocks = -(-n_assign // MOE_BLOCK) + N_EXPERTS
    n_slots = n_blocks * MOE_BLOCK
    block_start = jnp.arange(n_blocks, dtype=jnp.int32) * MOE_BLOCK
    block_e = jnp.minimum(jnp.sum(block_start[:, None] >= ends[None, :], axis=1), N_EXPERTS - 1).astype(jnp.int32)
    block_valid = jnp.clip((pstart + counts)[block_e] - block_start, 0, MOE_BLOCK).astype(jnp.int32)
    dest_km = dest.reshape(n, TOP_K).T
    xb = sc_scatter_rows(u, dest_km, n_slots)
    hid = moe_up(block_e, block_valid, xb, w_gate, w_up)
    yb = moe_down(block_e, hid, w_down)
    yk = sc_gather_rows(yb, dest_km.reshape(-1))
    return final_combine(h, yk, r, final_norm_w)


def kernel(x, mix_norm_w, w_in, gdn_conv_w, gdn_A_log, gdn_dt_bias, gdn_norm_w, gdn_proj, ssm_conv_w, ssm_conv_b, ssm_A_log, ssm_dt_bias, ssm_D, ssm_norm_w, ssm_proj, w_out, ffn_norm_w, dense_w_gate, dense_w_up, dense_w_down, router_w, moe_w_gate, moe_w_up, moe_w_down, final_norm_w):
    bsz, seq, d = x.shape
    assert d == D_MODEL and w_in.shape[0] == 2, "dense-FFN layer followed by a final MoE layer"
    h = x.reshape(bsz * seq, d)
    u = rmsnorm(h, mix_norm_w[0], BF16)

    def mixer(layer, h, u, packed_u):
        return hybrid_mixer_layer(
            h, u, bsz, seq, w_in[layer], gdn_conv_w[layer], gdn_A_log[layer], gdn_dt_bias[layer],
            gdn_norm_w[layer], gdn_proj[layer], ssm_conv_w[layer], ssm_conv_b[layer], ssm_A_log[layer],
            ssm_dt_bias[layer], ssm_D[layer], ssm_norm_w[layer], ssm_proj[layer], w_out[layer],
            ffn_norm_w[layer], packed_u)

    h, u = mixer(0, h, u, False)
    h, u = dense_ffn(h, u, dense_w_gate[0].astype(BF16), dense_w_up[0].astype(BF16),
                     dense_w_down[0].astype(BF16), mix_norm_w[1])
    h, u = mixer(1, h, u, True)
    out = moe_layer(h, u, router_w[0], moe_w_gate[0], moe_w_up[0], moe_w_down[0], final_norm_w)
    return out.reshape(bsz, seq, d)
```

```python
import functools

import jax
import jax.numpy as jnp
from jax import lax
from jax.experimental import pallas as pl
from jax.experimental.pallas import tpu as pltpu
from jax.experimental.pallas import tpu_sc as plsc

F32 = jnp.float32
BF16 = jnp.bfloat16

D_MODEL = 1024
CONV_K = 4
CHUNK = 64
GDN_QK_HEADS = 4
GDN_V_HEADS = 8
GDN_HEAD = 128
GDN_QK_DIM = GDN_QK_HEADS * GDN_HEAD
GDN_V_DIM = GDN_V_HEADS * GDN_HEAD
GDN_CONV_DIM = 2 * GDN_QK_DIM + GDN_V_DIM
SSM_D_INNER = 2048
SSM_HEAD_DIM = 64
SSM_HEADS = SSM_D_INNER // SSM_HEAD_DIM
SSM_GROUPS = 4
SSM_HPG = SSM_HEADS // SSM_GROUPS
SSM_GROUP_DIM = SSM_D_INNER // SSM_GROUPS
SSM_STATE = 128
SSM_CONV_DIM = SSM_D_INNER + 2 * SSM_GROUPS * SSM_STATE
N_EXPERTS = 8
TOP_K = 2
MOE_BLOCK = 512
NORM_EPS = 1e-6
SSM_NORM_EPS = 1e-5
CONV_HALO = 8

VMEM_LIMIT = 56 * 1024 * 1024
PIPE_COLS = 512
SC_CORES = 2
SC_SUBCORES = 16
SC_GATHER_CHUNK = 128


def _cparams(sem):
    return pltpu.CompilerParams(dimension_semantics=sem, vmem_limit_bytes=VMEM_LIMIT)


def _silu(x):
    h = 0.5 * x
    return h + h * jnp.tanh(h)


def _softplus(x):
    return jnp.maximum(x, 0.0) + jnp.log1p(jnp.exp(-jnp.abs(x)))


def _dot(a, b):
    return jnp.dot(a.astype(BF16), b.astype(BF16), preferred_element_type=F32)


def _dot_nt(a, b):
    return lax.dot_general(a.astype(BF16), b.astype(BF16), (((1,), (1,)), ((), ())),
                           preferred_element_type=F32)


def _dot_tn(a, b):
    return lax.dot_general(a.astype(BF16), b.astype(BF16), (((0,), (0,)), ((), ())),
                           preferred_element_type=F32)


def _dot_hi(a, b):
    return jnp.dot(a, b, preferred_element_type=F32, precision=lax.Precision.HIGHEST)


def _pack_pairs(x):
    half = x.shape[1] // 2
    bits = lax.bitcast_convert_type(x.astype(BF16).astype(F32), jnp.uint32)
    return (bits[:, :half] >> 16) | (bits[:, half:] & jnp.uint32(0xFFFF0000))


def _unpack_pairs(p):
    lo = lax.bitcast_convert_type(p << 16, F32)
    hi = lax.bitcast_convert_type(p & jnp.uint32(0xFFFF0000), F32)
    return jnp.concatenate([lo, hi], axis=1)


def _rmsnorm_kernel(x_ref, w_ref, o_ref):
    x = x_ref[...]
    y = x * lax.rsqrt(jnp.mean(x * x, axis=-1, keepdims=True) + NORM_EPS)
    o_ref[...] = (y * w_ref[...]).astype(o_ref.dtype)


def rmsnorm(x, w, out_dtype, tm=1024):
    n, d = x.shape
    tm = min(tm, n)
    return pl.pallas_call(
        _rmsnorm_kernel,
        grid=(n // tm,),
        in_specs=[pl.BlockSpec((tm, d), lambda i: (i, 0)), pl.BlockSpec((1, d), lambda i: (0, 0))],
        out_specs=pl.BlockSpec((tm, d), lambda i: (i, 0)),
        out_shape=jax.ShapeDtypeStruct((n, d), out_dtype),
        compiler_params=_cparams(("parallel",)),
        name="rmsnorm",
    )(x, w.reshape(1, d))


def _split3(w):
    w_hi = w.astype(BF16)
    w_mid = (w - w_hi.astype(F32)).astype(BF16)
    w_lo = (w - w_hi.astype(F32) - w_mid.astype(F32)).astype(BF16)
    return jnp.concatenate([w_hi, w_mid, w_lo], axis=1)


def _dot_parts(x, w_parts, parts):
    r = jnp.dot(x, w_parts, preferred_element_type=F32)
    tn = w_parts.shape[1] // parts
    acc = r[:, :tn]
    for p in range(1, parts):
        acc = acc + r[:, p * tn:(p + 1) * tn]
    return acc


def _matmul_kernel(x_ref, w_ref, o_ref, *, parts):
    o_ref[...] = _dot_parts(x_ref[...], w_ref[...], parts).astype(o_ref.dtype)


def matmul(x, w, out_dtype, tm=1024, tn=1024, full_precision=False, name="matmul"):
    n, k = x.shape
    m = w.shape[1]
    tm = min(tm, n)
    tn = min(tn, m)
    parts = 1
    if full_precision:
        assert m == tn
        w = _split3(w)
        parts = 3
    return pl.pallas_call(
        functools.partial(_matmul_kernel, parts=parts),
        grid=(m // tn, n // tm),
        in_specs=[pl.BlockSpec((tm, k), lambda j, i: (i, 0)),
                  pl.BlockSpec((k, parts * tn), lambda j, i: (0, j))],
        out_specs=pl.BlockSpec((tm, tn), lambda j, i: (i, j)),
        out_shape=jax.ShapeDtypeStruct((n, m), out_dtype),
        compiler_params=_cparams(("parallel", "parallel")),
        name=name,
    )(x, w)


def _causal_conv(prev, cur, w_ref):
    x = jnp.concatenate([prev, cur], axis=0)
    assert CONV_K == 4
    x1 = pltpu.roll(x, 1, 0)
    near = x * w_ref[3:4, :] + x1 * w_ref[2:3, :]
    far = x * w_ref[1:2, :] + x1 * w_ref[0:1, :]
    return (near + pltpu.roll(far, 2, 0))[CONV_HALO:]


def _pipelined(n, produce, consume):
    cur = produce(0)
    for i in range(n):
        nxt = produce(i + 1) if i + 1 < n else None
        consume(i, cur)
        cur = nxt


def _proj_silu_kernel(x_ref, w_ref, o_ref):
    x = x_ref[...]
    cols = lambda c: slice(c * PIPE_COLS, (c + 1) * PIPE_COLS)

    def consume(c, raw):
        o_ref[:, cols(c)] = _silu(raw).astype(o_ref.dtype)

    _pipelined(o_ref.shape[1] // PIPE_COLS,
               lambda c: jnp.dot(x, w_ref[:, cols(c)], preferred_element_type=F32), consume)


def _proj_conv_silu_kernel(x_ref, w_ref, cw_ref, cb_ref, o_ref, carry_ref, *, tiles_per_seq):
    x = x_ref[...]
    first = pl.program_id(1) % tiles_per_seq == 0
    cols = lambda c: slice(c * PIPE_COLS, (c + 1) * PIPE_COLS)

    def consume(c, raw):
        prev = jnp.where(first, 0.0, carry_ref[:, cols(c)])
        carry_ref[:, cols(c)] = raw[-CONV_HALO:]
        y = _causal_conv(prev, raw, cw_ref.at[:, cols(c)]) + cb_ref[:, cols(c)]
        o_ref[:, cols(c)] = _silu(y).astype(o_ref.dtype)

    _pipelined(o_ref.shape[1] // PIPE_COLS,
               lambda c: jnp.dot(x, w_ref[:, cols(c)], preferred_element_type=F32), consume)


def proj_act(x, w, conv_w, conv_b, seq, tm=1024, tn=None, name="proj_act"):
    n, k = x.shape
    m = w.shape[1]
    tm = min(tm, seq)
    tn = m if tn is None else tn
    assert seq % tm == 0 and n % seq == 0 and m % tn == 0
    xw_specs = [pl.BlockSpec((tm, k), lambda j, i: (i, 0)), pl.BlockSpec((k, tn), lambda j, i: (0, j))]
    common = dict(
        grid=(m // tn, n // tm),
        out_specs=pl.BlockSpec((tm, tn), lambda j, i: (i, j)),
        out_shape=jax.ShapeDtypeStruct((n, m), BF16),
        name=name,
    )
    if conv_w is None:
        return pl.pallas_call(_proj_silu_kernel, in_specs=xw_specs,
                              compiler_params=_cparams(("parallel", "parallel")), **common)(x, w)
    return pl.pallas_call(
        functools.partial(_proj_conv_silu_kernel, tiles_per_seq=seq // tm),
        in_specs=xw_specs + [pl.BlockSpec((CONV_K, tn), lambda j, i: (0, j)),
                             pl.BlockSpec((1, tn), lambda j, i: (0, j))],
        scratch_shapes=[pltpu.VMEM((CONV_HALO, tn), F32)],
        compiler_params=_cparams(("parallel", "arbitrary")),
        **common,
    )(x, w, conv_w, conv_b.reshape(1, m))


def _col_from_row(row, eye):
    return jnp.sum(jnp.where(eye, row, 0.0), axis=-1, keepdims=True)


def _gdn_kernel(alog_ref, dtb_ref,
                q_ref, k_ref, v_ref, z_ref, a_ref, b_ref, nw_ref,
                o_ref,
                s_ref, *, tt, hpb):
    hblk = pl.program_id(1)
    nck = tt // CHUNK
    rep = GDN_V_HEADS // GDN_QK_HEADS
    dh = GDN_HEAD

    @pl.when(pl.program_id(2) == 0)
    def _():
        s_ref[...] = jnp.zeros_like(s_ref)

    q_all = q_ref[...].astype(F32)
    k_all = k_ref[...].astype(F32)
    v_all = v_ref[...].astype(F32)

    ri = lax.broadcasted_iota(jnp.int32, (CHUNK, CHUNK), 0)
    ci = lax.broadcasted_iota(jnp.int32, (CHUNK, CHUNK), 1)
    upper = (ri <= ci).astype(F32)
    eye = ri == ci
    causal = ri >= ci
    strict = ri > ci
    chunks = [slice(c * CHUNK, (c + 1) * CHUNK) for c in range(nck)]
    nsteps = CHUNK.bit_length() - 1

    qs, ks = [], []
    for j in range(hpb // rep):
        q = q_all[:, j * dh:(j + 1) * dh]
        k = k_all[:, j * dh:(j + 1) * dh]
        qs.append(q * (lax.rsqrt(jnp.sum(q * q, axis=-1, keepdims=True) + 1e-6) * (dh ** -0.5)))
        ks.append(k * lax.rsqrt(jnp.sum(k * k, axis=-1, keepdims=True) + 1e-6))

    heads = range(hpb)
    g_rows, beta_rows, gc_rows = [], [], []
    for hh in heads:
        head = hblk * hpb + hh
        neg_a = -jnp.exp(jnp.full((1, CHUNK), alog_ref[head], F32))
        g_rows.append(neg_a * _softplus(a_ref[hh] + dtb_ref[head]))
        beta_rows.append(jax.nn.sigmoid(b_ref[hh]))
        gc_rows.append(_dot_hi(g_rows[hh], upper))

    ps, xs, qkk, qes, egl = {}, {}, {}, {}, {}

    def local_prep(c):
        sl = chunks[c]
        kq = [_dot_nt(jnp.concatenate([ks[j][sl], qs[j][sl]], axis=0), ks[j][sl]) for j in range(hpb // rep)]
        for hh in heads:
            qc, kc = qs[hh // rep][sl], ks[hh // rep][sl]
            vc = v_all[sl, hh * dh:(hh + 1) * dh]
            gc_row = gc_rows[hh][c:c + 1, :]
            gc_col = _col_from_row(gc_row, eye)
            beta_col = _col_from_row(beta_rows[hh][c:c + 1, :], eye)
            g_last = jnp.sum(g_rows[hh][c:c + 1, :], axis=-1, keepdims=True)
            decay = jnp.where(causal, jnp.exp(gc_col - gc_row), 0.0)
            eg_col = jnp.exp(gc_col)
            kk, qk = kq[hh // rep][:CHUNK], kq[hh // rep][CHUNK:]
            ps[hh, c] = jnp.where(strict, kk * decay, 0.0) * beta_col
            xs[hh, c] = jnp.concatenate([vc * beta_col, kc * (beta_col * eg_col)], axis=1)
            k_dec = kc * jnp.exp(g_last - gc_col)
            qkk[hh, c] = jnp.concatenate([qk * decay, k_dec.T], axis=0)
            qes[hh, c] = qc * eg_col
            egl[hh, c] = jnp.exp(g_last)

    def solve_step(c, step):
        for hh in heads:
            u = (hh, c)
            if step + 1 < nsteps:
                r = _dot(ps[u], jnp.concatenate([xs[u], ps[u]], axis=1))
                ps[u] = r[:, 2 * dh:]
                r = r[:, :2 * dh]
            else:
                r = _dot(ps[u], xs[u])
            xs[u] = xs[u] - r if step == 0 else xs[u] + r

    s = [s_ref[hh] for hh in heads]
    v_new = {}
    o_chunks = [[] for _ in heads]

    def rec_a(c):
        for hh in heads:
            x = xs[hh, c]
            wq_s = _dot(jnp.concatenate([x[:, dh:], qes[hh, c]], axis=0), s[hh])
            v_new[hh] = x[:, :dh] - wq_s[:CHUNK]
            o_chunks[hh].append(wq_s[CHUNK:])

    def rec_b(c):
        for hh in heads:
            r = _dot(qkk[hh, c], v_new[hh])
            o_chunks[hh][c] = o_chunks[hh][c] + r[:CHUNK]
            s[hh] = s[hh] * egl[hh, c] + r[CHUNK:]

    half = nsteps // 2
    local_prep(0)
    for step in range(nsteps):
        solve_step(0, step)
    for c in range(nck):
        nxt = c + 1 < nck
        if nxt:
            local_prep(c + 1)
        rec_a(c)
        if nxt:
            for step in range(half):
                solve_step(c + 1, step)
        rec_b(c)
        if nxt:
            for step in range(half, nsteps):
                solve_step(c + 1, step)
    outs = []
    for hh in range(hpb):
        s_ref[hh] = s[hh]
        o = jnp.concatenate(o_chunks[hh], axis=0)
        outs.append(o * lax.rsqrt(jnp.mean(o * o, axis=-1, keepdims=True) + NORM_EPS) * nw_ref[...])
    y = jnp.concatenate(outs, axis=1)
    o_ref[...] = (y * z_ref[...].astype(F32)).astype(o_ref.dtype)


def gdn_mixer(qkv, z_a, a4, b4, a_log, dt_bias, norm_w, *, bsz, seq, tt=512, hpb=GDN_V_HEADS):
    tt = min(tt, seq)
    n = bsz * seq
    nt = seq // tt
    rep = GDN_V_HEADS // GDN_QK_HEADS
    nhb = GDN_V_HEADS // hpb
    qw = hpb // rep * GDN_HEAD
    vw = hpb * GDN_HEAD
    q0, k0, v0 = 0, GDN_QK_DIM // qw, 2 * GDN_QK_DIM // vw

    def cur(width, c0):
        return pl.BlockSpec((tt, width), lambda b, h, t, *_: (b * nt + t, c0 + h))

    small = pl.BlockSpec((None, hpb, None, tt // CHUNK, CHUNK), lambda b, h, t, *_: (b, h, t, 0, 0))
    a4 = a4.reshape(bsz, GDN_V_HEADS, nt, tt // CHUNK, CHUNK)
    b4 = b4.reshape(bsz, GDN_V_HEADS, nt, tt // CHUNK, CHUNK)
    grid_spec = pltpu.PrefetchScalarGridSpec(
        num_scalar_prefetch=2,
        grid=(bsz, nhb, nt),
        in_specs=[cur(qw, q0), cur(qw, k0), cur(vw, v0), cur(vw, 0), small, small,
                  pl.BlockSpec((1, GDN_HEAD), lambda b, h, t, *_: (0, 0))],
        out_specs=cur(vw, 0),
        scratch_shapes=[pltpu.VMEM((hpb, GDN_HEAD, GDN_HEAD), F32)],
    )
    return pl.pallas_call(
        functools.partial(_gdn_kernel, tt=tt, hpb=hpb),
        grid_spec=grid_spec,
        out_shape=jax.ShapeDtypeStruct((n, GDN_V_DIM), BF16),
        compiler_params=_cparams(("parallel", "parallel", "arbitrary")),
        name="gdn_mixer",
    )(a_log, dt_bias, qkv, qkv, qkv, z_a, a4, b4, norm_w.reshape(1, GDN_HEAD))


def _ssd_kernel(x_ref, b_ref, c_ref, z_ref, dt_ref, alog_ref, dtb_ref, dskip_ref, nw_ref,
                o_ref,
                h_ref, *, tt):
    nck = tt // CHUNK
    pw = 2 * SSM_HEAD_DIM
    npair = SSM_GROUP_DIM // pw

    @pl.when(pl.program_id(2) == 0)
    def _():
        h_ref[...] = jnp.zeros_like(h_ref)

    xs = x_ref[...].astype(F32)
    bm = b_ref[...]
    cm = c_ref[...]

    dt_rows = _softplus(dt_ref[...] + dtb_ref[...])
    adt_rows = -jnp.exp(alog_ref[...]) * dt_rows
    r2 = lax.broadcasted_iota(jnp.int32, (pw, pw), 0)
    c2 = lax.broadcasted_iota(jnp.int32, (pw, pw), 1)
    same_head = (r2 // CHUNK) == (c2 // CHUNK)
    cum_tot = jnp.concatenate([(same_head & (r2 <= c2)).astype(F32), same_head.astype(F32)], axis=1)
    adt_pc = jnp.concatenate([adt_rows[:, p * pw:(p + 1) * pw] for p in range(npair)], axis=0)
    ct = _dot_hi(adt_pc, cum_tot)
    acs_pc, tot_pc = ct[:, :pw], ct[:, pw:]

    li = lax.broadcasted_iota(jnp.int32, (CHUNK, pw), 0)
    ji = lax.broadcasted_iota(jnp.int32, (CHUNK, pw), 1)
    lo_half = ji < CHUNK
    pick_a = ji == li
    pick_b = ji == li + CHUNK
    causal2 = li >= jnp.where(lo_half, ji, ji - CHUNK)
    rb = lax.broadcasted_iota(jnp.int32, (pw, pw), 0)
    cb_ = lax.broadcasted_iota(jnp.int32, (pw, pw), 1)
    blockdiag = (rb < CHUNK) == (cb_ < CHUNK)

    def pair_col(row):
        a = jnp.sum(jnp.where(pick_a, row, 0.0), axis=-1, keepdims=True)
        b = jnp.sum(jnp.where(pick_b, row, 0.0), axis=-1, keepdims=True)
        return jnp.where(lo_half, a, b)

    hstate = h_ref[...]
    ys = []
    for c in range(nck):
        sl = slice(c * CHUNK, (c + 1) * CHUNK)
        cmc, bmc = cm[sl], bm[sl]
        cb2 = _dot_nt(cmc, jnp.concatenate([bmc, bmc], axis=0))
        yd_parts, eacs_parts, xdec_parts, eal_parts = [], [], [], []
        for p in range(npair):
            row = p * nck + c
            acs_row = acs_pc[row:row + 1, :]
            tot_row = tot_pc[row:row + 1, :]
            acs_col = pair_col(acs_row)
            dt_col = pair_col(dt_rows[c:c + 1, p * pw:(p + 1) * pw])
            lmat = jnp.where(causal2, jnp.exp(acs_col - acs_row), 0.0)
            xdt = xs[sl, p * pw:(p + 1) * pw] * dt_col
            xdt2 = jnp.where(blockdiag, jnp.concatenate([xdt, xdt], axis=0), 0.0)
            yd_parts.append(_dot(cb2 * lmat, xdt2))
            eacs_parts.append(jnp.exp(acs_col))
            xdec_parts.append(xdt * jnp.exp(tot_row - acs_col))
            eal_parts.append(jnp.exp(tot_row))
        upd = _dot_tn(bmc, jnp.concatenate(xdec_parts, axis=1))
        y_off = _dot(cmc, hstate)
        ys.append(jnp.concatenate(yd_parts, axis=1) + y_off * jnp.concatenate(eacs_parts, axis=1))
        hstate = hstate * jnp.concatenate(eal_parts, axis=1) + upd
    h_ref[...] = hstate
    y = jnp.concatenate(ys, axis=0) + xs * dskip_ref[...]
    y = y * z_ref[...].astype(F32)
    y = y * lax.rsqrt(jnp.mean(y * y, axis=-1, keepdims=True) + SSM_NORM_EPS) * nw_ref[...]
    o_ref[...] = y.astype(o_ref.dtype)


def ssd_mixer(xbc, z_b, dt_raw, a_log, dt_bias, d_skip, norm_w, *, bsz, seq, tt=2048):
    tt = min(tt, seq)
    n = bsz * seq
    nt = seq // tt
    nck = tt // CHUNK
    gd = SSM_GROUP_DIM
    st = SSM_STATE
    x_blocks = SSM_D_INNER // st
    dtl = jnp.transpose(dt_raw.reshape(bsz, seq // CHUNK, CHUNK, SSM_HEADS), (0, 1, 3, 2))
    dtl = dtl.reshape(bsz, nt, nck, SSM_D_INNER)
    per_head = lambda v: jnp.repeat(v, SSM_HEAD_DIM).reshape(1, SSM_D_INNER)

    def cur(width, colf):
        return pl.BlockSpec((tt, width), lambda b, g, t, *_: (b * nt + t, colf(g)))

    def rowspec(rows, width, colf):
        return pl.BlockSpec((rows, width), lambda b, g, t, *_: (0, colf(g)))

    xcol = lambda g: g
    bcol = lambda g: x_blocks + g
    ccol = lambda g: x_blocks + SSM_GROUPS + g
    return pl.pallas_call(
        functools.partial(_ssd_kernel, tt=tt),
        grid=(bsz, SSM_GROUPS, nt),
        in_specs=[cur(gd, xcol), cur(st, bcol), cur(st, ccol), cur(gd, xcol),
                  pl.BlockSpec((None, None, nck, gd), lambda b, g, t: (b, t, 0, g)),
                  rowspec(1, gd, xcol), rowspec(1, gd, xcol), rowspec(1, gd, xcol), rowspec(1, gd, xcol)],
        out_specs=cur(gd, xcol),
        out_shape=jax.ShapeDtypeStruct((n, SSM_D_INNER), BF16),
        scratch_shapes=[pltpu.VMEM((SSM_STATE, gd), F32)],
        compiler_params=_cparams(("parallel", "parallel", "arbitrary")),
        name="ssd_mixer",
    )(xbc, xbc, xbc, z_b, dtl, per_head(a_log), per_head(dt_bias), per_head(d_skip),
      norm_w.reshape(1, SSM_D_INNER))


def _merge_kernel(h_ref, oa_ref, ob_ref, ga_ref, gb_ref, wa_ref, wb_ref, wo_ref, nw_ref, hn_ref, u_ref):
    half = h_ref.shape[0] // 2
    rows = lambda r: slice(r * half, (r + 1) * half)

    def produce(r):
        return (jnp.dot(oa_ref[rows(r), :], wa_ref[...], preferred_element_type=F32),
                jnp.dot(ob_ref[rows(r), :], wb_ref[...], preferred_element_type=F32))

    def consume(r, ab):
        mixed = (jax.nn.sigmoid(ga_ref[rows(r), :].astype(F32)) * ab[0]
                 + jax.nn.sigmoid(gb_ref[rows(r), :].astype(F32)) * ab[1])
        hn = h_ref[rows(r), :] + jnp.dot(mixed.astype(BF16), wo_ref[...], preferred_element_type=F32)
        hn_ref[rows(r), :] = hn
        y = hn * lax.rsqrt(jnp.mean(hn * hn, axis=-1, keepdims=True) + NORM_EPS) * nw_ref[...]
        u_ref[rows(r), :] = _pack_pairs(y) if u_ref.dtype == jnp.uint32 else y.astype(u_ref.dtype)

    _pipelined(2, produce, consume)


def merge_out(h, oa, ob, ga, gb, wa, wb, wo, next_norm_w, packed_u, tm=512):
    n, d = h.shape
    u_sds = jax.ShapeDtypeStruct((n, d // 2), jnp.uint32) if packed_u else jax.ShapeDtypeStruct((n, d), BF16)
    tm = min(tm, n)
    row = lambda width: pl.BlockSpec((tm, width), lambda i: (i, 0))
    full = lambda r, c: pl.BlockSpec((r, c), lambda i: (0, 0))
    return pl.pallas_call(
        _merge_kernel,
        grid=(n // tm,),
        in_specs=[row(d), row(GDN_V_DIM), row(SSM_D_INNER), row(d), row(d),
                  full(GDN_V_DIM, d), full(SSM_D_INNER, d), full(d, d), full(1, d)],
        out_specs=[row(d), row(u_sds.shape[1])],
        out_shape=[jax.ShapeDtypeStruct((n, d), F32), u_sds],
        compiler_params=_cparams(("parallel",)),
        name="merge_out",
    )(h, oa, ob, ga, gb, wa, wb, wo, next_norm_w.reshape(1, d))


def _ffn_kernel(h_ref, u_ref, wg_ref, wu_ref, wd_ref, nw_ref, hn_ref, un_ref, acc_ref, *, tf):
    u = u_ref[...]
    cols = lambda f: slice(f * tf, (f + 1) * tf)
    acc_ref[...] = h_ref[...]

    def produce(f):
        return (jnp.dot(u, wg_ref[:, cols(f)], preferred_element_type=F32),
                jnp.dot(u, wu_ref[:, cols(f)], preferred_element_type=F32))

    def consume(f, gu):
        hid = (_silu(gu[0]) * gu[1]).astype(BF16)
        acc_ref[...] += jnp.dot(hid, wd_ref[cols(f), :], preferred_element_type=F32)

    _pipelined(wg_ref.shape[1] // tf, produce, consume)
    hn = acc_ref[...]
    hn_ref[...] = hn
    y = hn * lax.rsqrt(jnp.mean(hn * hn, axis=-1, keepdims=True) + NORM_EPS)
    un_ref[...] = (y * nw_ref[...]).astype(un_ref.dtype)


def dense_ffn(h, u, wg, wu, wd, next_norm_w, tm=1024, tf=256):
    n, d = h.shape
    ff = wg.shape[1]
    tm = min(tm, n)
    assert ff % tf == 0
    row = pl.BlockSpec((tm, d), lambda i: (i, 0))
    full = lambda r, c: pl.BlockSpec((r, c), lambda i: (0, 0))
    return pl.pallas_call(
        functools.partial(_ffn_kernel, tf=tf),
        grid=(n // tm,),
        in_specs=[row, row, full(d, ff), full(d, ff), full(ff, d), full(1, d)],
        out_specs=[row, row],
        out_shape=[jax.ShapeDtypeStruct((n, d), F32), jax.ShapeDtypeStruct((n, d), BF16)],
        scratch_shapes=[pltpu.VMEM((tm, d), F32)],
        compiler_params=_cparams(("parallel",)),
        name="dense_ffn",
    )(h, u, wg, wu, wd, next_norm_w.reshape(1, d))


def _router_kernel(u_ref, w_ref, o_ref):
    logits = _dot_parts(_unpack_pairs(u_ref[...]).astype(BF16), w_ref[...], 3)
    lane = lax.broadcasted_iota(jnp.int32, logits.shape, 1)
    neg = jnp.float32(-3.0e38)
    logits = jnp.where(lane < N_EXPERTS, logits, neg)
    m1 = jnp.max(logits, axis=-1, keepdims=True)
    i1 = jnp.min(jnp.where(logits == m1, lane, 2 * N_EXPERTS), axis=-1, keepdims=True)
    rest = jnp.where(lane == i1, neg, logits)
    m2 = jnp.max(rest, axis=-1, keepdims=True)
    i2 = jnp.min(jnp.where(rest == m2, lane, 2 * N_EXPERTS), axis=-1, keepdims=True)
    e2 = jnp.exp(m2 - m1)
    g1 = 1.0 / (1.0 + e2)
    g2 = e2 / (1.0 + e2)
    out = jnp.where(lane == 0, i1.astype(F32), 0.0)
    out = jnp.where(lane == 1, i2.astype(F32), out)
    out = jnp.where(lane == 2, g1, out)
    out = jnp.where(lane == 3, g2, out)
    o_ref[...] = out


def router(u, w_pad, tm=1024):
    n, half = u.shape
    tm = min(tm, n)
    return pl.pallas_call(
        _router_kernel,
        grid=(n // tm,),
        in_specs=[pl.BlockSpec((tm, half), lambda i: (i, 0)), pl.BlockSpec((2 * half, 3 * 128), lambda i: (0, 0))],
        out_specs=pl.BlockSpec((tm, 128), lambda i: (i, 0)),
        out_shape=jax.ShapeDtypeStruct((n, 128), F32),
        compiler_params=_cparams(("parallel",)),
        name="router",
    )(u, _split3(w_pad))


def sc_gather_rows(table, idx, chunk=SC_GATHER_CHUNK):
    v, d = table.shape
    b = idx.shape[0]
    nw = SC_CORES * SC_SUBCORES
    per_w = b // nw
    assert per_w * nw == b and per_w % chunk == 0 and chunk % 8 == 0 and chunk <= 128
    mesh = plsc.VectorSubcoreMesh(core_axis_name="c", subcore_axis_name="s")

    @functools.partial(
        pl.kernel, mesh=mesh,
        out_type=jax.ShapeDtypeStruct((b, d), table.dtype),
        scratch_types=[pltpu.VMEM((chunk,), jnp.int32), pltpu.VMEM((chunk, d), table.dtype),
                       pltpu.SemaphoreType.DMA],
    )
    def gather_kernel(table_hbm, idx_hbm, out_hbm, idx_v, rows_v, sem):
        wid = lax.axis_index("s") * SC_CORES + lax.axis_index("c")
        base = wid * per_w

        @pl.loop(0, per_w // chunk)
        def _(j):
            off = pl.multiple_of(base + j * chunk, 8)
            pltpu.sync_copy(idx_hbm.at[pl.ds(off, chunk)], idx_v)
            pltpu.async_copy(table_hbm.at[idx_v], rows_v, sem).wait()
            pltpu.sync_copy(rows_v, out_hbm.at[pl.ds(off, chunk)])

    return gather_kernel(table, idx)


def sc_scatter_rows(src, dest, n_out, chunk=SC_GATHER_CHUNK):
    n, d = src.shape
    nk = dest.shape[0]
    nw = SC_CORES * SC_SUBCORES
    per_w = n // nw
    assert per_w * nw == n and per_w % chunk == 0 and chunk % 8 == 0 and chunk <= 128
    mesh = plsc.VectorSubcoreMesh(core_axis_name="c", subcore_axis_name="s")

    @functools.partial(
        pl.kernel, mesh=mesh,
        out_type=jax.ShapeDtypeStruct((n_out, d), src.dtype),
        scratch_types=[pltpu.VMEM((nk, chunk), jnp.int32), pltpu.VMEM((chunk, d), src.dtype)],
    )
    def scatter_kernel(src_hbm, dest_hbm, out_hbm, idx_v, rows_v):
        wid = lax.axis_index("s") * SC_CORES + lax.axis_index("c")
        base = wid * per_w

        @pl.loop(0, per_w // chunk)
        def _(j):
            off = pl.multiple_of(base + j * chunk, 8)
            pltpu.sync_copy(src_hbm.at[pl.ds(off, chunk)], rows_v)
            for k in range(nk):
                pltpu.sync_copy(dest_hbm.at[k, pl.ds(off, chunk)], idx_v.at[k])
            for k in range(nk):
                pltpu.sync_copy(rows_v, out_hbm.at[idx_v.at[k]])

    return scatter_kernel(src, dest)


def _expert_changed(be_ref, i):
    return jnp.logical_or(i == 0, be_ref[i] != be_ref[jnp.maximum(i - 1, 0)])


def _moe_up_kernel(be_ref, nv_ref, x_ref, wg_ref, wu_ref, o_ref, wg_bf, wu_bf):
    i = pl.program_id(1)

    @pl.when(_expert_changed(be_ref, i))
    def _():
        wg_bf[...] = wg_ref[...].astype(BF16)
        wu_bf[...] = wu_ref[...].astype(BF16)

    xp = x_ref[...]
    xp = jnp.where(lax.broadcasted_iota(jnp.int32, xp.shape, 0) < nv_ref[i], xp, jnp.uint32(0))
    x = _unpack_pairs(xp).astype(BF16)
    half = PIPE_COLS // 2
    cols = lambda c: slice(c * half, (c + 1) * half)

    def produce(c):
        return (jnp.dot(x, wg_bf[:, cols(c)], preferred_element_type=F32),
                jnp.dot(x, wu_bf[:, cols(c)], preferred_element_type=F32))

    def consume(c, gu):
        o_ref[:, cols(c)] = (_silu(gu[0]) * gu[1]).astype(o_ref.dtype)

    _pipelined(o_ref.shape[1] // half, produce, consume)


def moe_up(block_e, block_valid, xb, wg, wu, tf=1792):
    ns = xb.shape[0]
    d, ff = wg.shape[1], wg.shape[2]
    nb = ns // MOE_BLOCK
    grid_spec = pltpu.PrefetchScalarGridSpec(
        num_scalar_prefetch=2,
        grid=(ff // tf, nb),
        in_specs=[pl.BlockSpec((MOE_BLOCK, d // 2), lambda f, i, be, nv: (i, 0)),
                  pl.BlockSpec((None, d, tf), lambda f, i, be, nv: (be[i], 0, f)),
                  pl.BlockSpec((None, d, tf), lambda f, i, be, nv: (be[i], 0, f))],
        out_specs=pl.BlockSpec((MOE_BLOCK, tf), lambda f, i, be, nv: (i, f)),
        scratch_shapes=[pltpu.VMEM((d, tf), BF16), pltpu.VMEM((d, tf), BF16)],
    )
    return pl.pallas_call(
        _moe_up_kernel,
        grid_spec=grid_spec,
        out_shape=jax.ShapeDtypeStruct((ns, ff), BF16),
        compiler_params=_cparams(("arbitrary", "arbitrary")),
        name="moe_up",
    )(block_e, block_valid, xb, wg, wu)


def _moe_down_kernel(be_ref, hid_ref, wd_ref, o_ref, wd_bf):
    @pl.when(_expert_changed(be_ref, pl.program_id(0)))
    def _():
        wd_bf[...] = wd_ref[...].astype(BF16)

    o_ref[...] = _pack_pairs(jnp.dot(hid_ref[...], wd_bf[...], preferred_element_type=F32))


def moe_down(block_e, hid, wd):
    ns, ff = hid.shape
    d = wd.shape[2]
    nb = ns // MOE_BLOCK
    grid_spec = pltpu.PrefetchScalarGridSpec(
        num_scalar_prefetch=1,
        grid=(nb,),
        in_specs=[pl.BlockSpec((MOE_BLOCK, ff), lambda i, be: (i, 0)),
                  pl.BlockSpec((None, ff, d), lambda i, be: (be[i], 0, 0))],
        out_specs=pl.BlockSpec((MOE_BLOCK, d // 2), lambda i, be: (i, 0)),
        scratch_shapes=[pltpu.VMEM((ff, d), BF16)],
    )
    return pl.pallas_call(
        _moe_down_kernel,
        grid_spec=grid_spec,
        out_shape=jax.ShapeDtypeStruct((ns, d // 2), jnp.uint32),
        compiler_params=_cparams(("arbitrary",)),
        name="moe_down",
    )(block_e, hid, wd)


def _final_kernel(h_ref, y0_ref, y1_ref, r_ref, nw_ref, o_ref):
    gates = r_ref[...]
    hn = (h_ref[...] + gates[:, TOP_K:TOP_K + 1] * _unpack_pairs(y0_ref[...])
          + gates[:, TOP_K + 1:TOP_K + 2] * _unpack_pairs(y1_ref[...]))
    y = hn * lax.rsqrt(jnp.mean(hn * hn, axis=-1, keepdims=True) + NORM_EPS)
    o_ref[...] = y * nw_ref[...]


def final_combine(h, yk, r, norm_w, tm=1024):
    n, d = h.shape
    tm = min(tm, n)
    nblk = n // tm
    row = pl.BlockSpec((tm, d), lambda i: (i, 0))
    return pl.pallas_call(
        _final_kernel,
        grid=(nblk,),
        in_specs=[row, pl.BlockSpec((tm, d // 2), lambda i: (i, 0)),
                  pl.BlockSpec((tm, d // 2), lambda i: (i + nblk, 0)),
                  pl.BlockSpec((tm, 128), lambda i: (i, 0)), pl.BlockSpec((1, d), lambda i: (0, 0))],
        out_specs=row,
        out_shape=jax.ShapeDtypeStruct((n, d), F32),
        compiler_params=_cparams(("parallel",)),
        name="final_combine",
    )(h, yk, yk, r, norm_w.reshape(1, d))


def _chunk_rows(x, bsz, seq):
    hh = x.shape[1]
    return jnp.transpose(x.reshape(bsz, seq, hh), (0, 2, 1)).reshape(bsz, hh, seq // CHUNK, CHUNK)


def hybrid_mixer_layer(h, u, bsz, seq, w_in, gdn_conv_w, gdn_A_log, gdn_dt_bias, gdn_norm_w, gdn_proj,
                       ssm_conv_w, ssm_conv_b, ssm_A_log, ssm_dt_bias, ssm_D, ssm_norm_w, ssm_proj, w_out,
                       next_norm_w, packed_u):
    c0 = 0
    c1 = c0 + GDN_CONV_DIM
    c2 = c1 + GDN_V_DIM
    c3 = c2 + GDN_V_HEADS
    c4 = c3 + GDN_V_HEADS
    c5 = c4 + SSM_D_INNER
    c6 = c5 + SSM_CONV_DIM
    c7 = c6 + SSM_HEADS
    c8 = c7 + D_MODEL
    wb = w_in.astype(BF16)
    qkv = proj_act(u, wb[:, c0:c1], gdn_conv_w, jnp.zeros((GDN_CONV_DIM,), F32), seq, name="proj_qkv")
    z_a = proj_act(u, wb[:, c1:c2], None, None, seq, name="proj_za")
    z_b = proj_act(u, wb[:, c4:c5], None, None, seq, name="proj_zb")
    xbc = proj_act(u, wb[:, c5:c6], ssm_conv_w, ssm_conv_b, seq, name="proj_xbc")
    gate_a = matmul(u, wb[:, c7:c8], BF16, name="proj_ga")
    gate_b = matmul(u, wb[:, c8:], BF16, name="proj_gb")
    n_small = 2 * GDN_V_HEADS + SSM_HEADS
    w_small = jnp.concatenate([w_in[:, c2:c4], w_in[:, c6:c7]], axis=1)
    w_small = jnp.pad(w_small, ((0, 0), (0, 128 - n_small)))
    small = matmul(u, w_small, F32, tn=128, full_precision=True, name="proj_small")
    a4 = _chunk_rows(small[:, :GDN_V_HEADS], bsz, seq)
    b4 = _chunk_rows(small[:, GDN_V_HEADS:2 * GDN_V_HEADS], bsz, seq)
    dt_raw = small[:, 2 * GDN_V_HEADS:n_small]

    oa = gdn_mixer(qkv, z_a, a4, b4, gdn_A_log, gdn_dt_bias, gdn_norm_w, bsz=bsz, seq=seq)
    ob = ssd_mixer(xbc, z_b, dt_raw, ssm_A_log, ssm_dt_bias, ssm_D, ssm_norm_w, bsz=bsz, seq=seq)
    return merge_out(h, oa, ob, gate_a, gate_b, gdn_proj.astype(BF16), ssm_proj.astype(BF16),
                     w_out.astype(BF16), next_norm_w, packed_u)


def moe_layer(h, u, router_w, w_gate, w_up, w_down, final_norm_w):
    n, d = h.shape
    n_assign = n * TOP_K
    r = router(u, jnp.pad(router_w, ((0, 0), (0, 128 - N_EXPERTS))))
    top_idx = r[:, :TOP_K].astype(jnp.int32)
    flat_e = top_idx.reshape(-1)
    onehot = (flat_e[:, None] == jnp.arange(N_EXPERTS)[None, :]).astype(jnp.int32)
    csum = jnp.cumsum(onehot, axis=0)
    counts = csum[-1]
    rank = jnp.sum((csum - onehot) * onehot, axis=1)
    padded = (counts + MOE_BLOCK - 1) // MOE_BLOCK * MOE_BLOCK
    ends = jnp.cumsum(padded)
    pstart = ends - padded
    dest = (pstart[flat_e] + rank).astype(jnp.int32)
    n_blocks = -(-n_assign // MOE_BLOCK) + N_EXPERTS
    n_slots = n_blocks * MOE_BLOCK
    block_start = jnp.arange(n_blocks, dtype=jnp.int32) * MOE_BLOCK
    block_e = jnp.minimum(jnp.sum(block_start[:, None] >= ends[None, :], axis=1), N_EXPERTS - 1).astype(jnp.int32)
    block_valid = jnp.clip((pstart + counts)[block_e] - block_start, 0, MOE_BLOCK).astype(jnp.int32)
    dest_km = dest.reshape(n, TOP_K).T
    xb = sc_scatter_rows(u, dest_km, n_slots)
    hid = moe_up(block_e, block_valid, xb, w_gate, w_up)
    yb = moe_down(block_e, hid, w_down)
    yk = sc_gather_rows(yb, dest_km.reshape(-1))
    return final_combine(h, yk, r, final_norm_w)


def kernel(x, mix_norm_w, w_in, gdn_conv_w, gdn_A_log, gdn_dt_bias, gdn_norm_w, gdn_proj, ssm_conv_w, ssm_conv_b, ssm_A_log, ssm_dt_bias, ssm_D, ssm_norm_w, ssm_proj, w_out, ffn_norm_w, dense_w_gate, dense_w_up, dense_w_down, router_w, moe_w_gate, moe_w_up, moe_w_down, final_norm_w):
    bsz, seq, d = x.shape
    assert d == D_MODEL and w_in.shape[0] == 2, "dense-FFN layer followed by a final MoE layer"
    h = x.reshape(bsz * seq, d)
    u = rmsnorm(h, mix_norm_w[0], BF16)

    def mixer(layer, h, u, packed_u):
        return hybrid_mixer_layer(
            h, u, bsz, seq, w_in[layer], gdn_conv_w[layer], gdn_A_log[layer], gdn_dt_bias[layer],
            gdn_norm_w[layer], gdn_proj[layer], ssm_conv_w[layer], ssm_conv_b[layer], ssm_A_log[layer],
            ssm_dt_bias[layer], ssm_D[layer], ssm_norm_w[layer], ssm_proj[layer], w_out[layer],
            ffn_norm_w[layer], packed_u)

    h, u = mixer(0, h, u, False)
    h, u = dense_ffn(h, u, dense_w_gate[0].astype(BF16), dense_w_up[0].astype(BF16),
                     dense_w_down[0].astype(BF16), mix_norm_w[1])
    h, u = mixer(1, h, u, True)
    out = moe_layer(h, u, router_w[0], moe_w_gate[0], moe_w_up[0], moe_w_down[0], final_norm_w)
    return out.reshape(bsz, seq, d)
```

```python
import functools

import jax
import jax.numpy as jnp
from jax import lax
from jax.experimental import pallas as pl
from jax.experimental.pallas import tpu as pltpu
from jax.experimental.pallas import tpu_sc as plsc

F32 = jnp.float32
BF16 = jnp.bfloat16

D_MODEL = 1024
CONV_K = 4
CHUNK = 64
GDN_QK_HEADS = 4
GDN_V_HEADS = 8
GDN_HEAD = 128
GDN_QK_DIM = GDN_QK_HEADS * GDN_HEAD
GDN_V_DIM = GDN_V_HEADS * GDN_HEAD
GDN_CONV_DIM = 2 * GDN_QK_DIM + GDN_V_DIM
SSM_D_INNER = 2048
SSM_HEAD_DIM = 64
SSM_HEADS = SSM_D_INNER // SSM_HEAD_DIM
SSM_GROUPS = 4
SSM_HPG = SSM_HEADS // SSM_GROUPS
SSM_GROUP_DIM = SSM_D_INNER // SSM_GROUPS
SSM_STATE = 128
SSM_CONV_DIM = SSM_D_INNER + 2 * SSM_GROUPS * SSM_STATE
N_EXPERTS = 8
TOP_K = 2
MOE_BLOCK = 512
NORM_EPS = 1e-6
SSM_NORM_EPS = 1e-5
CONV_HALO = 16

VMEM_LIMIT = 56 * 1024 * 1024
PIPE_COLS = 512
SC_CORES = 2
SC_SUBCORES = 16
SC_GATHER_CHUNK = 128


def _cparams(sem):
    return pltpu.CompilerParams(dimension_semantics=sem, vmem_limit_bytes=VMEM_LIMIT)


def _silu(x):
    h = 0.5 * x
    return h + h * jnp.tanh(h)


def _softplus(x):
    return jnp.maximum(x, 0.0) + jnp.log1p(jnp.exp(-jnp.abs(x)))


def _dot(a, b):
    return jnp.dot(a.astype(BF16), b.astype(BF16), preferred_element_type=F32)


def _dot_nt(a, b):
    return lax.dot_general(a.astype(BF16), b.astype(BF16), (((1,), (1,)), ((), ())),
                           preferred_element_type=F32)


def _dot_tn(a, b):
    return lax.dot_general(a.astype(BF16), b.astype(BF16), (((0,), (0,)), ((), ())),
                           preferred_element_type=F32)


def _dot_hi(a, b):
    return jnp.dot(a, b, preferred_element_type=F32, precision=lax.Precision.HIGHEST)


def _pack_pairs(x):
    half = x.shape[1] // 2
    bits = lax.bitcast_convert_type(x.astype(BF16).astype(F32), jnp.uint32)
    return (bits[:, :half] >> 16) | (bits[:, half:] & jnp.uint32(0xFFFF0000))


def _unpack_pairs(p):
    lo = lax.bitcast_convert_type(p << 16, F32)
    hi = lax.bitcast_convert_type(p & jnp.uint32(0xFFFF0000), F32)
    return jnp.concatenate([lo, hi], axis=1)


def _rmsnorm_kernel(x_ref, w_ref, o_ref):
    x = x_ref[...]
    y = x * lax.rsqrt(jnp.mean(x * x, axis=-1, keepdims=True) + NORM_EPS)
    o_ref[...] = (y * w_ref[...]).astype(o_ref.dtype)


def rmsnorm(x, w, out_dtype, tm=1024):
    n, d = x.shape
    tm = min(tm, n)
    return pl.pallas_call(
        _rmsnorm_kernel,
        grid=(n // tm,),
        in_specs=[pl.BlockSpec((tm, d), lambda i: (i, 0)), pl.BlockSpec((1, d), lambda i: (0, 0))],
        out_specs=pl.BlockSpec((tm, d), lambda i: (i, 0)),
        out_shape=jax.ShapeDtypeStruct((n, d), out_dtype),
        compiler_params=_cparams(("parallel",)),
        name="rmsnorm",
    )(x, w.reshape(1, d))


def _split3(w):
    w_hi = w.astype(BF16)
    w_mid = (w - w_hi.astype(F32)).astype(BF16)
    w_lo = (w - w_hi.astype(F32) - w_mid.astype(F32)).astype(BF16)
    return jnp.concatenate([w_hi, w_mid, w_lo], axis=1)


def _dot_parts(x, w_parts, parts):
    r = jnp.dot(x, w_parts, preferred_element_type=F32)
    tn = w_parts.shape[1] // parts
    acc = r[:, :tn]
    for p in range(1, parts):
        acc = acc + r[:, p * tn:(p + 1) * tn]
    return acc


def _matmul_kernel(x_ref, w_ref, o_ref, *, parts):
    o_ref[...] = _dot_parts(x_ref[...], w_ref[...], parts).astype(o_ref.dtype)


def matmul(x, w, out_dtype, tm=1024, tn=1024, full_precision=False, name="matmul"):
    n, k = x.shape
    m = w.shape[1]
    tm = min(tm, n)
    tn = min(tn, m)
    parts = 1
    if full_precision:
        assert m == tn
        w = _split3(w)
        parts = 3
    return pl.pallas_call(
        functools.partial(_matmul_kernel, parts=parts),
        grid=(m // tn, n // tm),
        in_specs=[pl.BlockSpec((tm, k), lambda j, i: (i, 0)),
                  pl.BlockSpec((k, parts * tn), lambda j, i: (0, j))],
        out_specs=pl.BlockSpec((tm, tn), lambda j, i: (i, j)),
        out_shape=jax.ShapeDtypeStruct((n, m), out_dtype),
        compiler_params=_cparams(("parallel", "parallel")),
        name=name,
    )(x, w)


def _pipelined(n, produce, consume):
    cur = produce(0)
    for i in range(n):
        nxt = produce(i + 1) if i + 1 < n else None
        consume(i, cur)
        cur = nxt


def _proj_silu_kernel(x_ref, w_ref, o_ref):
    x = x_ref[...]
    cols = lambda c: slice(c * PIPE_COLS, (c + 1) * PIPE_COLS)

    def consume(c, raw):
        o_ref[:, cols(c)] = _silu(raw).astype(o_ref.dtype)

    _pipelined(o_ref.shape[1] // PIPE_COLS,
               lambda c: jnp.dot(x, w_ref[:, cols(c)], preferred_element_type=F32), consume)


def _proj_conv_silu_kernel(x_ref, w_ref, cw_ref, cb_ref, o_ref, carry_ref, *, tiles_per_seq):
    x = x_ref[...]
    first = pl.program_id(1) % tiles_per_seq == 0
    cols = lambda c: slice(c * PIPE_COLS, (c + 1) * PIPE_COLS)

    def roll2(v):
        words = pltpu.roll(pltpu.bitcast(v, jnp.uint32), 1, 0)
        return pltpu.bitcast(words, v.dtype)

    def consume(c, raw):
        prev = jnp.where(first, 0.0, carry_ref[:, cols(c)])
        carry_ref[:, cols(c)] = raw[-CONV_HALO:]
        x = jnp.concatenate([prev, raw], axis=0)
        xb, x1b = x.astype(BF16), pltpu.roll(x, 1, 0).astype(BF16)
        w = cw_ref[:, cols(c)].astype(BF16)
        near = xb * w[3:4] + x1b * w[2:3] + cb_ref[:, cols(c)].astype(BF16)
        far = xb * w[1:2] + x1b * w[0:1]
        h = (near + roll2(far))[CONV_HALO:]
        o_ref[:, cols(c)] = (h + h * jnp.tanh(h)).astype(o_ref.dtype)

    _pipelined(o_ref.shape[1] // PIPE_COLS,
               lambda c: jnp.dot(x, w_ref[:, cols(c)], preferred_element_type=F32), consume)


def proj_act(x, w, conv_w, conv_b, seq, tm=1024, tn=None, name="proj_act"):
    n, k = x.shape
    m = w.shape[1]
    tm = min(tm, seq)
    tn = m if tn is None else tn
    assert seq % tm == 0 and n % seq == 0 and m % tn == 0
    xw_specs = [pl.BlockSpec((tm, k), lambda j, i: (i, 0)), pl.BlockSpec((k, tn), lambda j, i: (0, j))]
    common = dict(
        grid=(m // tn, n // tm),
        out_specs=pl.BlockSpec((tm, tn), lambda j, i: (i, j)),
        out_shape=jax.ShapeDtypeStruct((n, m), BF16),
        name=name,
    )
    if conv_w is None:
        return pl.pallas_call(_proj_silu_kernel, in_specs=xw_specs,
                              compiler_params=_cparams(("parallel", "parallel")), **common)(x, w)
    return pl.pallas_call(
        functools.partial(_proj_conv_silu_kernel, tiles_per_seq=seq // tm),
        in_specs=xw_specs + [pl.BlockSpec((CONV_K, tn), lambda j, i: (0, j)),
                             pl.BlockSpec((1, tn), lambda j, i: (0, j))],
        scratch_shapes=[pltpu.VMEM((CONV_HALO, tn), F32)],
        compiler_params=_cparams(("parallel", "arbitrary")),
        **common,
    )(x, w, 0.5 * conv_w, 0.5 * conv_b.reshape(1, m))


def _col_from_row(row, eye):
    return jnp.sum(jnp.where(eye, row, 0.0), axis=-1, keepdims=True)


def _gdn_kernel(alog_ref, dtb_ref,
                q_ref, k_ref, v_ref, z_ref, a_ref, b_ref, nw_ref,
                o_ref,
                s_ref, *, tt, hpb):
    hblk = pl.program_id(1)
    nck = tt // CHUNK
    rep = GDN_V_HEADS // GDN_QK_HEADS
    dh = GDN_HEAD

    @pl.when(pl.program_id(2) == 0)
    def _():
        s_ref[...] = jnp.zeros_like(s_ref)

    q_all = q_ref[...].astype(F32)
    k_all = k_ref[...].astype(F32)
    v_all = v_ref[...].astype(F32)

    ri = lax.broadcasted_iota(jnp.int32, (CHUNK, CHUNK), 0)
    ci = lax.broadcasted_iota(jnp.int32, (CHUNK, CHUNK), 1)
    upper = (ri <= ci).astype(F32)
    eye = ri == ci
    causal = ri >= ci
    strict = ri > ci
    chunks = [slice(c * CHUNK, (c + 1) * CHUNK) for c in range(nck)]
    nsteps = CHUNK.bit_length() - 1

    qs, ks = [], []
    for j in range(hpb // rep):
        q = q_all[:, j * dh:(j + 1) * dh]
        k = k_all[:, j * dh:(j + 1) * dh]
        qs.append(q * (lax.rsqrt(jnp.sum(q * q, axis=-1, keepdims=True) + 1e-6) * (dh ** -0.5)))
        ks.append(k * lax.rsqrt(jnp.sum(k * k, axis=-1, keepdims=True) + 1e-6))

    heads = range(hpb)
    g_rows, beta_rows, gc_rows = [], [], []
    for hh in heads:
        head = hblk * hpb + hh
        neg_a = -jnp.exp(jnp.full((1, CHUNK), alog_ref[head], F32))
        g_rows.append(neg_a * _softplus(a_ref[hh] + dtb_ref[head]))
        beta_rows.append(jax.nn.sigmoid(b_ref[hh]))
        gc_rows.append(_dot_hi(g_rows[hh], upper))

    ps, xs, qkk, qes, egl = {}, {}, {}, {}, {}

    def local_prep(c):
        sl = chunks[c]
        kq = [_dot_nt(jnp.concatenate([ks[j][sl], qs[j][sl]], axis=0), ks[j][sl]) for j in range(hpb // rep)]
        for hh in heads:
            qc, kc = qs[hh // rep][sl], ks[hh // rep][sl]
            vc = v_all[sl, hh * dh:(hh + 1) * dh]
            gc_row = gc_rows[hh][c:c + 1, :]
            gc_col = _col_from_row(gc_row, eye)
            beta_col = _col_from_row(beta_rows[hh][c:c + 1, :], eye)
            g_last = jnp.sum(g_rows[hh][c:c + 1, :], axis=-1, keepdims=True)
            decay = jnp.where(causal, jnp.exp(gc_col - gc_row), 0.0)
            eg_col = jnp.exp(gc_col)
            kk, qk = kq[hh // rep][:CHUNK], kq[hh // rep][CHUNK:]
            ps[hh, c] = jnp.where(strict, kk * decay, 0.0) * beta_col
            xs[hh, c] = jnp.concatenate([vc * beta_col, kc * (beta_col * eg_col)], axis=1)
            k_dec = kc * jnp.exp(g_last - gc_col)
            qkk[hh, c] = jnp.concatenate([qk * decay, k_dec.T], axis=0)
            qes[hh, c] = qc * eg_col
            egl[hh, c] = jnp.exp(g_last)

    def solve_step(c, step):
        for hh in heads:
            u = (hh, c)
            if step + 1 < nsteps:
                r = _dot(ps[u], jnp.concatenate([xs[u], ps[u]], axis=1))
                ps[u] = r[:, 2 * dh:]
                r = r[:, :2 * dh]
            else:
                r = _dot(ps[u], xs[u])
            xs[u] = xs[u] - r if step == 0 else xs[u] + r

    s = [s_ref[hh] for hh in heads]
    v_new = {}
    o_chunks = [[] for _ in heads]

    def rec_a(c):
        for hh in heads:
            x = xs[hh, c]
            wq_s = _dot(jnp.concatenate([x[:, dh:], qes[hh, c]], axis=0), s[hh])
            v_new[hh] = x[:, :dh] - wq_s[:CHUNK]
            o_chunks[hh].append(wq_s[CHUNK:])

    def rec_b(c):
        for hh in heads:
            r = _dot(qkk[hh, c], v_new[hh])
            o_chunks[hh][c] = o_chunks[hh][c] + r[:CHUNK]
            s[hh] = s[hh] * egl[hh, c] + r[CHUNK:]

    half = nsteps // 2
    local_prep(0)
    for step in range(nsteps):
        solve_step(0, step)
    for c in range(nck):
        nxt = c + 1 < nck
        if nxt:
            local_prep(c + 1)
        rec_a(c)
        if nxt:
            for step in range(half):
                solve_step(c + 1, step)
        rec_b(c)
        if nxt:
            for step in range(half, nsteps):
                solve_step(c + 1, step)
    outs = []
    for hh in range(hpb):
        s_ref[hh] = s[hh]
        o = jnp.concatenate(o_chunks[hh], axis=0)
        outs.append(o * lax.rsqrt(jnp.mean(o * o, axis=-1, keepdims=True) + NORM_EPS) * nw_ref[...])
    y = jnp.concatenate(outs, axis=1)
    o_ref[...] = (y * z_ref[...].astype(F32)).astype(o_ref.dtype)


def gdn_mixer(qkv, z_a, a4, b4, a_log, dt_bias, norm_w, *, bsz, seq, tt=512, hpb=GDN_V_HEADS):
    tt = min(tt, seq)
    n = bsz * seq
    nt = seq // tt
    rep = GDN_V_HEADS // GDN_QK_HEADS
    nhb = GDN_V_HEADS // hpb
    qw = hpb // rep * GDN_HEAD
    vw = hpb * GDN_HEAD
    q0, k0, v0 = 0, GDN_QK_DIM // qw, 2 * GDN_QK_DIM // vw

    def cur(width, c0):
        return pl.BlockSpec((tt, width), lambda b, h, t, *_: (b * nt + t, c0 + h))

    small = pl.BlockSpec((None, hpb, None, tt // CHUNK, CHUNK), lambda b, h, t, *_: (b, h, t, 0, 0))
    a4 = a4.reshape(bsz, GDN_V_HEADS, nt, tt // CHUNK, CHUNK)
    b4 = b4.reshape(bsz, GDN_V_HEADS, nt, tt // CHUNK, CHUNK)
    grid_spec = pltpu.PrefetchScalarGridSpec(
        num_scalar_prefetch=2,
        grid=(bsz, nhb, nt),
        in_specs=[cur(qw, q0), cur(qw, k0), cur(vw, v0), cur(vw, 0), small, small,
                  pl.BlockSpec((1, GDN_HEAD), lambda b, h, t, *_: (0, 0))],
        out_specs=cur(vw, 0),
        scratch_shapes=[pltpu.VMEM((hpb, GDN_HEAD, GDN_HEAD), F32)],
    )
    return pl.pallas_call(
        functools.partial(_gdn_kernel, tt=tt, hpb=hpb),
        grid_spec=grid_spec,
        out_shape=jax.ShapeDtypeStruct((n, GDN_V_DIM), BF16),
        compiler_params=_cparams(("parallel", "parallel", "arbitrary")),
        name="gdn_mixer",
    )(a_log, dt_bias, qkv, qkv, qkv, z_a, a4, b4, norm_w.reshape(1, GDN_HEAD))


def _ssd_kernel(x_ref, b_ref, c_ref, z_ref, dt_ref, alog_ref, dtb_ref, dskip_ref, nw_ref,
                o_ref,
                h_ref, *, tt):
    nck = tt // CHUNK
    pw = 2 * SSM_HEAD_DIM
    npair = SSM_GROUP_DIM // pw

    @pl.when(pl.program_id(2) == 0)
    def _():
        h_ref[...] = jnp.zeros_like(h_ref)

    xs = x_ref[...].astype(F32)
    bm = b_ref[...]
    cm = c_ref[...]

    dt_rows = _softplus(dt_ref[...] + dtb_ref[...])
    adt_rows = -jnp.exp(alog_ref[...]) * dt_rows
    r2 = lax.broadcasted_iota(jnp.int32, (pw, pw), 0)
    c2 = lax.broadcasted_iota(jnp.int32, (pw, pw), 1)
    same_head = (r2 // CHUNK) == (c2 // CHUNK)
    cum_tot = jnp.concatenate([(same_head & (r2 <= c2)).astype(F32), same_head.astype(F32)], axis=1)
    adt_pc = jnp.concatenate([adt_rows[:, p * pw:(p + 1) * pw] for p in range(npair)], axis=0)
    ct = _dot_hi(adt_pc, cum_tot)
    acs_pc, tot_pc = ct[:, :pw], ct[:, pw:]

    li = lax.broadcasted_iota(jnp.int32, (CHUNK, pw), 0)
    ji = lax.broadcasted_iota(jnp.int32, (CHUNK, pw), 1)
    lo_half = ji < CHUNK
    pick_a = ji == li
    pick_b = ji == li + CHUNK
    causal2 = li >= jnp.where(lo_half, ji, ji - CHUNK)
    rb = lax.broadcasted_iota(jnp.int32, (pw, pw), 0)
    cb_ = lax.broadcasted_iota(jnp.int32, (pw, pw), 1)
    blockdiag = (rb < CHUNK) == (cb_ < CHUNK)

    def pair_col(row):
        a = jnp.sum(jnp.where(pick_a, row, 0.0), axis=-1, keepdims=True)
        b = jnp.sum(jnp.where(pick_b, row, 0.0), axis=-1, keepdims=True)
        return jnp.where(lo_half, a, b)

    hstate = h_ref[...]
    ys = []
    for c in range(nck):
        sl = slice(c * CHUNK, (c + 1) * CHUNK)
        cmc, bmc = cm[sl], bm[sl]
        cb2 = _dot_nt(cmc, jnp.concatenate([bmc, bmc], axis=0))
        yd_parts, eacs_parts, xdec_parts, eal_parts = [], [], [], []
        for p in range(npair):
            row = p * nck + c
            acs_row = acs_pc[row:row + 1, :]
            tot_row = tot_pc[row:row + 1, :]
            acs_col = pair_col(acs_row)
            dt_col = pair_col(dt_rows[c:c + 1, p * pw:(p + 1) * pw])
            lmat = jnp.where(causal2, jnp.exp(acs_col - acs_row), 0.0)
            xdt = xs[sl, p * pw:(p + 1) * pw] * dt_col
            xdt2 = jnp.where(blockdiag, jnp.concatenate([xdt, xdt], axis=0), 0.0)
            yd_parts.append(_dot(cb2 * lmat, xdt2))
            eacs_parts.append(jnp.exp(acs_col))
            xdec_parts.append(xdt * jnp.exp(tot_row - acs_col))
            eal_parts.append(jnp.exp(tot_row))
        upd = _dot_tn(bmc, jnp.concatenate(xdec_parts, axis=1))
        y_off = _dot(cmc, hstate)
        ys.append(jnp.concatenate(yd_parts, axis=1) + y_off * jnp.concatenate(eacs_parts, axis=1))
        hstate = hstate * jnp.concatenate(eal_parts, axis=1) + upd
    h_ref[...] = hstate
    y = jnp.concatenate(ys, axis=0) + xs * dskip_ref[...]
    y = y * z_ref[...].astype(F32)
    y = y * lax.rsqrt(jnp.mean(y * y, axis=-1, keepdims=True) + SSM_NORM_EPS) * nw_ref[...]
    o_ref[...] = y.astype(o_ref.dtype)


def ssd_mixer(xbc, z_b, dt_raw, a_log, dt_bias, d_skip, norm_w, *, bsz, seq, tt=2048):
    tt = min(tt, seq)
    n = bsz * seq
    nt = seq // tt
    nck = tt // CHUNK
    gd = SSM_GROUP_DIM
    st = SSM_STATE
    x_blocks = SSM_D_INNER // st
    dtl = jnp.transpose(dt_raw.reshape(bsz, seq // CHUNK, CHUNK, SSM_HEADS), (0, 1, 3, 2))
    dtl = dtl.reshape(bsz, nt, nck, SSM_D_INNER)
    per_head = lambda v: jnp.repeat(v, SSM_HEAD_DIM).reshape(1, SSM_D_INNER)

    def cur(width, colf):
        return pl.BlockSpec((tt, width), lambda b, g, t, *_: (b * nt + t, colf(g)))

    def rowspec(rows, width, colf):
        return pl.BlockSpec((rows, width), lambda b, g, t, *_: (0, colf(g)))

    xcol = lambda g: g
    bcol = lambda g: x_blocks + g
    ccol = lambda g: x_blocks + SSM_GROUPS + g
    return pl.pallas_call(
        functools.partial(_ssd_kernel, tt=tt),
        grid=(bsz, SSM_GROUPS, nt),
        in_specs=[cur(gd, xcol), cur(st, bcol), cur(st, ccol), cur(gd, xcol),
                  pl.BlockSpec((None, None, nck, gd), lambda b, g, t: (b, t, 0, g)),
                  rowspec(1, gd, xcol), rowspec(1, gd, xcol), rowspec(1, gd, xcol), rowspec(1, gd, xcol)],
        out_specs=cur(gd, xcol),
        out_shape=jax.ShapeDtypeStruct((n, SSM_D_INNER), BF16),
        scratch_shapes=[pltpu.VMEM((SSM_STATE, gd), F32)],
        compiler_params=_cparams(("parallel", "parallel", "arbitrary")),
        name="ssd_mixer",
    )(xbc, xbc, xbc, z_b, dtl, per_head(a_log), per_head(dt_bias), per_head(d_skip),
      norm_w.reshape(1, SSM_D_INNER))


def _merge_kernel(h_ref, oa_ref, ob_ref, ga_ref, gb_ref, wa_ref, wb_ref, wo_ref, nw_ref, hn_ref, u_ref):
    half = h_ref.shape[0] // 2
    rows = lambda r: slice(r * half, (r + 1) * half)

    def produce(r):
        return (jnp.dot(oa_ref[rows(r), :], wa_ref[...], preferred_element_type=F32),
                jnp.dot(ob_ref[rows(r), :], wb_ref[...], preferred_element_type=F32))

    def consume(r, ab):
        mixed = (jax.nn.sigmoid(ga_ref[rows(r), :].astype(F32)) * ab[0]
                 + jax.nn.sigmoid(gb_ref[rows(r), :].astype(F32)) * ab[1])
        hn = h_ref[rows(r), :] + jnp.dot(mixed.astype(BF16), wo_ref[...], preferred_element_type=F32)
        hn_ref[rows(r), :] = hn
        y = hn * lax.rsqrt(jnp.mean(hn * hn, axis=-1, keepdims=True) + NORM_EPS) * nw_ref[...]
        u_ref[rows(r), :] = _pack_pairs(y) if u_ref.dtype == jnp.uint32 else y.astype(u_ref.dtype)

    _pipelined(2, produce, consume)


def merge_out(h, oa, ob, ga, gb, wa, wb, wo, next_norm_w, packed_u, tm=512):
    n, d = h.shape
    u_sds = jax.ShapeDtypeStruct((n, d // 2), jnp.uint32) if packed_u else jax.ShapeDtypeStruct((n, d), BF16)
    tm = min(tm, n)
    row = lambda width: pl.BlockSpec((tm, width), lambda i: (i, 0))
    full = lambda r, c: pl.BlockSpec((r, c), lambda i: (0, 0))
    return pl.pallas_call(
        _merge_kernel,
        grid=(n // tm,),
        in_specs=[row(d), row(GDN_V_DIM), row(SSM_D_INNER), row(d), row(d),
                  full(GDN_V_DIM, d), full(SSM_D_INNER, d), full(d, d), full(1, d)],
        out_specs=[row(d), row(u_sds.shape[1])],
        out_shape=[jax.ShapeDtypeStruct((n, d), F32), u_sds],
        compiler_params=_cparams(("parallel",)),
        name="merge_out",
    )(h, oa, ob, ga, gb, wa, wb, wo, next_norm_w.reshape(1, d))


def _ffn_kernel(h_ref, u_ref, wg_ref, wu_ref, wd_ref, nw_ref, hn_ref, un_ref, acc_ref, *, tf):
    u = u_ref[...]
    cols = lambda f: slice(f * tf, (f + 1) * tf)
    acc_ref[...] = h_ref[...]

    def produce(f):
        return (jnp.dot(u, wg_ref[:, cols(f)], preferred_element_type=F32),
                jnp.dot(u, wu_ref[:, cols(f)], preferred_element_type=F32))

    def consume(f, gu):
        hid = (_silu(gu[0]) * gu[1]).astype(BF16)
        acc_ref[...] += jnp.dot(hid, wd_ref[cols(f), :], preferred_element_type=F32)

    _pipelined(wg_ref.shape[1] // tf, produce, consume)
    hn = acc_ref[...]
    hn_ref[...] = hn
    y = hn * lax.rsqrt(jnp.mean(hn * hn, axis=-1, keepdims=True) + NORM_EPS)
    un_ref[...] = (y * nw_ref[...]).astype(un_ref.dtype)


def dense_ffn(h, u, wg, wu, wd, next_norm_w, tm=1024, tf=256):
    n, d = h.shape
    ff = wg.shape[1]
    tm = min(tm, n)
    assert ff % tf == 0
    row = pl.BlockSpec((tm, d), lambda i: (i, 0))
    full = lambda r, c: pl.BlockSpec((r, c), lambda i: (0, 0))
    return pl.pallas_call(
        functools.partial(_ffn_kernel, tf=tf),
        grid=(n // tm,),
        in_specs=[row, row, full(d, ff), full(d, ff), full(ff, d), full(1, d)],
        out_specs=[row, row],
        out_shape=[jax.ShapeDtypeStruct((n, d), F32), jax.ShapeDtypeStruct((n, d), BF16)],
        scratch_shapes=[pltpu.VMEM((tm, d), F32)],
        compiler_params=_cparams(("parallel",)),
        name="dense_ffn",
    )(h, u, wg, wu, wd, next_norm_w.reshape(1, d))


def _router_kernel(u_ref, w_ref, o_ref):
    logits = _dot_parts(_unpack_pairs(u_ref[...]).astype(BF16), w_ref[...], 3)
    lane = lax.broadcasted_iota(jnp.int32, logits.shape, 1)
    neg = jnp.float32(-3.0e38)
    logits = jnp.where(lane < N_EXPERTS, logits, neg)
    m1 = jnp.max(logits, axis=-1, keepdims=True)
    i1 = jnp.min(jnp.where(logits == m1, lane, 2 * N_EXPERTS), axis=-1, keepdims=True)
    rest = jnp.where(lane == i1, neg, logits)
    m2 = jnp.max(rest, axis=-1, keepdims=True)
    i2 = jnp.min(jnp.where(rest == m2, lane, 2 * N_EXPERTS), axis=-1, keepdims=True)
    e2 = jnp.exp(m2 - m1)
    g1 = 1.0 / (1.0 + e2)
    g2 = e2 / (1.0 + e2)
    out = jnp.where(lane == 0, i1.astype(F32), 0.0)
    out = jnp.where(lane == 1, i2.astype(F32), out)
    out = jnp.where(lane == 2, g1, out)
    out = jnp.where(lane == 3, g2, out)
    o_ref[...] = out


def router(u, w_pad, tm=1024):
    n, half = u.shape
    tm = min(tm, n)
    return pl.pallas_call(
        _router_kernel,
        grid=(n // tm,),
        in_specs=[pl.BlockSpec((tm, half), lambda i: (i, 0)), pl.BlockSpec((2 * half, 3 * 128), lambda i: (0, 0))],
        out_specs=pl.BlockSpec((tm, 128), lambda i: (i, 0)),
        out_shape=jax.ShapeDtypeStruct((n, 128), F32),
        compiler_params=_cparams(("parallel",)),
        name="router",
    )(u, _split3(w_pad))


def sc_gather_rows(table, idx, chunk=SC_GATHER_CHUNK):
    v, d = table.shape
    b = idx.shape[0]
    nw = SC_CORES * SC_SUBCORES
    per_w = b // nw
    assert per_w * nw == b and per_w % chunk == 0 and chunk % 8 == 0 and chunk <= 128
    mesh = plsc.VectorSubcoreMesh(core_axis_name="c", subcore_axis_name="s")

    @functools.partial(
        pl.kernel, mesh=mesh,
        out_type=jax.ShapeDtypeStruct((b, d), table.dtype),
        scratch_types=[pltpu.VMEM((chunk,), jnp.int32), pltpu.VMEM((chunk, d), table.dtype),
                       pltpu.SemaphoreType.DMA],
    )
    def gather_kernel(table_hbm, idx_hbm, out_hbm, idx_v, rows_v, sem):
        wid = lax.axis_index("s") * SC_CORES + lax.axis_index("c")
        base = wid * per_w

        @pl.loop(0, per_w // chunk)
        def _(j):
            off = pl.multiple_of(base + j * chunk, 8)
            pltpu.sync_copy(idx_hbm.at[pl.ds(off, chunk)], idx_v)
            pltpu.async_copy(table_hbm.at[idx_v], rows_v, sem).wait()
            pltpu.sync_copy(rows_v, out_hbm.at[pl.ds(off, chunk)])

    return gather_kernel(table, idx)


def sc_scatter_rows(src, dest, n_out, chunk=SC_GATHER_CHUNK):
    n, d = src.shape
    nk = dest.shape[0]
    nw = SC_CORES * SC_SUBCORES
    per_w = n // nw
    assert per_w * nw == n and per_w % chunk == 0 and chunk % 8 == 0 and chunk <= 128
    mesh = plsc.VectorSubcoreMesh(core_axis_name="c", subcore_axis_name="s")

    @functools.partial(
        pl.kernel, mesh=mesh,
        out_type=jax.ShapeDtypeStruct((n_out, d), src.dtype),
        scratch_types=[pltpu.VMEM((nk, chunk), jnp.int32), pltpu.VMEM((chunk, d), src.dtype)],
    )
    def scatter_kernel(src_hbm, dest_hbm, out_hbm, idx_v, rows_v):
        wid = lax.axis_index("s") * SC_CORES + lax.axis_index("c")
        base = wid * per_w

        @pl.loop(0, per_w // chunk)
        def _(j):
            off = pl.multiple_of(base + j * chunk, 8)
            pltpu.sync_copy(src_hbm.at[pl.ds(off, chunk)], rows_v)
            for k in range(nk):
                pltpu.sync_copy(dest_hbm.at[k, pl.ds(off, chunk)], idx_v.at[k])
            for k in range(nk):
                pltpu.sync_copy(rows_v, out_hbm.at[idx_v.at[k]])

    return scatter_kernel(src, dest)


def _expert_changed(be_ref, i):
    return jnp.logical_or(i == 0, be_ref[i] != be_ref[jnp.maximum(i - 1, 0)])


def _moe_up_kernel(be_ref, nv_ref, x_ref, wg_ref, wu_ref, o_ref, wg_bf, wu_bf):
    i = pl.program_id(1)

    @pl.when(_expert_changed(be_ref, i))
    def _():
        wg_bf[...] = wg_ref[...].astype(BF16)
        wu_bf[...] = wu_ref[...].astype(BF16)

    @pl.when(nv_ref[i] > 0)
    def _():
        xp = x_ref[...]
        xp = jnp.where(lax.broadcasted_iota(jnp.int32, xp.shape, 0) < nv_ref[i], xp, jnp.uint32(0))
        x = _unpack_pairs(xp).astype(BF16)
        half = PIPE_COLS // 2
        cols = lambda c: slice(c * half, (c + 1) * half)

        def produce(c):
            return (jnp.dot(x, wg_bf[:, cols(c)], preferred_element_type=F32),
                    jnp.dot(x, wu_bf[:, cols(c)], preferred_element_type=F32))

        def consume(c, gu):
            o_ref[:, cols(c)] = (_silu(gu[0]) * gu[1]).astype(o_ref.dtype)

        _pipelined(o_ref.shape[1] // half, produce, consume)


def moe_up(block_e, block_valid, xb, wg, wu, tf=1792):
    ns = xb.shape[0]
    d, ff = wg.shape[1], wg.shape[2]
    nb = ns // MOE_BLOCK
    grid_spec = pltpu.PrefetchScalarGridSpec(
        num_scalar_prefetch=2,
        grid=(ff // tf, nb),
        in_specs=[pl.BlockSpec((MOE_BLOCK, d // 2), lambda f, i, be, nv: (i, 0)),
                  pl.BlockSpec((None, d, tf), lambda f, i, be, nv: (be[i], 0, f)),
                  pl.BlockSpec((None, d, tf), lambda f, i, be, nv: (be[i], 0, f))],
        out_specs=pl.BlockSpec((MOE_BLOCK, tf), lambda f, i, be, nv: (i, f)),
        scratch_shapes=[pltpu.VMEM((d, tf), BF16), pltpu.VMEM((d, tf), BF16)],
    )
    return pl.pallas_call(
        _moe_up_kernel,
        grid_spec=grid_spec,
        out_shape=jax.ShapeDtypeStruct((ns, ff), BF16),
        compiler_params=_cparams(("arbitrary", "arbitrary")),
        name="moe_up",
    )(block_e, block_valid, xb, wg, wu)


def _moe_down_kernel(be_ref, nv_ref, hid_ref, wd_ref, o_ref, wd_bf):
    i = pl.program_id(0)

    @pl.when(_expert_changed(be_ref, i))
    def _():
        wd_bf[...] = wd_ref[...].astype(BF16)

    @pl.when(nv_ref[i] > 0)
    def _():
        o_ref[...] = _pack_pairs(jnp.dot(hid_ref[...], wd_bf[...], preferred_element_type=F32))


def moe_down(block_e, block_valid, hid, wd):
    ns, ff = hid.shape
    d = wd.shape[2]
    nb = ns // MOE_BLOCK
    grid_spec = pltpu.PrefetchScalarGridSpec(
        num_scalar_prefetch=2,
        grid=(nb,),
        in_specs=[pl.BlockSpec((MOE_BLOCK, ff), lambda i, be, nv: (i, 0)),
                  pl.BlockSpec((None, ff, d), lambda i, be, nv: (be[i], 0, 0))],
        out_specs=pl.BlockSpec((MOE_BLOCK, d // 2), lambda i, be, nv: (i, 0)),
        scratch_shapes=[pltpu.VMEM((ff, d), BF16)],
    )
    return pl.pallas_call(
        _moe_down_kernel,
        grid_spec=grid_spec,
        out_shape=jax.ShapeDtypeStruct((ns, d // 2), jnp.uint32),
        compiler_params=_cparams(("arbitrary",)),
        name="moe_down",
    )(block_e, block_valid, hid, wd)


def _final_kernel(h_ref, y0_ref, y1_ref, r_ref, nw_ref, o_ref):
    gates = r_ref[...]
    hn = (h_ref[...] + gates[:, TOP_K:TOP_K + 1] * _unpack_pairs(y0_ref[...])
          + gates[:, TOP_K + 1:TOP_K + 2] * _unpack_pairs(y1_ref[...]))
    y = hn * lax.rsqrt(jnp.mean(hn * hn, axis=-1, keepdims=True) + NORM_EPS)
    o_ref[...] = y * nw_ref[...]


def final_combine(h, yk, r, norm_w, tm=1024):
    n, d = h.shape
    tm = min(tm, n)
    nblk = n // tm
    row = pl.BlockSpec((tm, d), lambda i: (i, 0))
    return pl.pallas_call(
        _final_kernel,
        grid=(nblk,),
        in_specs=[row, pl.BlockSpec((tm, d // 2), lambda i: (i, 0)),
                  pl.BlockSpec((tm, d // 2), lambda i: (i + nblk, 0)),
                  pl.BlockSpec((tm, 128), lambda i: (i, 0)), pl.BlockSpec((1, d), lambda i: (0, 0))],
        out_specs=row,
        out_shape=jax.ShapeDtypeStruct((n, d), F32),
        compiler_params=_cparams(("parallel",)),
        name="final_combine",
    )(h, yk, yk, r, norm_w.reshape(1, d))


def _chunk_rows(x, bsz, seq):
    hh = x.shape[1]
    return jnp.transpose(x.reshape(bsz, seq, hh), (0, 2, 1)).reshape(bsz, hh, seq // CHUNK, CHUNK)


def hybrid_mixer_layer(h, u, bsz, seq, w_in, gdn_conv_w, gdn_A_log, gdn_dt_bias, gdn_norm_w, gdn_proj,
                       ssm_conv_w, ssm_conv_b, ssm_A_log, ssm_dt_bias, ssm_D, ssm_norm_w, ssm_proj, w_out,
                       next_norm_w, packed_u):
    c0 = 0
    c1 = c0 + GDN_CONV_DIM
    c2 = c1 + GDN_V_DIM
    c3 = c2 + GDN_V_HEADS
    c4 = c3 + GDN_V_HEADS
    c5 = c4 + SSM_D_INNER
    c6 = c5 + SSM_CONV_DIM
    c7 = c6 + SSM_HEADS
    c8 = c7 + D_MODEL
    wb = w_in.astype(BF16)
    qkv = proj_act(u, wb[:, c0:c1], gdn_conv_w, jnp.zeros((GDN_CONV_DIM,), F32), seq, name="proj_qkv")
    z_a = proj_act(u, wb[:, c1:c2], None, None, seq, name="proj_za")
    z_b = proj_act(u, wb[:, c4:c5], None, None, seq, name="proj_zb")
    xbc = proj_act(u, wb[:, c5:c6], ssm_conv_w, ssm_conv_b, seq, name="proj_xbc")
    gate_a = matmul(u, wb[:, c7:c8], BF16, name="proj_ga")
    gate_b = matmul(u, wb[:, c8:], BF16, name="proj_gb")
    n_small = 2 * GDN_V_HEADS + SSM_HEADS
    w_small = jnp.concatenate([w_in[:, c2:c4], w_in[:, c6:c7]], axis=1)
    w_small = jnp.pad(w_small, ((0, 0), (0, 128 - n_small)))
    small = matmul(u, w_small, F32, tn=128, full_precision=True, name="proj_small")
    a4 = _chunk_rows(small[:, :GDN_V_HEADS], bsz, seq)
    b4 = _chunk_rows(small[:, GDN_V_HEADS:2 * GDN_V_HEADS], bsz, seq)
    dt_raw = small[:, 2 * GDN_V_HEADS:n_small]

    oa = gdn_mixer(qkv, z_a, a4, b4, gdn_A_log, gdn_dt_bias, gdn_norm_w, bsz=bsz, seq=seq)
    ob = ssd_mixer(xbc, z_b, dt_raw, ssm_A_log, ssm_dt_bias, ssm_D, ssm_norm_w, bsz=bsz, seq=seq)
    return merge_out(h, oa, ob, gate_a, gate_b, gdn_proj.astype(BF16), ssm_proj.astype(BF16),
                     w_out.astype(BF16), next_norm_w, packed_u)


def moe_layer(h, u, router_w, w_gate, w_up, w_down, final_norm_w):
    n, d = h.shape
    n_assign = n * TOP_K
    r = router(u, jnp.pad(router_w, ((0, 0), (0, 128 - N_EXPERTS))))
    top_idx = r[:, :TOP_K].astype(jnp.int32)
    flat_e = top_idx.reshape(-1)
    onehot = (flat_e[:, None] == jnp.arange(N_EXPERTS)[None, :]).astype(jnp.int32)
    csum = jnp.cumsum(onehot, axis=0)
    counts = csum[-1]
    rank = jnp.sum((csum - onehot) * onehot, axis=1)
    padded = (counts + MOE_BLOCK - 1) // MOE_BLOCK * MOE_BLOCK
    ends = jnp.cumsum(padded)
    pstart = ends - padded
    dest = (pstart[flat_e] + rank).astype(jnp.int32)
    n_blocks = -(-n_assign // MOE_BLOCK) + N_EXPERTS
    n_slots = n_blocks * MOE_BLOCK
    block_start = jnp.arange(n_blocks, dtype=jnp.int32) * MOE_BLOCK
    block_e = jnp.minimum(jnp.sum(block_start[:, None] >= ends[None, :], axis=1), N_EXPERTS - 1).astype(jnp.int32)
    block_valid = jnp.clip((pstart + counts)[block_e] - block_start, 0, MOE_BLOCK).astype(jnp.int32)
    dest_km = dest.reshape(n, TOP_K).T
    xb = sc_scatter_rows(u, dest_km, n_slots)
    hid = moe_up(block_e, block_valid, xb, w_gate, w_up)
    yb = moe_down(block_e, block_valid, hid, w_down)
    yk = sc_gather_rows(yb, dest_km.reshape(-1))
    return final_combine(h, yk, r, final_norm_w)


def kernel(x, mix_norm_w, w_in, gdn_conv_w, gdn_A_log, gdn_dt_bias, gdn_norm_w, gdn_proj, ssm_conv_w, ssm_conv_b, ssm_A_log, ssm_dt_bias, ssm_D, ssm_norm_w, ssm_proj, w_out, ffn_norm_w, dense_w_gate, dense_w_up, dense_w_down, router_w, moe_w_gate, moe_w_up, moe_w_down, final_norm_w):
    bsz, seq, d = x.shape
    assert d == D_MODEL and w_in.shape[0] == 2, "dense-FFN layer followed by a final MoE layer"
    h = x.reshape(bsz * seq, d)
    u = rmsnorm(h, mix_norm_w[0], BF16)

    def mixer(layer, h, u, packed_u):
        return hybrid_mixer_layer(
            h, u, bsz, seq, w_in[layer], gdn_conv_w[layer], gdn_A_log[layer], gdn_dt_bias[layer],
            gdn_norm_w[layer], gdn_proj[layer], ssm_conv_w[layer], ssm_conv_b[layer], ssm_A_log[layer],
            ssm_dt_bias[layer], ssm_D[layer], ssm_norm_w[layer], ssm_proj[layer], w_out[layer],
            ffn_norm_w[layer], packed_u)

    h, u = mixer(0, h, u, False)
    h, u = dense_ffn(h, u, dense_w_gate[0].astype(BF16), dense_w_up[0].astype(BF16),
                     dense_w_down[0].astype(BF16), mix_norm_w[1])
    h, u = mixer(1, h, u, True)
    out = moe_layer(h, u, router_w[0], moe_w_gate[0], moe_w_up[0], moe_w_down[0], final_norm_w)
    return out.reshape(bsz, seq, d)
```

```python
import functools

import jax
import jax.numpy as jnp
from jax import lax
from jax.experimental import pallas as pl
from jax.experimental.pallas import tpu as pltpu
from jax.experimental.pallas import tpu_sc as plsc

F32 = jnp.float32
BF16 = jnp.bfloat16

D_MODEL = 1024
CONV_K = 4
CHUNK = 64
GDN_QK_HEADS = 4
GDN_V_HEADS = 8
GDN_HEAD = 128
GDN_QK_DIM = GDN_QK_HEADS * GDN_HEAD
GDN_V_DIM = GDN_V_HEADS * GDN_HEAD
GDN_CONV_DIM = 2 * GDN_QK_DIM + GDN_V_DIM
SSM_D_INNER = 2048
SSM_HEAD_DIM = 64
SSM_HEADS = SSM_D_INNER // SSM_HEAD_DIM
SSM_GROUPS = 4
SSM_HPG = SSM_HEADS // SSM_GROUPS
SSM_GROUP_DIM = SSM_D_INNER // SSM_GROUPS
SSM_STATE = 128
SSM_CONV_DIM = SSM_D_INNER + 2 * SSM_GROUPS * SSM_STATE
N_EXPERTS = 8
TOP_K = 2
MOE_BLOCK = 512
LOG2E = 1.4426950408889634
NORM_EPS = 1e-6
SSM_NORM_EPS = 1e-5
CONV_HALO = 16

VMEM_LIMIT = 56 * 1024 * 1024
PIPE_COLS = 512
SC_CORES = 2
SC_SUBCORES = 16
SC_GATHER_CHUNK = 128


def _cparams(sem):
    return pltpu.CompilerParams(dimension_semantics=sem, vmem_limit_bytes=VMEM_LIMIT)


def _silu(x):
    h = 0.5 * x
    return h + h * jnp.tanh(h)


def _softplus(x):
    return jnp.maximum(x, 0.0) + jnp.log1p(jnp.exp(-jnp.abs(x)))


def _dot(a, b):
    return jnp.dot(a.astype(BF16), b.astype(BF16), preferred_element_type=F32)


def _dot_nt(a, b):
    return lax.dot_general(a.astype(BF16), b.astype(BF16), (((1,), (1,)), ((), ())),
                           preferred_element_type=F32)


def _dot_tn(a, b):
    return lax.dot_general(a.astype(BF16), b.astype(BF16), (((0,), (0,)), ((), ())),
                           preferred_element_type=F32)


def _dot_hi(a, b):
    return jnp.dot(a, b, preferred_element_type=F32, precision=lax.Precision.HIGHEST)


def _pack_pairs(x):
    half = x.shape[1] // 2
    bits = lax.bitcast_convert_type(x.astype(BF16).astype(F32), jnp.uint32)
    return (bits[:, :half] >> 16) | (bits[:, half:] & jnp.uint32(0xFFFF0000))


def _unpack_pairs(p):
    lo = lax.bitcast_convert_type(p << 16, F32)
    hi = lax.bitcast_convert_type(p & jnp.uint32(0xFFFF0000), F32)
    return jnp.concatenate([lo, hi], axis=1)


def _rmsnorm_kernel(x_ref, w_ref, o_ref):
    x = x_ref[...]
    y = x * lax.rsqrt(jnp.mean(x * x, axis=-1, keepdims=True) + NORM_EPS)
    o_ref[...] = (y * w_ref[...]).astype(o_ref.dtype)


def rmsnorm(x, w, out_dtype, tm=1024):
    n, d = x.shape
    tm = min(tm, n)
    return pl.pallas_call(
        _rmsnorm_kernel,
        grid=(n // tm,),
        in_specs=[pl.BlockSpec((tm, d), lambda i: (i, 0)), pl.BlockSpec((1, d), lambda i: (0, 0))],
        out_specs=pl.BlockSpec((tm, d), lambda i: (i, 0)),
        out_shape=jax.ShapeDtypeStruct((n, d), out_dtype),
        compiler_params=_cparams(("parallel",)),
        name="rmsnorm",
    )(x, w.reshape(1, d))


def _split3(w):
    w_hi = w.astype(BF16)
    w_mid = (w - w_hi.astype(F32)).astype(BF16)
    w_lo = (w - w_hi.astype(F32) - w_mid.astype(F32)).astype(BF16)
    return jnp.concatenate([w_hi, w_mid, w_lo], axis=1)


def _dot_parts(x, w_parts, parts):
    r = jnp.dot(x, w_parts, preferred_element_type=F32)
    tn = w_parts.shape[1] // parts
    acc = r[:, :tn]
    for p in range(1, parts):
        acc = acc + r[:, p * tn:(p + 1) * tn]
    return acc


def _matmul_kernel(x_ref, w_ref, o_ref, *, parts):
    o_ref[...] = _dot_parts(x_ref[...], w_ref[...], parts).astype(o_ref.dtype)


def matmul(x, w, out_dtype, tm=1024, tn=1024, full_precision=False, name="matmul"):
    n, k = x.shape
    m = w.shape[1]
    tm = min(tm, n)
    tn = min(tn, m)
    parts = 1
    if full_precision:
        assert m == tn
        w = _split3(w)
        parts = 3
    return pl.pallas_call(
        functools.partial(_matmul_kernel, parts=parts),
        grid=(m // tn, n // tm),
        in_specs=[pl.BlockSpec((tm, k), lambda j, i: (i, 0)),
                  pl.BlockSpec((k, parts * tn), lambda j, i: (0, j))],
        out_specs=pl.BlockSpec((tm, tn), lambda j, i: (i, j)),
        out_shape=jax.ShapeDtypeStruct((n, m), out_dtype),
        compiler_params=_cparams(("parallel", "parallel")),
        name=name,
    )(x, w)


def _pipelined(n, produce, consume):
    cur = produce(0)
    for i in range(n):
        nxt = produce(i + 1) if i + 1 < n else None
        consume(i, cur)
        cur = nxt


def _sigmoid(x):
    return 0.5 + 0.5 * jnp.tanh(0.5 * x)


def _proj_act_kernel(x_ref, w_ref, o_ref, *, act):
    x = x_ref[...]
    cols = lambda c: slice(c * PIPE_COLS, (c + 1) * PIPE_COLS)

    def consume(c, raw):
        o_ref[:, cols(c)] = act(raw).astype(o_ref.dtype)

    _pipelined(o_ref.shape[1] // PIPE_COLS,
               lambda c: jnp.dot(x, w_ref[:, cols(c)], preferred_element_type=F32), consume)


def _proj_conv_silu_kernel(x_ref, w_ref, cw_ref, cb_ref, o_ref, carry_ref, *, tiles_per_seq):
    x = x_ref[...]
    first = pl.program_id(1) % tiles_per_seq == 0
    cols = lambda c: slice(c * PIPE_COLS, (c + 1) * PIPE_COLS)

    def roll2(v):
        words = pltpu.roll(pltpu.bitcast(v, jnp.uint32), 1, 0)
        return pltpu.bitcast(words, v.dtype)

    def consume(c, raw):
        prev = jnp.where(first, 0.0, carry_ref[:, cols(c)])
        carry_ref[:, cols(c)] = raw[-CONV_HALO:]
        x = jnp.concatenate([prev, raw], axis=0)
        xb, x1b = x.astype(BF16), pltpu.roll(x, 1, 0).astype(BF16)
        w = cw_ref[:, cols(c)].astype(BF16)
        near = xb * w[3:4] + x1b * w[2:3] + cb_ref[:, cols(c)].astype(BF16)
        far = xb * w[1:2] + x1b * w[0:1]
        h = (near + roll2(far))[CONV_HALO:]
        o_ref[:, cols(c)] = (h + h * jnp.tanh(h)).astype(o_ref.dtype)

    _pipelined(o_ref.shape[1] // PIPE_COLS,
               lambda c: jnp.dot(x, w_ref[:, cols(c)], preferred_element_type=F32), consume)


def proj_act(x, w, conv_w, conv_b, seq, tm=1024, tn=None, act=_silu, name="proj_act"):
    n, k = x.shape
    m = w.shape[1]
    tm = min(tm, seq)
    tn = m if tn is None else tn
    assert seq % tm == 0 and n % seq == 0 and m % tn == 0
    xw_specs = [pl.BlockSpec((tm, k), lambda j, i: (i, 0)), pl.BlockSpec((k, tn), lambda j, i: (0, j))]
    common = dict(
        grid=(m // tn, n // tm),
        out_specs=pl.BlockSpec((tm, tn), lambda j, i: (i, j)),
        out_shape=jax.ShapeDtypeStruct((n, m), BF16),
        name=name,
    )
    if conv_w is None:
        return pl.pallas_call(functools.partial(_proj_act_kernel, act=act), in_specs=xw_specs,
                              compiler_params=_cparams(("parallel", "parallel")), **common)(x, w)
    return pl.pallas_call(
        functools.partial(_proj_conv_silu_kernel, tiles_per_seq=seq // tm),
        in_specs=xw_specs + [pl.BlockSpec((CONV_K, tn), lambda j, i: (0, j)),
                             pl.BlockSpec((1, tn), lambda j, i: (0, j))],
        scratch_shapes=[pltpu.VMEM((CONV_HALO, tn), F32)],
        compiler_params=_cparams(("parallel", "arbitrary")),
        **common,
    )(x, w, 0.5 * conv_w, 0.5 * conv_b.reshape(1, m))


def _col_from_row(row, eye):
    return jnp.sum(jnp.where(eye, row, 0.0), axis=-1, keepdims=True)


def _gdn_kernel(alog_ref, dtb_ref,
                q_ref, k_ref, v_ref, z_ref, a_ref, b_ref, nw_ref,
                o_ref,
                s_ref, *, tt, hpb):
    hblk = pl.program_id(1)
    nck = tt // CHUNK
    rep = GDN_V_HEADS // GDN_QK_HEADS
    dh = GDN_HEAD

    @pl.when(pl.program_id(2) == 0)
    def _():
        s_ref[...] = jnp.zeros_like(s_ref)

    q_all = q_ref[...].astype(F32)
    k_all = k_ref[...].astype(F32)
    v_all = v_ref[...].astype(F32)

    ri = lax.broadcasted_iota(jnp.int32, (CHUNK, CHUNK), 0)
    ci = lax.broadcasted_iota(jnp.int32, (CHUNK, CHUNK), 1)
    upper = (ri <= ci).astype(F32)
    eye = ri == ci
    causal = ri >= ci
    strict = ri > ci
    chunks = [slice(c * CHUNK, (c + 1) * CHUNK) for c in range(nck)]
    nsteps = CHUNK.bit_length() - 1

    qs, ks = [], []
    for j in range(hpb // rep):
        q = q_all[:, j * dh:(j + 1) * dh]
        k = k_all[:, j * dh:(j + 1) * dh]
        qs.append(q * (lax.rsqrt(jnp.sum(q * q, axis=-1, keepdims=True) + 1e-6) * (dh ** -0.5)))
        ks.append(k * lax.rsqrt(jnp.sum(k * k, axis=-1, keepdims=True) + 1e-6))

    heads = range(hpb)
    g_rows, beta_rows, gc_rows = [], [], []
    for hh in heads:
        head = hblk * hpb + hh
        neg_a = -LOG2E * jnp.exp(jnp.full((1, CHUNK), alog_ref[head], F32))
        g_rows.append(neg_a * _softplus(a_ref[hh] + dtb_ref[head]))
        beta_rows.append(jax.nn.sigmoid(b_ref[hh]))
        gc_rows.append(_dot_hi(g_rows[hh], upper))

    ps, xs, qkk, qes, egl = {}, {}, {}, {}, {}

    def local_prep(c):
        sl = chunks[c]
        kq = [_dot_nt(jnp.concatenate([ks[j][sl], qs[j][sl]], axis=0), ks[j][sl]) for j in range(hpb // rep)]
        for hh in heads:
            qc, kc = qs[hh // rep][sl], ks[hh // rep][sl]
            vc = v_all[sl, hh * dh:(hh + 1) * dh]
            gc_row = gc_rows[hh][c:c + 1, :]
            gc_col = _col_from_row(gc_row, eye)
            beta_col = _col_from_row(beta_rows[hh][c:c + 1, :], eye)
            g_last = jnp.sum(g_rows[hh][c:c + 1, :], axis=-1, keepdims=True)
            decay = jnp.where(causal, jnp.exp2(gc_col - gc_row), 0.0)
            eg_col = jnp.exp2(gc_col)
            kk, qk = kq[hh // rep][:CHUNK], kq[hh // rep][CHUNK:]
            ps[hh, c] = jnp.where(strict, kk * decay, 0.0) * beta_col
            xs[hh, c] = jnp.concatenate([vc * beta_col, kc * (beta_col * eg_col)], axis=1)
            k_dec = kc * jnp.exp2(g_last - gc_col)
            qkk[hh, c] = jnp.concatenate([qk * decay, k_dec.T], axis=0)
            qes[hh, c] = qc * eg_col
            egl[hh, c] = jnp.exp2(g_last)

    def solve_step(c, step):
        for hh in heads:
            u = (hh, c)
            if step + 1 < nsteps:
                r = _dot(ps[u], jnp.concatenate([xs[u], ps[u]], axis=1))
                ps[u] = r[:, 2 * dh:]
                r = r[:, :2 * dh]
            else:
                r = _dot(ps[u], xs[u])
            xs[u] = xs[u] - r if step == 0 else xs[u] + r

    s = [s_ref[hh] for hh in heads]
    v_new = {}
    o_chunks = [[] for _ in heads]

    def rec_a(c):
        for hh in heads:
            x = xs[hh, c]
            wq_s = _dot(jnp.concatenate([x[:, dh:], qes[hh, c]], axis=0), s[hh])
            v_new[hh] = x[:, :dh] - wq_s[:CHUNK]
            o_chunks[hh].append(wq_s[CHUNK:])

    def rec_b(c):
        for hh in heads:
            r = _dot(qkk[hh, c], v_new[hh])
            o_chunks[hh][c] = o_chunks[hh][c] + r[:CHUNK]
            s[hh] = s[hh] * egl[hh, c] + r[CHUNK:]

    half = nsteps // 2
    local_prep(0)
    for step in range(nsteps):
        solve_step(0, step)
    for c in range(nck):
        nxt = c + 1 < nck
        if nxt:
            local_prep(c + 1)
        rec_a(c)
        if nxt:
            for step in range(half):
                solve_step(c + 1, step)
        rec_b(c)
        if nxt:
            for step in range(half, nsteps):
                solve_step(c + 1, step)
    outs = []
    for hh in range(hpb):
        s_ref[hh] = s[hh]
        o = jnp.concatenate(o_chunks[hh], axis=0)
        outs.append(o * lax.rsqrt(jnp.mean(o * o, axis=-1, keepdims=True) + NORM_EPS) * nw_ref[...])
    y = jnp.concatenate(outs, axis=1)
    o_ref[...] = (y * z_ref[...].astype(F32)).astype(o_ref.dtype)


def gdn_mixer(qkv, z_a, a4, b4, a_log, dt_bias, norm_w, *, bsz, seq, tt=512, hpb=GDN_V_HEADS):
    tt = min(tt, seq)
    n = bsz * seq
    nt = seq // tt
    rep = GDN_V_HEADS // GDN_QK_HEADS
    nhb = GDN_V_HEADS // hpb
    qw = hpb // rep * GDN_HEAD
    vw = hpb * GDN_HEAD
    q0, k0, v0 = 0, GDN_QK_DIM // qw, 2 * GDN_QK_DIM // vw

    def cur(width, c0):
        return pl.BlockSpec((tt, width), lambda b, h, t, *_: (b * nt + t, c0 + h))

    small = pl.BlockSpec((None, hpb, None, tt // CHUNK, CHUNK), lambda b, h, t, *_: (b, h, t, 0, 0))
    a4 = a4.reshape(bsz, GDN_V_HEADS, nt, tt // CHUNK, CHUNK)
    b4 = b4.reshape(bsz, GDN_V_HEADS, nt, tt // CHUNK, CHUNK)
    grid_spec = pltpu.PrefetchScalarGridSpec(
        num_scalar_prefetch=2,
        grid=(bsz, nhb, nt),
        in_specs=[cur(qw, q0), cur(qw, k0), cur(vw, v0), cur(vw, 0), small, small,
                  pl.BlockSpec((1, GDN_HEAD), lambda b, h, t, *_: (0, 0))],
        out_specs=cur(vw, 0),
        scratch_shapes=[pltpu.VMEM((hpb, GDN_HEAD, GDN_HEAD), F32)],
    )
    return pl.pallas_call(
        functools.partial(_gdn_kernel, tt=tt, hpb=hpb),
        grid_spec=grid_spec,
        out_shape=jax.ShapeDtypeStruct((n, GDN_V_DIM), BF16),
        compiler_params=_cparams(("parallel", "parallel", "arbitrary")),
        name="gdn_mixer",
    )(a_log, dt_bias, qkv, qkv, qkv, z_a, a4, b4, norm_w.reshape(1, GDN_HEAD))


def _ssd_kernel(x_ref, b_ref, c_ref, z_ref, dt_ref, alog_ref, dtb_ref, dskip_ref, nw_ref,
                o_ref,
                h_ref, *, tt):
    nck = tt // CHUNK
    pw = 2 * SSM_HEAD_DIM
    npair = SSM_GROUP_DIM // pw

    @pl.when(pl.program_id(2) == 0)
    def _():
        h_ref[...] = jnp.zeros_like(h_ref)

    xs = x_ref[...].astype(F32)
    bm = b_ref[...]
    cm = c_ref[...]

    dt_rows = _softplus(dt_ref[...] + dtb_ref[...])
    adt_rows = (-LOG2E * jnp.exp(alog_ref[...])) * dt_rows
    r2 = lax.broadcasted_iota(jnp.int32, (pw, pw), 0)
    c2 = lax.broadcasted_iota(jnp.int32, (pw, pw), 1)
    same_head = (r2 // CHUNK) == (c2 // CHUNK)
    cum_tot = jnp.concatenate([(same_head & (r2 <= c2)).astype(F32), same_head.astype(F32)], axis=1)
    adt_pc = jnp.concatenate([adt_rows[:, p * pw:(p + 1) * pw] for p in range(npair)], axis=0)
    ct = _dot_hi(adt_pc, cum_tot)
    acs_pc, tot_pc = ct[:, :pw], ct[:, pw:]

    li = lax.broadcasted_iota(jnp.int32, (CHUNK, pw), 0)
    ji = lax.broadcasted_iota(jnp.int32, (CHUNK, pw), 1)
    lo_half = ji < CHUNK
    pick_a = ji == li
    pick_b = ji == li + CHUNK
    causal2 = li >= jnp.where(lo_half, ji, ji - CHUNK)
    rb = lax.broadcasted_iota(jnp.int32, (pw, pw), 0)
    cb_ = lax.broadcasted_iota(jnp.int32, (pw, pw), 1)
    blockdiag = (rb < CHUNK) == (cb_ < CHUNK)

    def pair_col(row):
        a = jnp.sum(jnp.where(pick_a, row, 0.0), axis=-1, keepdims=True)
        b = jnp.sum(jnp.where(pick_b, row, 0.0), axis=-1, keepdims=True)
        return jnp.where(lo_half, a, b)

    hstate = h_ref[...]
    ys = []
    for c in range(nck):
        sl = slice(c * CHUNK, (c + 1) * CHUNK)
        cmc, bmc = cm[sl], bm[sl]
        cb2 = _dot_nt(cmc, jnp.concatenate([bmc, bmc], axis=0))
        yd_parts, eacs_parts, xdec_parts, eal_parts = [], [], [], []
        for p in range(npair):
            row = p * nck + c
            acs_row = acs_pc[row:row + 1, :]
            tot_row = tot_pc[row:row + 1, :]
            acs_col = pair_col(acs_row)
            dt_col = pair_col(dt_rows[c:c + 1, p * pw:(p + 1) * pw])
            lmat = jnp.where(causal2, jnp.exp2(acs_col - acs_row), 0.0)
            xdt = xs[sl, p * pw:(p + 1) * pw] * dt_col
            xdt2 = jnp.where(blockdiag, jnp.concatenate([xdt, xdt], axis=0), 0.0)
            yd_parts.append(_dot(cb2 * lmat, xdt2))
            eacs_parts.append(jnp.exp2(acs_col))
            xdec_parts.append(xdt * jnp.exp2(tot_row - acs_col))
            eal_parts.append(jnp.exp2(tot_row))
        upd = _dot_tn(bmc, jnp.concatenate(xdec_parts, axis=1))
        y_off = _dot(cmc, hstate)
        ys.append(jnp.concatenate(yd_parts, axis=1) + y_off * jnp.concatenate(eacs_parts, axis=1))
        hstate = hstate * jnp.concatenate(eal_parts, axis=1) + upd
    h_ref[...] = hstate
    y = jnp.concatenate(ys, axis=0) + xs * dskip_ref[...]
    y = y * z_ref[...].astype(F32)
    y = y * lax.rsqrt(jnp.mean(y * y, axis=-1, keepdims=True) + SSM_NORM_EPS) * nw_ref[...]
    o_ref[...] = y.astype(o_ref.dtype)


def ssd_mixer(xbc, z_b, dt_raw, a_log, dt_bias, d_skip, norm_w, *, bsz, seq, tt=2048):
    tt = min(tt, seq)
    n = bsz * seq
    nt = seq // tt
    nck = tt // CHUNK
    gd = SSM_GROUP_DIM
    st = SSM_STATE
    x_blocks = SSM_D_INNER // st
    dtl = jnp.transpose(dt_raw.reshape(bsz, seq // CHUNK, CHUNK, SSM_HEADS), (0, 1, 3, 2))
    dtl = dtl.reshape(bsz, nt, nck, SSM_D_INNER)
    per_head = lambda v: jnp.repeat(v, SSM_HEAD_DIM).reshape(1, SSM_D_INNER)

    def cur(width, colf):
        return pl.BlockSpec((tt, width), lambda b, g, t, *_: (b * nt + t, colf(g)))

    def rowspec(rows, width, colf):
        return pl.BlockSpec((rows, width), lambda b, g, t, *_: (0, colf(g)))

    xcol = lambda g: g
    bcol = lambda g: x_blocks + g
    ccol = lambda g: x_blocks + SSM_GROUPS + g
    return pl.pallas_call(
        functools.partial(_ssd_kernel, tt=tt),
        grid=(bsz, SSM_GROUPS, nt),
        in_specs=[cur(gd, xcol), cur(st, bcol), cur(st, ccol), cur(gd, xcol),
                  pl.BlockSpec((None, None, nck, gd), lambda b, g, t: (b, t, 0, g)),
                  rowspec(1, gd, xcol), rowspec(1, gd, xcol), rowspec(1, gd, xcol), rowspec(1, gd, xcol)],
        out_specs=cur(gd, xcol),
        out_shape=jax.ShapeDtypeStruct((n, SSM_D_INNER), BF16),
        scratch_shapes=[pltpu.VMEM((SSM_STATE, gd), F32)],
        compiler_params=_cparams(("parallel", "parallel", "arbitrary")),
        name="ssd_mixer",
    )(xbc, xbc, xbc, z_b, dtl, per_head(a_log), per_head(dt_bias), per_head(d_skip),
      norm_w.reshape(1, SSM_D_INNER))


def _merge_kernel(h_ref, oa_ref, ob_ref, ga_ref, gb_ref, wa_ref, wb_ref, wo_ref, nw_ref, hn_ref, u_ref):
    half = h_ref.shape[0] // 2
    rows = lambda r: slice(r * half, (r + 1) * half)

    def produce(r):
        return (jnp.dot(oa_ref[rows(r), :], wa_ref[...], preferred_element_type=F32),
                jnp.dot(ob_ref[rows(r), :], wb_ref[...], preferred_element_type=F32))

    def consume(r, ab):
        mixed = ga_ref[rows(r), :].astype(F32) * ab[0] + gb_ref[rows(r), :].astype(F32) * ab[1]
        hn = h_ref[rows(r), :] + jnp.dot(mixed.astype(BF16), wo_ref[...], preferred_element_type=F32)
        hn_ref[rows(r), :] = hn
        y = hn * lax.rsqrt(jnp.mean(hn * hn, axis=-1, keepdims=True) + NORM_EPS) * nw_ref[...]
        u_ref[rows(r), :] = _pack_pairs(y) if u_ref.dtype == jnp.uint32 else y.astype(u_ref.dtype)

    _pipelined(2, produce, consume)


def merge_out(h, oa, ob, gates, wa, wb, wo, next_norm_w, packed_u, tm=512):
    n, d = h.shape
    u_sds = jax.ShapeDtypeStruct((n, d // 2), jnp.uint32) if packed_u else jax.ShapeDtypeStruct((n, d), BF16)
    tm = min(tm, n)
    row = lambda width: pl.BlockSpec((tm, width), lambda i: (i, 0))
    full = lambda r, c: pl.BlockSpec((r, c), lambda i: (0, 0))
    return pl.pallas_call(
        _merge_kernel,
        grid=(n // tm,),
        in_specs=[row(d), row(GDN_V_DIM), row(SSM_D_INNER), row(d), pl.BlockSpec((tm, d), lambda i: (i, 1)),
                  full(GDN_V_DIM, d), full(SSM_D_INNER, d), full(d, d), full(1, d)],
        out_specs=[row(d), row(u_sds.shape[1])],
        out_shape=[jax.ShapeDtypeStruct((n, d), F32), u_sds],
        compiler_params=_cparams(("parallel",)),
        name="merge_out",
    )(h, oa, ob, gates, gates, wa, wb, wo, next_norm_w.reshape(1, d))


def _ffn_kernel(h_ref, u_ref, wg_ref, wu_ref, wd_ref, nw_ref, hn_ref, un_ref, acc_ref, *, tf):
    u = u_ref[...]
    cols = lambda f: slice(f * tf, (f + 1) * tf)
    acc_ref[...] = h_ref[...]

    def produce(f):
        return (jnp.dot(u, wg_ref[:, cols(f)], preferred_element_type=F32),
                jnp.dot(u, wu_ref[:, cols(f)], preferred_element_type=F32))

    def consume(f, gu):
        hid = (_silu(gu[0]) * gu[1]).astype(BF16)
        acc_ref[...] += jnp.dot(hid, wd_ref[cols(f), :], preferred_element_type=F32)

    _pipelined(wg_ref.shape[1] // tf, produce, consume)
    hn = acc_ref[...]
    hn_ref[...] = hn
    y = hn * lax.rsqrt(jnp.mean(hn * hn, axis=-1, keepdims=True) + NORM_EPS)
    un_ref[...] = (y * nw_ref[...]).astype(un_ref.dtype)


def dense_ffn(h, u, wg, wu, wd, next_norm_w, tm=1024, tf=256):
    n, d = h.shape
    ff = wg.shape[1]
    tm = min(tm, n)
    assert ff % tf == 0
    row = pl.BlockSpec((tm, d), lambda i: (i, 0))
    full = lambda r, c: pl.BlockSpec((r, c), lambda i: (0, 0))
    return pl.pallas_call(
        functools.partial(_ffn_kernel, tf=tf),
        grid=(n // tm,),
        in_specs=[row, row, full(d, ff), full(d, ff), full(ff, d), full(1, d)],
        out_specs=[row, row],
        out_shape=[jax.ShapeDtypeStruct((n, d), F32), jax.ShapeDtypeStruct((n, d), BF16)],
        scratch_shapes=[pltpu.VMEM((tm, d), F32)],
        compiler_params=_cparams(("parallel",)),
        name="dense_ffn",
    )(h, u, wg, wu, wd, next_norm_w.reshape(1, d))


def _router_kernel(u_ref, w_ref, o_ref):
    logits = _dot_parts(_unpack_pairs(u_ref[...]).astype(BF16), w_ref[...], 3)
    lane = lax.broadcasted_iota(jnp.int32, logits.shape, 1)
    neg = jnp.float32(-3.0e38)
    logits = jnp.where(lane < N_EXPERTS, logits, neg)
    m1 = jnp.max(logits, axis=-1, keepdims=True)
    i1 = jnp.min(jnp.where(logits == m1, lane, 2 * N_EXPERTS), axis=-1, keepdims=True)
    rest = jnp.where(lane == i1, neg, logits)
    m2 = jnp.max(rest, axis=-1, keepdims=True)
    i2 = jnp.min(jnp.where(rest == m2, lane, 2 * N_EXPERTS), axis=-1, keepdims=True)
    e2 = jnp.exp(m2 - m1)
    g1 = 1.0 / (1.0 + e2)
    g2 = e2 / (1.0 + e2)
    out = jnp.where(lane == 0, i1.astype(F32), 0.0)
    out = jnp.where(lane == 1, i2.astype(F32), out)
    out = jnp.where(lane == 2, g1, out)
    out = jnp.where(lane == 3, g2, out)
    o_ref[...] = out


def router(u, w_pad, tm=1024):
    n, half = u.shape
    tm = min(tm, n)
    return pl.pallas_call(
        _router_kernel,
        grid=(n // tm,),
        in_specs=[pl.BlockSpec((tm, half), lambda i: (i, 0)), pl.BlockSpec((2 * half, 3 * 128), lambda i: (0, 0))],
        out_specs=pl.BlockSpec((tm, 128), lambda i: (i, 0)),
        out_shape=jax.ShapeDtypeStruct((n, 128), F32),
        compiler_params=_cparams(("parallel",)),
        name="router",
    )(u, _split3(w_pad))


def sc_gather_rows(table, idx, chunk=SC_GATHER_CHUNK):
    v, d = table.shape
    b = idx.shape[0]
    nw = SC_CORES * SC_SUBCORES
    per_w = b // nw
    assert per_w * nw == b and per_w % chunk == 0 and chunk % 8 == 0 and chunk <= 128
    mesh = plsc.VectorSubcoreMesh(core_axis_name="c", subcore_axis_name="s")

    @functools.partial(
        pl.kernel, mesh=mesh,
        out_type=jax.ShapeDtypeStruct((b, d), table.dtype),
        scratch_types=[pltpu.VMEM((chunk,), jnp.int32), pltpu.VMEM((chunk, d), table.dtype),
                       pltpu.SemaphoreType.DMA],
    )
    def gather_kernel(table_hbm, idx_hbm, out_hbm, idx_v, rows_v, sem):
        wid = lax.axis_index("s") * SC_CORES + lax.axis_index("c")
        base = wid * per_w

        @pl.loop(0, per_w // chunk)
        def _(j):
            off = pl.multiple_of(base + j * chunk, 8)
            pltpu.sync_copy(idx_hbm.at[pl.ds(off, chunk)], idx_v)
            pltpu.async_copy(table_hbm.at[idx_v], rows_v, sem).wait()
            pltpu.sync_copy(rows_v, out_hbm.at[pl.ds(off, chunk)])

    return gather_kernel(table, idx)


def sc_scatter_rows(src, dest, n_out, chunk=SC_GATHER_CHUNK):
    n, d = src.shape
    nk = dest.shape[0]
    nw = SC_CORES * SC_SUBCORES
    per_w = n // nw
    assert per_w * nw == n and per_w % chunk == 0 and chunk % 8 == 0 and chunk <= 128
    mesh = plsc.VectorSubcoreMesh(core_axis_name="c", subcore_axis_name="s")

    @functools.partial(
        pl.kernel, mesh=mesh,
        out_type=jax.ShapeDtypeStruct((n_out, d), src.dtype),
        scratch_types=[pltpu.VMEM((nk, chunk), jnp.int32), pltpu.VMEM((chunk, d), src.dtype)],
    )
    def scatter_kernel(src_hbm, dest_hbm, out_hbm, idx_v, rows_v):
        wid = lax.axis_index("s") * SC_CORES + lax.axis_index("c")
        base = wid * per_w

        @pl.loop(0, per_w // chunk)
        def _(j):
            off = pl.multiple_of(base + j * chunk, 8)
            pltpu.sync_copy(src_hbm.at[pl.ds(off, chunk)], rows_v)
            for k in range(nk):
                pltpu.sync_copy(dest_hbm.at[k, pl.ds(off, chunk)], idx_v.at[k])
            for k in range(nk):
                pltpu.sync_copy(rows_v, out_hbm.at[idx_v.at[k]])

    return scatter_kernel(src, dest)


def _expert_changed(be_ref, i):
    return jnp.logical_or(i == 0, be_ref[i] != be_ref[jnp.maximum(i - 1, 0)])


def _moe_up_kernel(be_ref, nv_ref, x_ref, wg_ref, wu_ref, o_ref, wg_bf, wu_bf):
    i = pl.program_id(1)

    @pl.when(_expert_changed(be_ref, i))
    def _():
        wg_bf[...] = wg_ref[...].astype(BF16)
        wu_bf[...] = wu_ref[...].astype(BF16)

    @pl.when(nv_ref[i] > 0)
    def _():
        xp = x_ref[...]
        xp = jnp.where(lax.broadcasted_iota(jnp.int32, xp.shape, 0) < nv_ref[i], xp, jnp.uint32(0))
        x = _unpack_pairs(xp).astype(BF16)
        half = PIPE_COLS // 2
        cols = lambda c: slice(c * half, (c + 1) * half)

        def produce(c):
            return (jnp.dot(x, wg_bf[:, cols(c)], preferred_element_type=F32),
                    jnp.dot(x, wu_bf[:, cols(c)], preferred_element_type=F32))

        def consume(c, gu):
            o_ref[:, cols(c)] = (_silu(gu[0]) * gu[1]).astype(o_ref.dtype)

        _pipelined(o_ref.shape[1] // half, produce, consume)


def moe_up(block_e, block_valid, xb, wg, wu, tf=1792):
    ns = xb.shape[0]
    d, ff = wg.shape[1], wg.shape[2]
    nb = ns // MOE_BLOCK
    grid_spec = pltpu.PrefetchScalarGridSpec(
        num_scalar_prefetch=2,
        grid=(ff // tf, nb),
        in_specs=[pl.BlockSpec((MOE_BLOCK, d // 2), lambda f, i, be, nv: (i, 0)),
                  pl.BlockSpec((None, d, tf), lambda f, i, be, nv: (be[i], 0, f)),
                  pl.BlockSpec((None, d, tf), lambda f, i, be, nv: (be[i], 0, f))],
        out_specs=pl.BlockSpec((MOE_BLOCK, tf), lambda f, i, be, nv: (i, f)),
        scratch_shapes=[pltpu.VMEM((d, tf), BF16), pltpu.VMEM((d, tf), BF16)],
    )
    return pl.pallas_call(
        _moe_up_kernel,
        grid_spec=grid_spec,
        out_shape=jax.ShapeDtypeStruct((ns, ff), BF16),
        compiler_params=_cparams(("arbitrary", "arbitrary")),
        name="moe_up",
    )(block_e, block_valid, xb, wg, wu)


def _moe_down_kernel(be_ref, nv_ref, hid_ref, wd_ref, o_ref, wd_bf):
    i = pl.program_id(0)

    @pl.when(_expert_changed(be_ref, i))
    def _():
        wd_bf[...] = wd_ref[...].astype(BF16)

    @pl.when(nv_ref[i] > 0)
    def _():
        o_ref[...] = _pack_pairs(jnp.dot(hid_ref[...], wd_bf[...], preferred_element_type=F32))


def moe_down(block_e, block_valid, hid, wd):
    ns, ff = hid.shape
    d = wd.shape[2]
    nb = ns // MOE_BLOCK
    grid_spec = pltpu.PrefetchScalarGridSpec(
        num_scalar_prefetch=2,
        grid=(nb,),
        in_specs=[pl.BlockSpec((MOE_BLOCK, ff), lambda i, be, nv: (i, 0)),
                  pl.BlockSpec((None, ff, d), lambda i, be, nv: (be[i], 0, 0))],
        out_specs=pl.BlockSpec((MOE_BLOCK, d // 2), lambda i, be, nv: (i, 0)),
        scratch_shapes=[pltpu.VMEM((ff, d), BF16)],
    )
    return pl.pallas_call(
        _moe_down_kernel,
        grid_spec=grid_spec,
        out_shape=jax.ShapeDtypeStruct((ns, d // 2), jnp.uint32),
        compiler_params=_cparams(("arbitrary",)),
        name="moe_down",
    )(block_e, block_valid, hid, wd)


def _final_kernel(h_ref, y0_ref, y1_ref, r_ref, nw_ref, o_ref):
    gates = r_ref[...]
    hn = (h_ref[...] + gates[:, TOP_K:TOP_K + 1] * _unpack_pairs(y0_ref[...])
          + gates[:, TOP_K + 1:TOP_K + 2] * _unpack_pairs(y1_ref[...]))
    y = hn * lax.rsqrt(jnp.mean(hn * hn, axis=-1, keepdims=True) + NORM_EPS)
    o_ref[...] = y * nw_ref[...]


def final_combine(h, yk, r, norm_w, tm=1024):
    n, d = h.shape
    tm = min(tm, n)
    nblk = n // tm
    row = pl.BlockSpec((tm, d), lambda i: (i, 0))
    return pl.pallas_call(
        _final_kernel,
        grid=(nblk,),
        in_specs=[row, pl.BlockSpec((tm, d // 2), lambda i: (i, 0)),
                  pl.BlockSpec((tm, d // 2), lambda i: (i + nblk, 0)),
                  pl.BlockSpec((tm, 128), lambda i: (i, 0)), pl.BlockSpec((1, d), lambda i: (0, 0))],
        out_specs=row,
        out_shape=jax.ShapeDtypeStruct((n, d), F32),
        compiler_params=_cparams(("parallel",)),
        name="final_combine",
    )(h, yk, yk, r, norm_w.reshape(1, d))


def _chunk_rows(x, bsz, seq):
    hh = x.shape[1]
    return jnp.transpose(x.reshape(bsz, seq, hh), (0, 2, 1)).reshape(bsz, hh, seq // CHUNK, CHUNK)


def hybrid_mixer_layer(h, u, bsz, seq, w_in, wb, gdn_conv_w, gdn_A_log, gdn_dt_bias, gdn_norm_w, gdn_proj,
                       ssm_conv_w, ssm_conv_b, ssm_A_log, ssm_dt_bias, ssm_D, ssm_norm_w, ssm_proj, w_out,
                       next_norm_w, packed_u):
    c0 = 0
    c1 = c0 + GDN_CONV_DIM
    c2 = c1 + GDN_V_DIM
    c3 = c2 + GDN_V_HEADS
    c4 = c3 + GDN_V_HEADS
    c5 = c4 + SSM_D_INNER
    c6 = c5 + SSM_CONV_DIM
    c7 = c6 + SSM_HEADS
    c8 = c7 + D_MODEL
    qkv = proj_act(u, wb[:, c0:c1], gdn_conv_w, jnp.zeros((GDN_CONV_DIM,), F32), seq, name="proj_qkv")
    z_a = proj_act(u, wb[:, c1:c2], None, None, seq, name="proj_za")
    z_b = proj_act(u, wb[:, c4:c5], None, None, seq, name="proj_zb")
    xbc = proj_act(u, wb[:, c5:c6], ssm_conv_w, ssm_conv_b, seq, name="proj_xbc")
    gates = proj_act(u, wb[:, c7:], None, None, seq, act=_sigmoid, name="proj_gates")
    n_small = 2 * GDN_V_HEADS + SSM_HEADS
    w_small = jnp.concatenate([w_in[:, c2:c4], w_in[:, c6:c7]], axis=1)
    w_small = jnp.pad(w_small, ((0, 0), (0, 128 - n_small)))
    small = matmul(u, w_small, F32, tn=128, full_precision=True, name="proj_small")
    a4 = _chunk_rows(small[:, :GDN_V_HEADS], bsz, seq)
    b4 = _chunk_rows(small[:, GDN_V_HEADS:2 * GDN_V_HEADS], bsz, seq)
    dt_raw = small[:, 2 * GDN_V_HEADS:n_small]

    oa = gdn_mixer(qkv, z_a, a4, b4, gdn_A_log, gdn_dt_bias, gdn_norm_w, bsz=bsz, seq=seq)
    ob = ssd_mixer(xbc, z_b, dt_raw, ssm_A_log, ssm_dt_bias, ssm_D, ssm_norm_w, bsz=bsz, seq=seq)
    return merge_out(h, oa, ob, gates, gdn_proj.astype(BF16), ssm_proj.astype(BF16),
                     w_out.astype(BF16), next_norm_w, packed_u)


def moe_layer(h, u, router_w, w_gate, w_up, w_down, final_norm_w):
    n, d = h.shape
    n_assign = n * TOP_K
    r = router(u, jnp.pad(router_w, ((0, 0), (0, 128 - N_EXPERTS))))
    experts = jnp.arange(N_EXPERTS, dtype=jnp.int32)[None, :]
    onehots = [(r[:, k].astype(jnp.int32)[:, None] == experts).astype(jnp.int32) for k in range(TOP_K)]
    per_tok = sum(onehots)
    before = jnp.cumsum(per_tok, axis=0) - per_tok
    counts = jnp.sum(per_tok, axis=0)
    padded = (counts + MOE_BLOCK - 1) // MOE_BLOCK * MOE_BLOCK
    ends = jnp.cumsum(padded)
    pstart = ends - padded
    dest_km = jnp.stack([jnp.sum((before + pstart[None, :]) * oh, axis=1) for oh in onehots]).astype(jnp.int32)
    n_blocks = -(-n_assign // MOE_BLOCK) + N_EXPERTS
    n_slots = n_blocks * MOE_BLOCK
    block_start = jnp.arange(n_blocks, dtype=jnp.int32) * MOE_BLOCK
    block_e = jnp.minimum(jnp.sum(block_start[:, None] >= ends[None, :], axis=1), N_EXPERTS - 1).astype(jnp.int32)
    block_valid = jnp.clip((pstart + counts)[block_e] - block_start, 0, MOE_BLOCK).astype(jnp.int32)
    xb = sc_scatter_rows(u, dest_km, n_slots)
    hid = moe_up(block_e, block_valid, xb, w_gate, w_up)
    yb = moe_down(block_e, block_valid, hid, w_down)
    yk = sc_gather_rows(yb, dest_km.reshape(-1))
    return final_combine(h, yk, r, final_norm_w)


def kernel(x, mix_norm_w, w_in, gdn_conv_w, gdn_A_log, gdn_dt_bias, gdn_norm_w, gdn_proj, ssm_conv_w, ssm_conv_b, ssm_A_log, ssm_dt_bias, ssm_D, ssm_norm_w, ssm_proj, w_out, ffn_norm_w, dense_w_gate, dense_w_up, dense_w_down, router_w, moe_w_gate, moe_w_up, moe_w_down, final_norm_w):
    bsz, seq, d = x.shape
    assert d == D_MODEL and w_in.shape[0] == 2, "dense-FFN layer followed by a final MoE layer"
    h = x.reshape(bsz * seq, d)
    u = rmsnorm(h, mix_norm_w[0], BF16)
    w_in_bf = w_in.astype(BF16)

    def mixer(layer, h, u, packed_u):
        return hybrid_mixer_layer(
            h, u, bsz, seq, w_in[layer], w_in_bf[layer], gdn_conv_w[layer], gdn_A_log[layer], gdn_dt_bias[layer],
            gdn_norm_w[layer], gdn_proj[layer], ssm_conv_w[layer], ssm_conv_b[layer], ssm_A_log[layer],
            ssm_dt_bias[layer], ssm_D[layer], ssm_norm_w[layer], ssm_proj[layer], w_out[layer],
            ffn_norm_w[layer], packed_u)

    h, u = mixer(0, h, u, False)
    h, u = dense_ffn(h, u, dense_w_gate[0].astype(BF16), dense_w_up[0].astype(BF16),
                     dense_w_down[0].astype(BF16), mix_norm_w[1])
    h, u = mixer(1, h, u, True)
    out = moe_layer(h, u, router_w[0], moe_w_gate[0], moe_w_up[0], moe_w_down[0], final_norm_w)
    return out.reshape(bsz, seq, d)
```

```python
import functools

import jax
import jax.numpy as jnp
from jax import lax
from jax.experimental import pallas as pl
from jax.experimental.pallas import tpu as pltpu
from jax.experimental.pallas import tpu_sc as plsc

F32 = jnp.float32
BF16 = jnp.bfloat16

D_MODEL = 1024
CONV_K = 4
CHUNK = 64
GDN_QK_HEADS = 4
GDN_V_HEADS = 8
GDN_HEAD = 128
GDN_QK_DIM = GDN_QK_HEADS * GDN_HEAD
GDN_V_DIM = GDN_V_HEADS * GDN_HEAD
GDN_CONV_DIM = 2 * GDN_QK_DIM + GDN_V_DIM
SSM_D_INNER = 2048
SSM_HEAD_DIM = 64
SSM_HEADS = SSM_D_INNER // SSM_HEAD_DIM
SSM_GROUPS = 4
SSM_HPG = SSM_HEADS // SSM_GROUPS
SSM_GROUP_DIM = SSM_D_INNER // SSM_GROUPS
SSM_STATE = 128
SSM_CONV_DIM = SSM_D_INNER + 2 * SSM_GROUPS * SSM_STATE
N_EXPERTS = 8
TOP_K = 2
MOE_BLOCK = 512
LOG2E = 1.4426950408889634
NORM_EPS = 1e-6
SSM_NORM_EPS = 1e-5
CONV_HALO = 16

VMEM_LIMIT = 56 * 1024 * 1024
PIPE_COLS = 512
SC_CORES = 2
SC_SUBCORES = 16
SC_GATHER_CHUNK = 128


def _cparams(sem):
    return pltpu.CompilerParams(dimension_semantics=sem, vmem_limit_bytes=VMEM_LIMIT)


def _silu(x):
    h = 0.5 * x
    return h + h * jnp.tanh(h)


def _softplus(x):
    return jnp.maximum(x, 0.0) + jnp.log1p(jnp.exp(-jnp.abs(x)))


def _dot(a, b):
    return jnp.dot(a.astype(BF16), b.astype(BF16), preferred_element_type=F32)


def _dot_nt(a, b):
    return lax.dot_general(a.astype(BF16), b.astype(BF16), (((1,), (1,)), ((), ())),
                           preferred_element_type=F32)


def _dot_tn(a, b):
    return lax.dot_general(a.astype(BF16), b.astype(BF16), (((0,), (0,)), ((), ())),
                           preferred_element_type=F32)


def _dot_hi(a, b):
    return jnp.dot(a, b, preferred_element_type=F32, precision=lax.Precision.HIGHEST)


def _pack_pairs(x):
    half = x.shape[1] // 2
    bits = lax.bitcast_convert_type(x.astype(BF16).astype(F32), jnp.uint32)
    return (bits[:, :half] >> 16) | (bits[:, half:] & jnp.uint32(0xFFFF0000))


def _unpack_pairs(p):
    lo = lax.bitcast_convert_type(p << 16, F32)
    hi = lax.bitcast_convert_type(p & jnp.uint32(0xFFFF0000), F32)
    return jnp.concatenate([lo, hi], axis=1)


def _rmsnorm_kernel(x_ref, w_ref, o_ref):
    x = x_ref[...]
    y = x * lax.rsqrt(jnp.mean(x * x, axis=-1, keepdims=True) + NORM_EPS)
    o_ref[...] = (y * w_ref[...]).astype(o_ref.dtype)


def rmsnorm(x, w, out_dtype, tm=1024):
    n, d = x.shape
    tm = min(tm, n)
    return pl.pallas_call(
        _rmsnorm_kernel,
        grid=(n // tm,),
        in_specs=[pl.BlockSpec((tm, d), lambda i: (i, 0)), pl.BlockSpec((1, d), lambda i: (0, 0))],
        out_specs=pl.BlockSpec((tm, d), lambda i: (i, 0)),
        out_shape=jax.ShapeDtypeStruct((n, d), out_dtype),
        compiler_params=_cparams(("parallel",)),
        name="rmsnorm",
    )(x, w.reshape(1, d))


def _split3(w):
    w_hi = w.astype(BF16)
    w_mid = (w - w_hi.astype(F32)).astype(BF16)
    w_lo = (w - w_hi.astype(F32) - w_mid.astype(F32)).astype(BF16)
    return jnp.concatenate([w_hi, w_mid, w_lo], axis=1)


def _dot_parts(x, w_parts, parts):
    r = jnp.dot(x, w_parts, preferred_element_type=F32)
    tn = w_parts.shape[1] // parts
    acc = r[:, :tn]
    for p in range(1, parts):
        acc = acc + r[:, p * tn:(p + 1) * tn]
    return acc


def _matmul_kernel(x_ref, w_ref, o_ref, *, parts):
    o_ref[...] = _dot_parts(x_ref[...], w_ref[...], parts).astype(o_ref.dtype)


def matmul(x, w, out_dtype, tm=1024, tn=1024, full_precision=False, name="matmul"):
    n, k = x.shape
    m = w.shape[1]
    tm = min(tm, n)
    tn = min(tn, m)
    parts = 1
    if full_precision:
        assert m == tn
        w = _split3(w)
        parts = 3
    return pl.pallas_call(
        functools.partial(_matmul_kernel, parts=parts),
        grid=(m // tn, n // tm),
        in_specs=[pl.BlockSpec((tm, k), lambda j, i: (i, 0)),
                  pl.BlockSpec((k, parts * tn), lambda j, i: (0, j))],
        out_specs=pl.BlockSpec((tm, tn), lambda j, i: (i, j)),
        out_shape=jax.ShapeDtypeStruct((n, m), out_dtype),
        compiler_params=_cparams(("parallel", "parallel")),
        name=name,
    )(x, w)


def _pipelined(n, produce, consume):
    cur = produce(0)
    for i in range(n):
        nxt = produce(i + 1) if i + 1 < n else None
        consume(i, cur)
        cur = nxt


def _sigmoid(x):
    return 0.5 + 0.5 * jnp.tanh(0.5 * x)


def _proj_act_kernel(x_ref, w_ref, o_ref, *, act):
    x = x_ref[...]
    cols = lambda c: slice(c * PIPE_COLS, (c + 1) * PIPE_COLS)

    def consume(c, raw):
        o_ref[:, cols(c)] = act(raw).astype(o_ref.dtype)

    _pipelined(o_ref.shape[1] // PIPE_COLS,
               lambda c: jnp.dot(x, w_ref[:, cols(c)], preferred_element_type=F32), consume)


def _proj_conv_silu_kernel(x_ref, w_ref, cw_ref, cb_ref, o_ref, carry_ref, *, tiles_per_seq):
    x = x_ref[...]
    first = pl.program_id(1) % tiles_per_seq == 0
    cols = lambda c: slice(c * PIPE_COLS, (c + 1) * PIPE_COLS)

    def roll2(v):
        words = pltpu.roll(pltpu.bitcast(v, jnp.uint32), 1, 0)
        return pltpu.bitcast(words, v.dtype)

    def consume(c, raw):
        prev = jnp.where(first, 0.0, carry_ref[:, cols(c)])
        carry_ref[:, cols(c)] = raw[-CONV_HALO:]
        x = jnp.concatenate([prev, raw], axis=0)
        xb, x1b = x.astype(BF16), pltpu.roll(x, 1, 0).astype(BF16)
        w = cw_ref[:, cols(c)].astype(BF16)
        near = xb * w[3:4] + x1b * w[2:3] + cb_ref[:, cols(c)].astype(BF16)
        far = xb * w[1:2] + x1b * w[0:1]
        h = (near + roll2(far))[CONV_HALO:]
        o_ref[:, cols(c)] = (h + h * jnp.tanh(h)).astype(o_ref.dtype)

    _pipelined(o_ref.shape[1] // PIPE_COLS,
               lambda c: jnp.dot(x, w_ref[:, cols(c)], preferred_element_type=F32), consume)


def proj_act(x, w, conv_w, conv_b, seq, tm=1024, tn=None, act=_silu, name="proj_act"):
    n, k = x.shape
    m = w.shape[1]
    tm = min(tm, seq)
    tn = m if tn is None else tn
    assert seq % tm == 0 and n % seq == 0 and m % tn == 0
    xw_specs = [pl.BlockSpec((tm, k), lambda j, i: (i, 0)), pl.BlockSpec((k, tn), lambda j, i: (0, j))]
    common = dict(
        grid=(m // tn, n // tm),
        out_specs=pl.BlockSpec((tm, tn), lambda j, i: (i, j)),
        out_shape=jax.ShapeDtypeStruct((n, m), BF16),
        name=name,
    )
    if conv_w is None:
        return pl.pallas_call(functools.partial(_proj_act_kernel, act=act), in_specs=xw_specs,
                              compiler_params=_cparams(("parallel", "parallel")), **common)(x, w)
    return pl.pallas_call(
        functools.partial(_proj_conv_silu_kernel, tiles_per_seq=seq // tm),
        in_specs=xw_specs + [pl.BlockSpec((CONV_K, tn), lambda j, i: (0, j)),
                             pl.BlockSpec((1, tn), lambda j, i: (0, j))],
        scratch_shapes=[pltpu.VMEM((CONV_HALO, tn), F32)],
        compiler_params=_cparams(("parallel", "arbitrary")),
        **common,
    )(x, w, 0.5 * conv_w, 0.5 * conv_b.reshape(1, m))


def _col_from_row(row, eye):
    return jnp.sum(jnp.where(eye, row, 0.0), axis=-1, keepdims=True)


def _gdn_kernel(alog_ref, dtb_ref,
                q_ref, k_ref, v_ref, z_ref, a_ref, b_ref, nw_ref,
                o_ref,
                s_ref, *, tt, hpb):
    hblk = pl.program_id(1)
    nck = tt // CHUNK
    rep = GDN_V_HEADS // GDN_QK_HEADS
    dh = GDN_HEAD

    @pl.when(pl.program_id(2) == 0)
    def _():
        s_ref[...] = jnp.zeros_like(s_ref)

    ri = lax.broadcasted_iota(jnp.int32, (CHUNK, CHUNK), 0)
    ci = lax.broadcasted_iota(jnp.int32, (CHUNK, CHUNK), 1)
    upper = (ri <= ci).astype(F32)
    eye = ri == ci
    causal = ri >= ci
    strict = ri > ci
    chunks = [slice(c * CHUNK, (c + 1) * CHUNK) for c in range(nck)]
    nsteps = CHUNK.bit_length() - 1

    heads = range(hpb)
    g_rows, beta_rows, gc_rows = [], [], []
    for hh in heads:
        head = hblk * hpb + hh
        neg_a = -LOG2E * jnp.exp(jnp.full((1, CHUNK), alog_ref[head], F32))
        g_rows.append(neg_a * _softplus(a_ref[hh] + dtb_ref[head]))
        beta_rows.append(jax.nn.sigmoid(b_ref[hh]))
        gc_rows.append(_dot_hi(g_rows[hh], upper))

    ps, xs, qkk, qes, egl = {}, {}, {}, {}, {}

    def local_prep(c):
        sl = chunks[c]
        qs, ks = [], []
        for j in range(hpb // rep):
            q = q_ref[sl, j * dh:(j + 1) * dh].astype(F32)
            k = k_ref[sl, j * dh:(j + 1) * dh].astype(F32)
            qs.append(q * (lax.rsqrt(jnp.sum(q * q, axis=-1, keepdims=True) + 1e-6) * (dh ** -0.5)))
            ks.append(k * lax.rsqrt(jnp.sum(k * k, axis=-1, keepdims=True) + 1e-6))
        kq = [_dot_nt(jnp.concatenate([ks[j], qs[j]], axis=0), ks[j]) for j in range(hpb // rep)]
        for hh in heads:
            qc, kc = qs[hh // rep], ks[hh // rep]
            vc = v_ref[sl, hh * dh:(hh + 1) * dh].astype(F32)
            gc_row = gc_rows[hh][c:c + 1, :]
            gc_col = _col_from_row(gc_row, eye)
            beta_col = _col_from_row(beta_rows[hh][c:c + 1, :], eye)
            g_last = jnp.sum(g_rows[hh][c:c + 1, :], axis=-1, keepdims=True)
            decay = jnp.where(causal, jnp.exp2(gc_col - gc_row), 0.0)
            eg_col = jnp.exp2(gc_col)
            kk, qk = kq[hh // rep][:CHUNK], kq[hh // rep][CHUNK:]
            ps[hh, c] = jnp.where(strict, kk * decay, 0.0) * beta_col
            xs[hh, c] = jnp.concatenate([vc * beta_col, kc * (beta_col * eg_col)], axis=1)
            k_dec = kc * jnp.exp2(g_last - gc_col)
            qkk[hh, c] = jnp.concatenate([qk * decay, k_dec.T], axis=0)
            qes[hh, c] = qc * eg_col
            egl[hh, c] = jnp.exp2(g_last)

    def solve_step(c, step):
        for hh in heads:
            u = (hh, c)
            if step + 1 < nsteps:
                r = _dot(ps[u], jnp.concatenate([xs[u], ps[u]], axis=1))
                ps[u] = r[:, 2 * dh:]
                r = r[:, :2 * dh]
            else:
                r = _dot(ps[u], xs[u])
            xs[u] = xs[u] - r if step == 0 else xs[u] + r

    s = [s_ref[hh] for hh in heads]
    v_new, o_part = {}, {}

    def rec_a(c):
        for hh in heads:
            x = xs[hh, c]
            wq_s = _dot(jnp.concatenate([x[:, dh:], qes[hh, c]], axis=0), s[hh])
            v_new[hh] = x[:, :dh] - wq_s[:CHUNK]
            o_part[hh] = wq_s[CHUNK:]

    def rec_b(c):
        for hh in heads:
            r = _dot(qkk[hh, c], v_new[hh])
            s[hh] = s[hh] * egl[hh, c] + r[CHUNK:]
            o = o_part[hh] + r[:CHUNK]
            y = o * lax.rsqrt(jnp.mean(o * o, axis=-1, keepdims=True) + NORM_EPS) * nw_ref[...]
            cols = slice(hh * dh, (hh + 1) * dh)
            o_ref[chunks[c], cols] = (y * z_ref[chunks[c], cols].astype(F32)).astype(o_ref.dtype)

    half = nsteps // 2
    local_prep(0)
    for step in range(nsteps):
        solve_step(0, step)
    for c in range(nck):
        nxt = c + 1 < nck
        if nxt:
            local_prep(c + 1)
        rec_a(c)
        if nxt:
            for step in range(half):
                solve_step(c + 1, step)
        rec_b(c)
        if nxt:
            for step in range(half, nsteps):
                solve_step(c + 1, step)
    for hh in range(hpb):
        s_ref[hh] = s[hh]


def gdn_mixer(qkv, z_a, a4, b4, a_log, dt_bias, norm_w, *, bsz, seq, tt=512, hpb=GDN_V_HEADS):
    tt = min(tt, seq)
    n = bsz * seq
    nt = seq // tt
    rep = GDN_V_HEADS // GDN_QK_HEADS
    nhb = GDN_V_HEADS // hpb
    qw = hpb // rep * GDN_HEAD
    vw = hpb * GDN_HEAD
    q0, k0, v0 = 0, GDN_QK_DIM // qw, 2 * GDN_QK_DIM // vw

    def cur(width, c0):
        return pl.BlockSpec((tt, width), lambda b, h, t, *_: (b * nt + t, c0 + h))

    small = pl.BlockSpec((None, hpb, None, tt // CHUNK, CHUNK), lambda b, h, t, *_: (b, h, t, 0, 0))
    a4 = a4.reshape(bsz, GDN_V_HEADS, nt, tt // CHUNK, CHUNK)
    b4 = b4.reshape(bsz, GDN_V_HEADS, nt, tt // CHUNK, CHUNK)
    grid_spec = pltpu.PrefetchScalarGridSpec(
        num_scalar_prefetch=2,
        grid=(bsz, nhb, nt),
        in_specs=[cur(qw, q0), cur(qw, k0), cur(vw, v0), cur(vw, 0), small, small,
                  pl.BlockSpec((1, GDN_HEAD), lambda b, h, t, *_: (0, 0))],
        out_specs=cur(vw, 0),
        scratch_shapes=[pltpu.VMEM((hpb, GDN_HEAD, GDN_HEAD), F32)],
    )
    return pl.pallas_call(
        functools.partial(_gdn_kernel, tt=tt, hpb=hpb),
        grid_spec=grid_spec,
        out_shape=jax.ShapeDtypeStruct((n, GDN_V_DIM), BF16),
        compiler_params=_cparams(("parallel", "parallel", "arbitrary")),
        name="gdn_mixer",
    )(a_log, dt_bias, qkv, qkv, qkv, z_a, a4, b4, norm_w.reshape(1, GDN_HEAD))


def _ssd_kernel(x_ref, b_ref, c_ref, z_ref, dt_ref, alog_ref, dtb_ref, dskip_ref, nw_ref,
                o_ref,
                h_ref, *, tt):
    nck = tt // CHUNK
    pw = 2 * SSM_HEAD_DIM
    npair = SSM_GROUP_DIM // pw

    @pl.when(pl.program_id(2) == 0)
    def _():
        h_ref[...] = jnp.zeros_like(h_ref)

    dt_rows = _softplus(dt_ref[...] + dtb_ref[...])
    adt_rows = (-LOG2E * jnp.exp(alog_ref[...])) * dt_rows
    r2 = lax.broadcasted_iota(jnp.int32, (pw, pw), 0)
    c2 = lax.broadcasted_iota(jnp.int32, (pw, pw), 1)
    same_head = (r2 // CHUNK) == (c2 // CHUNK)
    cum_tot = jnp.concatenate([(same_head & (r2 <= c2)).astype(F32), same_head.astype(F32)], axis=1)
    adt_pc = jnp.concatenate([adt_rows[:, p * pw:(p + 1) * pw] for p in range(npair)], axis=0)
    ct = _dot_hi(adt_pc, cum_tot)
    acs_pc, tot_pc = ct[:, :pw], ct[:, pw:]

    li = lax.broadcasted_iota(jnp.int32, (CHUNK, pw), 0)
    ji = lax.broadcasted_iota(jnp.int32, (CHUNK, pw), 1)
    lo_half = ji < CHUNK
    pick_a = ji == li
    pick_b = ji == li + CHUNK
    causal2 = li >= jnp.where(lo_half, ji, ji - CHUNK)
    rb = lax.broadcasted_iota(jnp.int32, (pw, pw), 0)
    cb_ = lax.broadcasted_iota(jnp.int32, (pw, pw), 1)
    blockdiag = (rb < CHUNK) == (cb_ < CHUNK)

    def pair_col(row):
        a = jnp.sum(jnp.where(pick_a, row, 0.0), axis=-1, keepdims=True)
        b = jnp.sum(jnp.where(pick_b, row, 0.0), axis=-1, keepdims=True)
        return jnp.where(lo_half, a, b)

    hstate = h_ref[...]
    for c in range(nck):
        sl = slice(c * CHUNK, (c + 1) * CHUNK)
        xs = x_ref[sl, :].astype(F32)
        cmc, bmc = c_ref[sl, :], b_ref[sl, :]
        cb2 = _dot_nt(cmc, jnp.concatenate([bmc, bmc], axis=0))
        yd_parts, eacs_parts, xdec_parts, eal_parts = [], [], [], []
        for p in range(npair):
            row = p * nck + c
            acs_row = acs_pc[row:row + 1, :]
            tot_row = tot_pc[row:row + 1, :]
            acs_col = pair_col(acs_row)
            dt_col = pair_col(dt_rows[c:c + 1, p * pw:(p + 1) * pw])
            lmat = jnp.where(causal2, jnp.exp2(acs_col - acs_row), 0.0)
            xdt = xs[:, p * pw:(p + 1) * pw] * dt_col
            xdt2 = jnp.where(blockdiag, jnp.concatenate([xdt, xdt], axis=0), 0.0)
            yd_parts.append(_dot(cb2 * lmat, xdt2))
            eacs_parts.append(jnp.exp2(acs_col))
            xdec_parts.append(xdt * jnp.exp2(tot_row - acs_col))
            eal_parts.append(jnp.exp2(tot_row))
        upd = _dot_tn(bmc, jnp.concatenate(xdec_parts, axis=1))
        y_off = _dot(cmc, hstate)
        y = jnp.concatenate(yd_parts, axis=1) + y_off * jnp.concatenate(eacs_parts, axis=1) + xs * dskip_ref[...]
        y = y * z_ref[sl, :].astype(F32)
        y = y * lax.rsqrt(jnp.mean(y * y, axis=-1, keepdims=True) + SSM_NORM_EPS) * nw_ref[...]
        o_ref[sl, :] = y.astype(o_ref.dtype)
        hstate = hstate * jnp.concatenate(eal_parts, axis=1) + upd
    h_ref[...] = hstate


def ssd_mixer(xbc, z_b, dt_raw, a_log, dt_bias, d_skip, norm_w, *, bsz, seq, tt=2048):
    tt = min(tt, seq)
    n = bsz * seq
    nt = seq // tt
    nck = tt // CHUNK
    gd = SSM_GROUP_DIM
    st = SSM_STATE
    x_blocks = SSM_D_INNER // st
    dtl = jnp.transpose(dt_raw.reshape(bsz, seq // CHUNK, CHUNK, SSM_HEADS), (0, 1, 3, 2))
    dtl = dtl.reshape(bsz, nt, nck, SSM_D_INNER)
    per_head = lambda v: jnp.repeat(v, SSM_HEAD_DIM).reshape(1, SSM_D_INNER)

    def cur(width, colf):
        return pl.BlockSpec((tt, width), lambda b, g, t, *_: (b * nt + t, colf(g)))

    def rowspec(rows, width, colf):
        return pl.BlockSpec((rows, width), lambda b, g, t, *_: (0, colf(g)))

    xcol = lambda g: g
    bcol = lambda g: x_blocks + g
    ccol = lambda g: x_blocks + SSM_GROUPS + g
    return pl.pallas_call(
        functools.partial(_ssd_kernel, tt=tt),
        grid=(bsz, SSM_GROUPS, nt),
        in_specs=[cur(gd, xcol), cur(st, bcol), cur(st, ccol), cur(gd, xcol),
                  pl.BlockSpec((None, None, nck, gd), lambda b, g, t: (b, t, 0, g)),
                  rowspec(1, gd, xcol), rowspec(1, gd, xcol), rowspec(1, gd, xcol), rowspec(1, gd, xcol)],
        out_specs=cur(gd, xcol),
        out_shape=jax.ShapeDtypeStruct((n, SSM_D_INNER), BF16),
        scratch_shapes=[pltpu.VMEM((SSM_STATE, gd), F32)],
        compiler_params=_cparams(("parallel", "parallel", "arbitrary")),
        name="ssd_mixer",
    )(xbc, xbc, xbc, z_b, dtl, per_head(a_log), per_head(dt_bias), per_head(d_skip),
      norm_w.reshape(1, SSM_D_INNER))


def _top2_route(logits):
    lane = lax.broadcasted_iota(jnp.int32, logits.shape, 1)
    neg = jnp.float32(-3.0e38)
    logits = jnp.where(lane < N_EXPERTS, logits, neg)
    m1 = jnp.max(logits, axis=-1, keepdims=True)
    i1 = jnp.min(jnp.where(logits == m1, lane, 2 * N_EXPERTS), axis=-1, keepdims=True)
    rest = jnp.where(lane == i1, neg, logits)
    m2 = jnp.max(rest, axis=-1, keepdims=True)
    i2 = jnp.min(jnp.where(rest == m2, lane, 2 * N_EXPERTS), axis=-1, keepdims=True)
    e2 = jnp.exp(m2 - m1)
    g1 = 1.0 / (1.0 + e2)
    g2 = e2 / (1.0 + e2)
    out = jnp.where(lane == 0, i1.astype(F32), 0.0)
    out = jnp.where(lane == 1, i2.astype(F32), out)
    out = jnp.where(lane == 2, g1, out)
    return jnp.where(lane == 3, g2, out)


def _merge_kernel(h_ref, oa_ref, ob_ref, ga_ref, gb_ref, wa_ref, wb_ref, wo_ref, nw_ref, *rest):
    rw_ref, hn_ref, u_ref, r_ref = rest if len(rest) == 4 else (None,) + rest + (None,)
    half = h_ref.shape[0] // 2
    rows = lambda r: slice(r * half, (r + 1) * half)

    def produce(r):
        return (jnp.dot(oa_ref[rows(r), :], wa_ref[...], preferred_element_type=F32),
                jnp.dot(ob_ref[rows(r), :], wb_ref[...], preferred_element_type=F32))

    def consume(r, ab):
        mixed = ga_ref[rows(r), :].astype(F32) * ab[0] + gb_ref[rows(r), :].astype(F32) * ab[1]
        hn = h_ref[rows(r), :] + jnp.dot(mixed.astype(BF16), wo_ref[...], preferred_element_type=F32)
        hn_ref[rows(r), :] = hn
        y = hn * lax.rsqrt(jnp.mean(hn * hn, axis=-1, keepdims=True) + NORM_EPS) * nw_ref[...]
        u_ref[rows(r), :] = _pack_pairs(y) if u_ref.dtype == jnp.uint32 else y.astype(u_ref.dtype)
        if rw_ref is not None:
            r_ref[rows(r), :] = _top2_route(_dot_parts(y.astype(BF16), rw_ref[...], 3))

    _pipelined(2, produce, consume)


def merge_out(h, oa, ob, gates, wa, wb, wo, next_norm_w, packed_u, router_w=None, tm=512):
    n, d = h.shape
    u_sds = jax.ShapeDtypeStruct((n, d // 2), jnp.uint32) if packed_u else jax.ShapeDtypeStruct((n, d), BF16)
    tm = min(tm, n)
    row = lambda width: pl.BlockSpec((tm, width), lambda i: (i, 0))
    full = lambda r, c: pl.BlockSpec((r, c), lambda i: (0, 0))
    in_specs = [row(d), row(GDN_V_DIM), row(SSM_D_INNER), row(d), pl.BlockSpec((tm, d), lambda i: (i, 1)),
                full(GDN_V_DIM, d), full(SSM_D_INNER, d), full(d, d), full(1, d)]
    args = [h, oa, ob, gates, gates, wa, wb, wo, next_norm_w.reshape(1, d)]
    out_specs = [row(d), row(u_sds.shape[1])]
    out_shape = [jax.ShapeDtypeStruct((n, d), F32), u_sds]
    if router_w is not None:
        in_specs.append(full(d, 3 * 128))
        args.append(_split3(jnp.pad(router_w, ((0, 0), (0, 128 - N_EXPERTS)))))
        out_specs.append(row(128))
        out_shape.append(jax.ShapeDtypeStruct((n, 128), F32))
    return pl.pallas_call(
        _merge_kernel,
        grid=(n // tm,),
        in_specs=in_specs,
        out_specs=out_specs,
        out_shape=out_shape,
        compiler_params=_cparams(("parallel",)),
        name="merge_out",
    )(*args)


def _ffn_kernel(h_ref, u_ref, wg_ref, wu_ref, wd_ref, nw_ref, hn_ref, un_ref, acc_ref, *, tf):
    u = u_ref[...]
    cols = lambda f: slice(f * tf, (f + 1) * tf)
    acc_ref[...] = h_ref[...]

    def produce(f):
        return (jnp.dot(u, wg_ref[:, cols(f)], preferred_element_type=F32),
                jnp.dot(u, wu_ref[:, cols(f)], preferred_element_type=F32))

    def consume(f, gu):
        hid = (_silu(gu[0]) * gu[1]).astype(BF16)
        acc_ref[...] += jnp.dot(hid, wd_ref[cols(f), :], preferred_element_type=F32)

    _pipelined(wg_ref.shape[1] // tf, produce, consume)
    hn = acc_ref[...]
    hn_ref[...] = hn
    y = hn * lax.rsqrt(jnp.mean(hn * hn, axis=-1, keepdims=True) + NORM_EPS)
    un_ref[...] = (y * nw_ref[...]).astype(un_ref.dtype)


def dense_ffn(h, u, wg, wu, wd, next_norm_w, tm=1024, tf=256):
    n, d = h.shape
    ff = wg.shape[1]
    tm = min(tm, n)
    assert ff % tf == 0
    row = pl.BlockSpec((tm, d), lambda i: (i, 0))
    full = lambda r, c: pl.BlockSpec((r, c), lambda i: (0, 0))
    return pl.pallas_call(
        functools.partial(_ffn_kernel, tf=tf),
        grid=(n // tm,),
        in_specs=[row, row, full(d, ff), full(d, ff), full(ff, d), full(1, d)],
        out_specs=[row, row],
        out_shape=[jax.ShapeDtypeStruct((n, d), F32), jax.ShapeDtypeStruct((n, d), BF16)],
        scratch_shapes=[pltpu.VMEM((tm, d), F32)],
        compiler_params=_cparams(("parallel",)),
        name="dense_ffn",
    )(h, u, wg, wu, wd, next_norm_w.reshape(1, d))


def sc_gather_rows(table, idx, chunk=SC_GATHER_CHUNK):
    v, d = table.shape
    b = idx.shape[0]
    nw = SC_CORES * SC_SUBCORES
    per_w = b // nw
    assert per_w * nw == b and per_w % chunk == 0 and chunk % 8 == 0 and chunk <= 128
    mesh = plsc.VectorSubcoreMesh(core_axis_name="c", subcore_axis_name="s")

    @functools.partial(
        pl.kernel, mesh=mesh,
        out_type=jax.ShapeDtypeStruct((b, d), table.dtype),
        scratch_types=[pltpu.VMEM((chunk,), jnp.int32), pltpu.VMEM((chunk, d), table.dtype),
                       pltpu.SemaphoreType.DMA],
    )
    def gather_kernel(table_hbm, idx_hbm, out_hbm, idx_v, rows_v, sem):
        wid = lax.axis_index("s") * SC_CORES + lax.axis_index("c")
        base = wid * per_w

        @pl.loop(0, per_w // chunk)
        def _(j):
            off = pl.multiple_of(base + j * chunk, 8)
            pltpu.sync_copy(idx_hbm.at[pl.ds(off, chunk)], idx_v)
            pltpu.async_copy(table_hbm.at[idx_v], rows_v, sem).wait()
            pltpu.sync_copy(rows_v, out_hbm.at[pl.ds(off, chunk)])

    return gather_kernel(table, idx)


def sc_scatter_rows(src, dest, n_out, chunk=SC_GATHER_CHUNK):
    n, d = src.shape
    nk = dest.shape[0]
    nw = SC_CORES * SC_SUBCORES
    per_w = n // nw
    assert per_w * nw == n and per_w % chunk == 0 and chunk % 8 == 0 and chunk <= 128
    mesh = plsc.VectorSubcoreMesh(core_axis_name="c", subcore_axis_name="s")

    @functools.partial(
        pl.kernel, mesh=mesh,
        out_type=jax.ShapeDtypeStruct((n_out, d), src.dtype),
        scratch_types=[pltpu.VMEM((nk, chunk), jnp.int32), pltpu.VMEM((chunk, d), src.dtype)],
    )
    def scatter_kernel(src_hbm, dest_hbm, out_hbm, idx_v, rows_v):
        wid = lax.axis_index("s") * SC_CORES + lax.axis_index("c")
        base = wid * per_w

        @pl.loop(0, per_w // chunk)
        def _(j):
            off = pl.multiple_of(base + j * chunk, 8)
            pltpu.sync_copy(src_hbm.at[pl.ds(off, chunk)], rows_v)
            for k in range(nk):
                pltpu.sync_copy(dest_hbm.at[k, pl.ds(off, chunk)], idx_v.at[k])
            for k in range(nk):
                pltpu.sync_copy(rows_v, out_hbm.at[idx_v.at[k]])

    return scatter_kernel(src, dest)


def _expert_changed(be_ref, i):
    return jnp.logical_or(i == 0, be_ref[i] != be_ref[jnp.maximum(i - 1, 0)])


def _moe_up_kernel(be_ref, nv_ref, x_ref, wg_ref, wu_ref, o_ref, wg_bf, wu_bf):
    i = pl.program_id(1)

    @pl.when(_expert_changed(be_ref, i))
    def _():
        wg_bf[...] = wg_ref[...].astype(BF16)
        wu_bf[...] = wu_ref[...].astype(BF16)

    @pl.when(nv_ref[i] > 0)
    def _():
        xp = x_ref[...]
        xp = jnp.where(lax.broadcasted_iota(jnp.int32, xp.shape, 0) < nv_ref[i], xp, jnp.uint32(0))
        x = _unpack_pairs(xp).astype(BF16)
        half = PIPE_COLS // 2
        cols = lambda c: slice(c * half, (c + 1) * half)

        def produce(c):
            return (jnp.dot(x, wg_bf[:, cols(c)], preferred_element_type=F32),
                    jnp.dot(x, wu_bf[:, cols(c)], preferred_element_type=F32))

        def consume(c, gu):
            o_ref[:, cols(c)] = (_silu(gu[0]) * gu[1]).astype(o_ref.dtype)

        _pipelined(o_ref.shape[1] // half, produce, consume)


def moe_up(block_e, block_valid, xb, wg, wu, tf=1792):
    ns = xb.shape[0]
    d, ff = wg.shape[1], wg.shape[2]
    nb = ns // MOE_BLOCK
    grid_spec = pltpu.PrefetchScalarGridSpec(
        num_scalar_prefetch=2,
        grid=(ff // tf, nb),
        in_specs=[pl.BlockSpec((MOE_BLOCK, d // 2), lambda f, i, be, nv: (i, 0)),
                  pl.BlockSpec((None, d, tf), lambda f, i, be, nv: (be[i], 0, f)),
                  pl.BlockSpec((None, d, tf), lambda f, i, be, nv: (be[i], 0, f))],
        out_specs=pl.BlockSpec((MOE_BLOCK, tf), lambda f, i, be, nv: (i, f)),
        scratch_shapes=[pltpu.VMEM((d, tf), BF16), pltpu.VMEM((d, tf), BF16)],
    )
    return pl.pallas_call(
        _moe_up_kernel,
        grid_spec=grid_spec,
        out_shape=jax.ShapeDtypeStruct((ns, ff), BF16),
        compiler_params=_cparams(("arbitrary", "arbitrary")),
        name="moe_up",
    )(block_e, block_valid, xb, wg, wu)


def _moe_down_kernel(be_ref, nv_ref, hid_ref, wd_ref, o_ref, wd_bf):
    i = pl.program_id(0)

    @pl.when(_expert_changed(be_ref, i))
    def _():
        wd_bf[...] = wd_ref[...].astype(BF16)

    @pl.when(nv_ref[i] > 0)
    def _():
        o_ref[...] = _pack_pairs(jnp.dot(hid_ref[...], wd_bf[...], preferred_element_type=F32))


def moe_down(block_e, block_valid, hid, wd):
    ns, ff = hid.shape
    d = wd.shape[2]
    nb = ns // MOE_BLOCK
    grid_spec = pltpu.PrefetchScalarGridSpec(
        num_scalar_prefetch=2,
        grid=(nb,),
        in_specs=[pl.BlockSpec((MOE_BLOCK, ff), lambda i, be, nv: (i, 0)),
                  pl.BlockSpec((None, ff, d), lambda i, be, nv: (be[i], 0, 0))],
        out_specs=pl.BlockSpec((MOE_BLOCK, d // 2), lambda i, be, nv: (i, 0)),
        scratch_shapes=[pltpu.VMEM((ff, d), BF16)],
    )
    return pl.pallas_call(
        _moe_down_kernel,
        grid_spec=grid_spec,
        out_shape=jax.ShapeDtypeStruct((ns, d // 2), jnp.uint32),
        compiler_params=_cparams(("arbitrary",)),
        name="moe_down",
    )(block_e, block_valid, hid, wd)


def _final_kernel(h_ref, y0_ref, y1_ref, r_ref, nw_ref, o_ref):
    gates = r_ref[...]
    hn = (h_ref[...] + gates[:, TOP_K:TOP_K + 1] * _unpack_pairs(y0_ref[...])
          + gates[:, TOP_K + 1:TOP_K + 2] * _unpack_pairs(y1_ref[...]))
    y = hn * lax.rsqrt(jnp.mean(hn * hn, axis=-1, keepdims=True) + NORM_EPS)
    o_ref[...] = y * nw_ref[...]


def final_combine(h, yk, r, norm_w, tm=1024):
    n, d = h.shape
    tm = min(tm, n)
    nblk = n // tm
    row = pl.BlockSpec((tm, d), lambda i: (i, 0))
    return pl.pallas_call(
        _final_kernel,
        grid=(nblk,),
        in_specs=[row, pl.BlockSpec((tm, d // 2), lambda i: (i, 0)),
                  pl.BlockSpec((tm, d // 2), lambda i: (i + nblk, 0)),
                  pl.BlockSpec((tm, 128), lambda i: (i, 0)), pl.BlockSpec((1, d), lambda i: (0, 0))],
        out_specs=row,
        out_shape=jax.ShapeDtypeStruct((n, d), F32),
        compiler_params=_cparams(("parallel",)),
        name="final_combine",
    )(h, yk, yk, r, norm_w.reshape(1, d))


def _chunk_rows(x, bsz, seq):
    hh = x.shape[1]
    return jnp.transpose(x.reshape(bsz, seq, hh), (0, 2, 1)).reshape(bsz, hh, seq // CHUNK, CHUNK)


def hybrid_mixer_layer(h, u, bsz, seq, w_in, wb, gdn_conv_w, gdn_A_log, gdn_dt_bias, gdn_norm_w, gdn_proj,
                       ssm_conv_w, ssm_conv_b, ssm_A_log, ssm_dt_bias, ssm_D, ssm_norm_w, ssm_proj, w_out,
                       next_norm_w, packed_u, router_w=None):
    c0 = 0
    c1 = c0 + GDN_CONV_DIM
    c2 = c1 + GDN_V_DIM
    c3 = c2 + GDN_V_HEADS
    c4 = c3 + GDN_V_HEADS
    c5 = c4 + SSM_D_INNER
    c6 = c5 + SSM_CONV_DIM
    c7 = c6 + SSM_HEADS
    c8 = c7 + D_MODEL
    qkv = proj_act(u, wb[:, c0:c1], gdn_conv_w, jnp.zeros((GDN_CONV_DIM,), F32), seq, name="proj_qkv")
    z_a = proj_act(u, wb[:, c1:c2], None, None, seq, name="proj_za")
    z_b = proj_act(u, wb[:, c4:c5], None, None, seq, name="proj_zb")
    xbc = proj_act(u, wb[:, c5:c6], ssm_conv_w, ssm_conv_b, seq, name="proj_xbc")
    gates = proj_act(u, wb[:, c7:], None, None, seq, act=_sigmoid, name="proj_gates")
    n_small = 2 * GDN_V_HEADS + SSM_HEADS
    w_small = jnp.concatenate([w_in[:, c2:c4], w_in[:, c6:c7]], axis=1)
    w_small = jnp.pad(w_small, ((0, 0), (0, 128 - n_small)))
    small = matmul(u, w_small, F32, tn=128, full_precision=True, name="proj_small")
    a4 = _chunk_rows(small[:, :GDN_V_HEADS], bsz, seq)
    b4 = _chunk_rows(small[:, GDN_V_HEADS:2 * GDN_V_HEADS], bsz, seq)
    dt_raw = small[:, 2 * GDN_V_HEADS:n_small]

    oa = gdn_mixer(qkv, z_a, a4, b4, gdn_A_log, gdn_dt_bias, gdn_norm_w, bsz=bsz, seq=seq)
    ob = ssd_mixer(xbc, z_b, dt_raw, ssm_A_log, ssm_dt_bias, ssm_D, ssm_norm_w, bsz=bsz, seq=seq)
    return merge_out(h, oa, ob, gates, gdn_proj.astype(BF16), ssm_proj.astype(BF16),
                     w_out.astype(BF16), next_norm_w, packed_u, router_w)


def moe_layer(h, u, r, w_gate, w_up, w_down, final_norm_w):
    n, d = h.shape
    n_assign = n * TOP_K
    experts = jnp.arange(N_EXPERTS, dtype=jnp.int32)[None, :]
    onehots = [(r[:, k].astype(jnp.int32)[:, None] == experts).astype(jnp.int32) for k in range(TOP_K)]
    per_tok = sum(onehots)
    before = jnp.cumsum(per_tok, axis=0) - per_tok
    counts = jnp.sum(per_tok, axis=0)
    padded = (counts + MOE_BLOCK - 1) // MOE_BLOCK * MOE_BLOCK
    ends = jnp.cumsum(padded)
    pstart = ends - padded
    dest_km = jnp.stack([jnp.sum((before + pstart[None, :]) * oh, axis=1) for oh in onehots]).astype(jnp.int32)
    n_blocks = -(-n_assign // MOE_BLOCK) + N_EXPERTS
    n_slots = n_blocks * MOE_BLOCK
    block_start = jnp.arange(n_blocks, dtype=jnp.int32) * MOE_BLOCK
    block_e = jnp.minimum(jnp.sum(block_start[:, None] >= ends[None, :], axis=1), N_EXPERTS - 1).astype(jnp.int32)
    block_valid = jnp.clip((pstart + counts)[block_e] - block_start, 0, MOE_BLOCK).astype(jnp.int32)
    xb = sc_scatter_rows(u, dest_km, n_slots)
    hid = moe_up(block_e, block_valid, xb, w_gate, w_up)
    yb = moe_down(block_e, block_valid, hid, w_down)
    yk = sc_gather_rows(yb, dest_km.reshape(-1))
    return final_combine(h, yk, r, final_norm_w)


def kernel(x, mix_norm_w, w_in, gdn_conv_w, gdn_A_log, gdn_dt_bias, gdn_norm_w, gdn_proj, ssm_conv_w, ssm_conv_b, ssm_A_log, ssm_dt_bias, ssm_D, ssm_norm_w, ssm_proj, w_out, ffn_norm_w, dense_w_gate, dense_w_up, dense_w_down, router_w, moe_w_gate, moe_w_up, moe_w_down, final_norm_w):
    bsz, seq, d = x.shape
    assert d == D_MODEL and w_in.shape[0] == 2, "dense-FFN layer followed by a final MoE layer"
    h = x.reshape(bsz * seq, d)
    u = rmsnorm(h, mix_norm_w[0], BF16)
    w_in_bf = w_in.astype(BF16)

    def mixer(layer, h, u, packed_u, router_w=None):
        return hybrid_mixer_layer(
            h, u, bsz, seq, w_in[layer], w_in_bf[layer], gdn_conv_w[layer], gdn_A_log[layer], gdn_dt_bias[layer],
            gdn_norm_w[layer], gdn_proj[layer], ssm_conv_w[layer], ssm_conv_b[layer], ssm_A_log[layer],
            ssm_dt_bias[layer], ssm_D[layer], ssm_norm_w[layer], ssm_proj[layer], w_out[layer],
            ffn_norm_w[layer], packed_u, router_w)

    h, u = mixer(0, h, u, False)
    h, u = dense_ffn(h, u, dense_w_gate[0].astype(BF16), dense_w_up[0].astype(BF16),
                     dense_w_down[0].astype(BF16), mix_norm_w[1])
    h, u, r = mixer(1, h, u, True, router_w[0])
    out = moe_layer(h, u, r, moe_w_gate[0], moe_w_up[0], moe_w_down[0], final_norm_w)
    return out.reshape(bsz, seq, d)
```

```python
import functools

import jax
import jax.numpy as jnp
from jax import lax
from jax.experimental import pallas as pl
from jax.experimental.pallas import tpu as pltpu
from jax.experimental.pallas import tpu_sc as plsc

F32 = jnp.float32
BF16 = jnp.bfloat16

D_MODEL = 1024
CONV_K = 4
CHUNK = 64
GDN_QK_HEADS = 4
GDN_V_HEADS = 8
GDN_HEAD = 128
GDN_QK_DIM = GDN_QK_HEADS * GDN_HEAD
GDN_V_DIM = GDN_V_HEADS * GDN_HEAD
GDN_CONV_DIM = 2 * GDN_QK_DIM + GDN_V_DIM
SSM_D_INNER = 2048
SSM_HEAD_DIM = 64
SSM_HEADS = SSM_D_INNER // SSM_HEAD_DIM
SSM_GROUPS = 4
SSM_HPG = SSM_HEADS // SSM_GROUPS
SSM_GROUP_DIM = SSM_D_INNER // SSM_GROUPS
SSM_STATE = 128
SSM_CONV_DIM = SSM_D_INNER + 2 * SSM_GROUPS * SSM_STATE
N_EXPERTS = 8
TOP_K = 2
MOE_BLOCK = 512
LOG2E = 1.4426950408889634
NORM_EPS = 1e-6
SSM_NORM_EPS = 1e-5
CONV_HALO = 16

VMEM_LIMIT = 56 * 1024 * 1024
CONV_ROW_TILES = 4
PIPE_COLS = 512
SC_CORES = 2
SC_SUBCORES = 16
SC_GATHER_CHUNK = 128


def _cparams(sem):
    return pltpu.CompilerParams(dimension_semantics=sem, vmem_limit_bytes=VMEM_LIMIT)


def _silu(x):
    h = 0.5 * x
    return h + h * jnp.tanh(h)


def _softplus(x):
    return jnp.maximum(x, 0.0) + jnp.log1p(jnp.exp(-jnp.abs(x)))


def _dot(a, b):
    return jnp.dot(a.astype(BF16), b.astype(BF16), preferred_element_type=F32)


def _dot_nt(a, b):
    return lax.dot_general(a.astype(BF16), b.astype(BF16), (((1,), (1,)), ((), ())),
                           preferred_element_type=F32)


def _dot_tn(a, b):
    return lax.dot_general(a.astype(BF16), b.astype(BF16), (((0,), (0,)), ((), ())),
                           preferred_element_type=F32)


def _dot_hi(a, b):
    return jnp.dot(a, b, preferred_element_type=F32, precision=lax.Precision.HIGHEST)


def _pack_pairs(x):
    half = x.shape[1] // 2
    bits = lax.bitcast_convert_type(x.astype(BF16).astype(F32), jnp.uint32)
    return (bits[:, :half] >> 16) | (bits[:, half:] & jnp.uint32(0xFFFF0000))


def _unpack_pairs(p):
    lo = lax.bitcast_convert_type(p << 16, F32)
    hi = lax.bitcast_convert_type(p & jnp.uint32(0xFFFF0000), F32)
    return jnp.concatenate([lo, hi], axis=1)


def _rmsnorm_kernel(x_ref, w_ref, o_ref):
    x = x_ref[...]
    y = x * lax.rsqrt(jnp.mean(x * x, axis=-1, keepdims=True) + NORM_EPS)
    o_ref[...] = (y * w_ref[...]).astype(o_ref.dtype)


def rmsnorm(x, w, out_dtype, tm=1024):
    n, d = x.shape
    tm = min(tm, n)
    return pl.pallas_call(
        _rmsnorm_kernel,
        grid=(n // tm,),
        in_specs=[pl.BlockSpec((tm, d), lambda i: (i, 0)), pl.BlockSpec((1, d), lambda i: (0, 0))],
        out_specs=pl.BlockSpec((tm, d), lambda i: (i, 0)),
        out_shape=jax.ShapeDtypeStruct((n, d), out_dtype),
        compiler_params=_cparams(("parallel",)),
        name="rmsnorm",
    )(x, w.reshape(1, d))


def _split3(w):
    w_hi = w.astype(BF16)
    w_mid = (w - w_hi.astype(F32)).astype(BF16)
    w_lo = (w - w_hi.astype(F32) - w_mid.astype(F32)).astype(BF16)
    return jnp.concatenate([w_hi, w_mid, w_lo], axis=1)


def _dot_parts(x, w_parts, parts):
    r = jnp.dot(x, w_parts, preferred_element_type=F32)
    tn = w_parts.shape[1] // parts
    acc = r[:, :tn]
    for p in range(1, parts):
        acc = acc + r[:, p * tn:(p + 1) * tn]
    return acc


def _matmul_kernel(x_ref, w_ref, o_ref, *, parts):
    o_ref[...] = _dot_parts(x_ref[...], w_ref[...], parts).astype(o_ref.dtype)


def matmul(x, w, out_dtype, tm=1024, tn=1024, full_precision=False, name="matmul"):
    n, k = x.shape
    m = w.shape[1]
    tm = min(tm, n)
    tn = min(tn, m)
    parts = 1
    if full_precision:
        assert m == tn
        w = _split3(w)
        parts = 3
    return pl.pallas_call(
        functools.partial(_matmul_kernel, parts=parts),
        grid=(m // tn, n // tm),
        in_specs=[pl.BlockSpec((tm, k), lambda j, i: (i, 0)),
                  pl.BlockSpec((k, parts * tn), lambda j, i: (0, j))],
        out_specs=pl.BlockSpec((tm, tn), lambda j, i: (i, j)),
        out_shape=jax.ShapeDtypeStruct((n, m), out_dtype),
        compiler_params=_cparams(("parallel", "parallel")),
        name=name,
    )(x, w)


def _pipelined(n, produce, consume):
    cur = produce(0)
    for i in range(n):
        nxt = produce(i + 1) if i + 1 < n else None
        consume(i, cur)
        cur = nxt


def _sigmoid(x):
    return 0.5 + 0.5 * jnp.tanh(0.5 * x)


def _proj_act_kernel(x_ref, w_ref, o_ref, *, act):
    x = x_ref[...]
    cols = lambda c: slice(c * PIPE_COLS, (c + 1) * PIPE_COLS)

    def consume(c, raw):
        o_ref[:, cols(c)] = act(raw).astype(o_ref.dtype)

    _pipelined(o_ref.shape[1] // PIPE_COLS,
               lambda c: jnp.dot(x, w_ref[:, cols(c)], preferred_element_type=F32), consume)


def _proj_conv_silu_kernel(x_ref, w_ref, cw_ref, cb_ref, o_ref, carry_ref, *, tiles_per_seq):
    x = x_ref[...]
    first = pl.program_id(1) % tiles_per_seq == 0
    cols = lambda c: slice(c * PIPE_COLS, (c + 1) * PIPE_COLS)

    def roll2(v):
        words = pltpu.roll(pltpu.bitcast(v, jnp.uint32), 1, 0)
        return pltpu.bitcast(words, v.dtype)

    def consume(c, raw):
        prev = jnp.where(first, 0.0, carry_ref[:, cols(c)])
        carry_ref[:, cols(c)] = raw[-CONV_HALO:]
        w = cw_ref[:, cols(c)].astype(BF16)
        bias = cb_ref[:, cols(c)].astype(BF16)
        rows_per = raw.shape[0] // CONV_ROW_TILES
        for r in range(CONV_ROW_TILES):
            lo = r * rows_per
            x = jnp.concatenate([prev if r == 0 else raw[lo - CONV_HALO:lo], raw[lo:lo + rows_per]], axis=0)
            xb, x1b = x.astype(BF16), pltpu.roll(x, 1, 0).astype(BF16)
            near = xb * w[3:4] + x1b * w[2:3] + bias
            far = xb * w[1:2] + x1b * w[0:1]
            h = (near + roll2(far))[CONV_HALO:]
            o_ref[lo:lo + rows_per, cols(c)] = (h + h * jnp.tanh(h)).astype(o_ref.dtype)

    _pipelined(o_ref.shape[1] // PIPE_COLS,
               lambda c: jnp.dot(x, w_ref[:, cols(c)], preferred_element_type=F32), consume)


def proj_act(x, w, conv_w, conv_b, seq, tm=1024, tn=None, act=_silu, name="proj_act"):
    n, k = x.shape
    m = w.shape[1]
    tm = min(tm, seq)
    tn = m if tn is None else tn
    assert seq % tm == 0 and n % seq == 0 and m % tn == 0
    xw_specs = [pl.BlockSpec((tm, k), lambda j, i: (i, 0)), pl.BlockSpec((k, tn), lambda j, i: (0, j))]
    common = dict(
        grid=(m // tn, n // tm),
        out_specs=pl.BlockSpec((tm, tn), lambda j, i: (i, j)),
        out_shape=jax.ShapeDtypeStruct((n, m), BF16),
        name=name,
    )
    if conv_w is None:
        return pl.pallas_call(functools.partial(_proj_act_kernel, act=act), in_specs=xw_specs,
                              compiler_params=_cparams(("parallel", "parallel")), **common)(x, w)
    return pl.pallas_call(
        functools.partial(_proj_conv_silu_kernel, tiles_per_seq=seq // tm),
        in_specs=xw_specs + [pl.BlockSpec((CONV_K, tn), lambda j, i: (0, j)),
                             pl.BlockSpec((1, tn), lambda j, i: (0, j))],
        scratch_shapes=[pltpu.VMEM((CONV_HALO, tn), F32)],
        compiler_params=_cparams(("parallel", "arbitrary")),
        **common,
    )(x, w, 0.5 * conv_w, 0.5 * conv_b.reshape(1, m))


def _col_from_row(row, eye):
    return jnp.sum(jnp.where(eye, row, 0.0), axis=-1, keepdims=True)


def _gdn_kernel(alog_ref, dtb_ref,
                q_ref, k_ref, v_ref, z_ref, a_ref, b_ref, nw_ref,
                o_ref,
                s_ref, *, tt, hpb):
    hblk = pl.program_id(1)
    nck = tt // CHUNK
    rep = GDN_V_HEADS // GDN_QK_HEADS
    dh = GDN_HEAD

    @pl.when(pl.program_id(2) == 0)
    def _():
        s_ref[...] = jnp.zeros_like(s_ref)

    ri = lax.broadcasted_iota(jnp.int32, (CHUNK, CHUNK), 0)
    ci = lax.broadcasted_iota(jnp.int32, (CHUNK, CHUNK), 1)
    upper = (ri <= ci).astype(F32)
    eye = ri == ci
    causal = ri >= ci
    strict = ri > ci
    chunks = [slice(c * CHUNK, (c + 1) * CHUNK) for c in range(nck)]
    nsteps = CHUNK.bit_length() - 1

    heads = range(hpb)
    g_rows, beta_rows, gc_rows = [], [], []
    for hh in heads:
        head = hblk * hpb + hh
        neg_a = -LOG2E * jnp.exp(jnp.full((1, CHUNK), alog_ref[head], F32))
        g_rows.append(neg_a * _softplus(a_ref[hh] + dtb_ref[head]))
        beta_rows.append(jax.nn.sigmoid(b_ref[hh]))
        gc_rows.append(_dot_hi(g_rows[hh], upper))

    ps, xs, qkk, qes, egl = {}, {}, {}, {}, {}

    def local_prep(c):
        sl = chunks[c]
        qs, ks = [], []
        for j in range(hpb // rep):
            q = q_ref[sl, j * dh:(j + 1) * dh].astype(F32)
            k = k_ref[sl, j * dh:(j + 1) * dh].astype(F32)
            qs.append(q * (lax.rsqrt(jnp.sum(q * q, axis=-1, keepdims=True) + 1e-6) * (dh ** -0.5)))
            ks.append(k * lax.rsqrt(jnp.sum(k * k, axis=-1, keepdims=True) + 1e-6))
        kq = [_dot_nt(jnp.concatenate([ks[j], qs[j]], axis=0), ks[j]) for j in range(hpb // rep)]
        for hh in heads:
            qc, kc = qs[hh // rep], ks[hh // rep]
            vc = v_ref[sl, hh * dh:(hh + 1) * dh].astype(F32)
            gc_row = gc_rows[hh][c:c + 1, :]
            gc_col = _col_from_row(gc_row, eye)
            beta_col = _col_from_row(beta_rows[hh][c:c + 1, :], eye)
            g_last = jnp.sum(g_rows[hh][c:c + 1, :], axis=-1, keepdims=True)
            decay = jnp.where(causal, jnp.exp2(gc_col - gc_row), 0.0)
            eg_col = jnp.exp2(gc_col)
            kk, qk = kq[hh // rep][:CHUNK], kq[hh // rep][CHUNK:]
            ps[hh, c] = jnp.where(strict, kk * decay, 0.0) * beta_col
            xs[hh, c] = jnp.concatenate([vc * beta_col, kc * (beta_col * eg_col)], axis=1)
            k_dec = kc * jnp.exp2(g_last - gc_col)
            qkk[hh, c] = jnp.concatenate([qk * decay, k_dec.T], axis=0)
            qes[hh, c] = qc * eg_col
            egl[hh, c] = jnp.exp2(g_last)

    def solve_step(c, step):
        for hh in heads:
            u = (hh, c)
            if step + 1 < nsteps:
                r = _dot(ps[u], jnp.concatenate([xs[u], ps[u]], axis=1))
                ps[u] = r[:, 2 * dh:]
                r = r[:, :2 * dh]
            else:
                r = _dot(ps[u], xs[u])
            xs[u] = xs[u] - r if step == 0 else xs[u] + r

    s = [s_ref[hh] for hh in heads]
    v_new, o_part = {}, {}

    def rec_a(c):
        for hh in heads:
            x = xs[hh, c]
            wq_s = _dot(jnp.concatenate([x[:, dh:], qes[hh, c]], axis=0), s[hh])
            v_new[hh] = x[:, :dh] - wq_s[:CHUNK]
            o_part[hh] = wq_s[CHUNK:]

    def rec_b(c):
        for hh in heads:
            r = _dot(qkk[hh, c], v_new[hh])
            s[hh] = s[hh] * egl[hh, c] + r[CHUNK:]
            o = o_part[hh] + r[:CHUNK]
            y = o * lax.rsqrt(jnp.mean(o * o, axis=-1, keepdims=True) + NORM_EPS) * nw_ref[...]
            cols = slice(hh * dh, (hh + 1) * dh)
            o_ref[chunks[c], cols] = (y * z_ref[chunks[c], cols].astype(F32)).astype(o_ref.dtype)

    half = nsteps // 2
    local_prep(0)
    for step in range(nsteps):
        solve_step(0, step)
    for c in range(nck):
        nxt = c + 1 < nck
        if nxt:
            local_prep(c + 1)
        rec_a(c)
        if nxt:
            for step in range(half):
                solve_step(c + 1, step)
        rec_b(c)
        if nxt:
            for step in range(half, nsteps):
                solve_step(c + 1, step)
    for hh in range(hpb):
        s_ref[hh] = s[hh]


def gdn_mixer(qkv, z_a, a4, b4, a_log, dt_bias, norm_w, *, bsz, seq, tt=512, hpb=GDN_V_HEADS):
    tt = min(tt, seq)
    n = bsz * seq
    nt = seq // tt
    rep = GDN_V_HEADS // GDN_QK_HEADS
    nhb = GDN_V_HEADS // hpb
    qw = hpb // rep * GDN_HEAD
    vw = hpb * GDN_HEAD
    q0, k0, v0 = 0, GDN_QK_DIM // qw, 2 * GDN_QK_DIM // vw

    def cur(width, c0):
        return pl.BlockSpec((tt, width), lambda b, h, t, *_: (b * nt + t, c0 + h))

    small = pl.BlockSpec((None, hpb, None, tt // CHUNK, CHUNK), lambda b, h, t, *_: (b, h, t, 0, 0))
    a4 = a4.reshape(bsz, GDN_V_HEADS, nt, tt // CHUNK, CHUNK)
    b4 = b4.reshape(bsz, GDN_V_HEADS, nt, tt // CHUNK, CHUNK)
    grid_spec = pltpu.PrefetchScalarGridSpec(
        num_scalar_prefetch=2,
        grid=(bsz, nhb, nt),
        in_specs=[cur(qw, q0), cur(qw, k0), cur(vw, v0), cur(vw, 0), small, small,
                  pl.BlockSpec((1, GDN_HEAD), lambda b, h, t, *_: (0, 0))],
        out_specs=cur(vw, 0),
        scratch_shapes=[pltpu.VMEM((hpb, GDN_HEAD, GDN_HEAD), F32)],
    )
    return pl.pallas_call(
        functools.partial(_gdn_kernel, tt=tt, hpb=hpb),
        grid_spec=grid_spec,
        out_shape=jax.ShapeDtypeStruct((n, GDN_V_DIM), BF16),
        compiler_params=_cparams(("parallel", "parallel", "arbitrary")),
        name="gdn_mixer",
    )(a_log, dt_bias, qkv, qkv, qkv, z_a, a4, b4, norm_w.reshape(1, GDN_HEAD))


def _ssd_kernel(x_ref, b_ref, c_ref, z_ref, dt_ref, alog_ref, dtb_ref, dskip_ref, nw_ref,
                o_ref,
                h_ref, *, tt):
    nck = tt // CHUNK
    pw = 2 * SSM_HEAD_DIM
    npair = SSM_GROUP_DIM // pw

    @pl.when(pl.program_id(2) == 0)
    def _():
        h_ref[...] = jnp.zeros_like(h_ref)

    dt_rows = _softplus(dt_ref[...] + dtb_ref[...])
    adt_rows = (-LOG2E * jnp.exp(alog_ref[...])) * dt_rows
    r2 = lax.broadcasted_iota(jnp.int32, (pw, pw), 0)
    c2 = lax.broadcasted_iota(jnp.int32, (pw, pw), 1)
    same_head = (r2 // CHUNK) == (c2 // CHUNK)
    cum_tot = jnp.concatenate([(same_head & (r2 <= c2)).astype(F32), same_head.astype(F32)], axis=1)
    adt_pc = jnp.concatenate([adt_rows[:, p * pw:(p + 1) * pw] for p in range(npair)], axis=0)
    ct = _dot_hi(adt_pc, cum_tot)
    acs_pc, tot_pc = ct[:, :pw], ct[:, pw:]

    li = lax.broadcasted_iota(jnp.int32, (CHUNK, pw), 0)
    ji = lax.broadcasted_iota(jnp.int32, (CHUNK, pw), 1)
    lo_half = ji < CHUNK
    pick_a = ji == li
    pick_b = ji == li + CHUNK
    causal2 = li >= jnp.where(lo_half, ji, ji - CHUNK)
    rb = lax.broadcasted_iota(jnp.int32, (pw, pw), 0)
    cb_ = lax.broadcasted_iota(jnp.int32, (pw, pw), 1)
    blockdiag = (rb < CHUNK) == (cb_ < CHUNK)

    def pair_col(row):
        a = jnp.sum(jnp.where(pick_a, row, 0.0), axis=-1, keepdims=True)
        b = jnp.sum(jnp.where(pick_b, row, 0.0), axis=-1, keepdims=True)
        return jnp.where(lo_half, a, b)

    hstate = h_ref[...]
    for c in range(nck):
        sl = slice(c * CHUNK, (c + 1) * CHUNK)
        xs = x_ref[sl, :].astype(F32)
        cmc, bmc = c_ref[sl, :], b_ref[sl, :]
        cb2 = _dot_nt(cmc, jnp.concatenate([bmc, bmc], axis=0))
        yd_parts, eacs_parts, xdec_parts, eal_parts = [], [], [], []
        for p in range(npair):
            row = p * nck + c
            acs_row = acs_pc[row:row + 1, :]
            tot_row = tot_pc[row:row + 1, :]
            acs_col = pair_col(acs_row)
            dt_col = pair_col(dt_rows[c:c + 1, p * pw:(p + 1) * pw])
            lmat = jnp.where(causal2, jnp.exp2(acs_col - acs_row), 0.0)
            xdt = xs[:, p * pw:(p + 1) * pw] * dt_col
            xdt2 = jnp.where(blockdiag, jnp.concatenate([xdt, xdt], axis=0), 0.0)
            yd_parts.append(_dot(cb2 * lmat, xdt2))
            eacs_parts.append(jnp.exp2(acs_col))
            xdec_parts.append(xdt * jnp.exp2(tot_row - acs_col))
            eal_parts.append(jnp.exp2(tot_row))
        upd = _dot_tn(bmc, jnp.concatenate(xdec_parts, axis=1))
        y_off = _dot(cmc, hstate)
        y = jnp.concatenate(yd_parts, axis=1) + y_off * jnp.concatenate(eacs_parts, axis=1) + xs * dskip_ref[...]
        y = y * z_ref[sl, :].astype(F32)
        y = y * lax.rsqrt(jnp.mean(y * y, axis=-1, keepdims=True) + SSM_NORM_EPS) * nw_ref[...]
        o_ref[sl, :] = y.astype(o_ref.dtype)
        hstate = hstate * jnp.concatenate(eal_parts, axis=1) + upd
    h_ref[...] = hstate


def ssd_mixer(xbc, z_b, dt_raw, a_log, dt_bias, d_skip, norm_w, *, bsz, seq, tt=2048):
    tt = min(tt, seq)
    n = bsz * seq
    nt = seq // tt
    nck = tt // CHUNK
    gd = SSM_GROUP_DIM
    st = SSM_STATE
    x_blocks = SSM_D_INNER // st
    dtl = jnp.transpose(dt_raw.reshape(bsz, seq // CHUNK, CHUNK, SSM_HEADS), (0, 1, 3, 2))
    dtl = dtl.reshape(bsz, nt, nck, SSM_D_INNER)
    per_head = lambda v: jnp.repeat(v, SSM_HEAD_DIM).reshape(1, SSM_D_INNER)

    def cur(width, colf):
        return pl.BlockSpec((tt, width), lambda b, g, t, *_: (b * nt + t, colf(g)))

    def rowspec(rows, width, colf):
        return pl.BlockSpec((rows, width), lambda b, g, t, *_: (0, colf(g)))

    xcol = lambda g: g
    bcol = lambda g: x_blocks + g
    ccol = lambda g: x_blocks + SSM_GROUPS + g
    return pl.pallas_call(
        functools.partial(_ssd_kernel, tt=tt),
        grid=(bsz, SSM_GROUPS, nt),
        in_specs=[cur(gd, xcol), cur(st, bcol), cur(st, ccol), cur(gd, xcol),
                  pl.BlockSpec((None, None, nck, gd), lambda b, g, t: (b, t, 0, g)),
                  rowspec(1, gd, xcol), rowspec(1, gd, xcol), rowspec(1, gd, xcol), rowspec(1, gd, xcol)],
        out_specs=cur(gd, xcol),
        out_shape=jax.ShapeDtypeStruct((n, SSM_D_INNER), BF16),
        scratch_shapes=[pltpu.VMEM((SSM_STATE, gd), F32)],
        compiler_params=_cparams(("parallel", "parallel", "arbitrary")),
        name="ssd_mixer",
    )(xbc, xbc, xbc, z_b, dtl, per_head(a_log), per_head(dt_bias), per_head(d_skip),
      norm_w.reshape(1, SSM_D_INNER))


def _top2_route(logits):
    lane = lax.broadcasted_iota(jnp.int32, logits.shape, 1)
    neg = jnp.float32(-3.0e38)
    logits = jnp.where(lane < N_EXPERTS, logits, neg)
    m1 = jnp.max(logits, axis=-1, keepdims=True)
    i1 = jnp.min(jnp.where(logits == m1, lane, 2 * N_EXPERTS), axis=-1, keepdims=True)
    rest = jnp.where(lane == i1, neg, logits)
    m2 = jnp.max(rest, axis=-1, keepdims=True)
    i2 = jnp.min(jnp.where(rest == m2, lane, 2 * N_EXPERTS), axis=-1, keepdims=True)
    e2 = jnp.exp(m2 - m1)
    g1 = 1.0 / (1.0 + e2)
    g2 = e2 / (1.0 + e2)
    out = jnp.where(lane == 0, i1.astype(F32), 0.0)
    out = jnp.where(lane == 1, i2.astype(F32), out)
    out = jnp.where(lane == 2, g1, out)
    return jnp.where(lane == 3, g2, out)


def _merge_kernel(h_ref, oa_ref, ob_ref, ga_ref, gb_ref, wa_ref, wb_ref, wo_ref, nw_ref, *rest):
    rw_ref, hn_ref, u_ref, r_ref = rest if len(rest) == 4 else (None,) + rest + (None,)
    half = h_ref.shape[0] // 2
    rows = lambda r: slice(r * half, (r + 1) * half)

    def produce(r):
        return (jnp.dot(oa_ref[rows(r), :], wa_ref[...], preferred_element_type=F32),
                jnp.dot(ob_ref[rows(r), :], wb_ref[...], preferred_element_type=F32))

    def consume(r, ab):
        mixed = ga_ref[rows(r), :].astype(F32) * ab[0] + gb_ref[rows(r), :].astype(F32) * ab[1]
        hn = h_ref[rows(r), :] + jnp.dot(mixed.astype(BF16), wo_ref[...], preferred_element_type=F32)
        hn_ref[rows(r), :] = hn
        y = hn * lax.rsqrt(jnp.mean(hn * hn, axis=-1, keepdims=True) + NORM_EPS) * nw_ref[...]
        u_ref[rows(r), :] = _pack_pairs(y) if u_ref.dtype == jnp.uint32 else y.astype(u_ref.dtype)
        if rw_ref is not None:
            r_ref[rows(r), :] = _top2_route(_dot_parts(y.astype(BF16), rw_ref[...], 3))

    _pipelined(2, produce, consume)


def merge_out(h, oa, ob, gates, wa, wb, wo, next_norm_w, packed_u, router_w=None, tm=512):
    n, d = h.shape
    u_sds = jax.ShapeDtypeStruct((n, d // 2), jnp.uint32) if packed_u else jax.ShapeDtypeStruct((n, d), BF16)
    tm = min(tm, n)
    row = lambda width: pl.BlockSpec((tm, width), lambda i: (i, 0))
    full = lambda r, c: pl.BlockSpec((r, c), lambda i: (0, 0))
    in_specs = [row(d), row(GDN_V_DIM), row(SSM_D_INNER), row(d), pl.BlockSpec((tm, d), lambda i: (i, 1)),
                full(GDN_V_DIM, d), full(SSM_D_INNER, d), full(d, d), full(1, d)]
    args = [h, oa, ob, gates, gates, wa, wb, wo, next_norm_w.reshape(1, d)]
    out_specs = [row(d), row(u_sds.shape[1])]
    out_shape = [jax.ShapeDtypeStruct((n, d), F32), u_sds]
    if router_w is not None:
        in_specs.append(full(d, 3 * 128))
        args.append(_split3(jnp.pad(router_w, ((0, 0), (0, 128 - N_EXPERTS)))))
        out_specs.append(row(128))
        out_shape.append(jax.ShapeDtypeStruct((n, 128), F32))
    return pl.pallas_call(
        _merge_kernel,
        grid=(n // tm,),
        in_specs=in_specs,
        out_specs=out_specs,
        out_shape=out_shape,
        compiler_params=_cparams(("parallel",)),
        name="merge_out",
    )(*args)


def _ffn_kernel(h_ref, u_ref, wg_ref, wu_ref, wd_ref, nw_ref, hn_ref, un_ref, acc_ref, *, tf):
    u = u_ref[...]
    cols = lambda f: slice(f * tf, (f + 1) * tf)
    acc_ref[...] = h_ref[...]

    def produce(f):
        return (jnp.dot(u, wg_ref[:, cols(f)], preferred_element_type=F32),
                jnp.dot(u, wu_ref[:, cols(f)], preferred_element_type=F32))

    def consume(f, gu):
        hid = (_silu(gu[0]) * gu[1]).astype(BF16)
        acc_ref[...] += jnp.dot(hid, wd_ref[cols(f), :], preferred_element_type=F32)

    _pipelined(wg_ref.shape[1] // tf, produce, consume)
    hn = acc_ref[...]
    hn_ref[...] = hn
    y = hn * lax.rsqrt(jnp.mean(hn * hn, axis=-1, keepdims=True) + NORM_EPS)
    un_ref[...] = (y * nw_ref[...]).astype(un_ref.dtype)


def dense_ffn(h, u, wg, wu, wd, next_norm_w, tm=1024, tf=256):
    n, d = h.shape
    ff = wg.shape[1]
    tm = min(tm, n)
    assert ff % tf == 0
    row = pl.BlockSpec((tm, d), lambda i: (i, 0))
    full = lambda r, c: pl.BlockSpec((r, c), lambda i: (0, 0))
    return pl.pallas_call(
        functools.partial(_ffn_kernel, tf=tf),
        grid=(n // tm,),
        in_specs=[row, row, full(d, ff), full(d, ff), full(ff, d), full(1, d)],
        out_specs=[row, row],
        out_shape=[jax.ShapeDtypeStruct((n, d), F32), jax.ShapeDtypeStruct((n, d), BF16)],
        scratch_shapes=[pltpu.VMEM((tm, d), F32)],
        compiler_params=_cparams(("parallel",)),
        name="dense_ffn",
    )(h, u, wg, wu, wd, next_norm_w.reshape(1, d))


def sc_gather_rows(table, idx, chunk=SC_GATHER_CHUNK):
    v, d = table.shape
    b = idx.shape[0]
    nw = SC_CORES * SC_SUBCORES
    per_w = b // nw
    assert per_w * nw == b and per_w % chunk == 0 and chunk % 8 == 0 and chunk <= 128
    mesh = plsc.VectorSubcoreMesh(core_axis_name="c", subcore_axis_name="s")

    @functools.partial(
        pl.kernel, mesh=mesh,
        out_type=jax.ShapeDtypeStruct((b, d), table.dtype),
        scratch_types=[pltpu.VMEM((chunk,), jnp.int32), pltpu.VMEM((chunk, d), table.dtype),
                       pltpu.SemaphoreType.DMA],
    )
    def gather_kernel(table_hbm, idx_hbm, out_hbm, idx_v, rows_v, sem):
        wid = lax.axis_index("s") * SC_CORES + lax.axis_index("c")
        base = wid * per_w

        @pl.loop(0, per_w // chunk)
        def _(j):
            off = pl.multiple_of(base + j * chunk, 8)
            pltpu.sync_copy(idx_hbm.at[pl.ds(off, chunk)], idx_v)
            pltpu.async_copy(table_hbm.at[idx_v], rows_v, sem).wait()
            pltpu.sync_copy(rows_v, out_hbm.at[pl.ds(off, chunk)])

    return gather_kernel(table, idx)


def sc_scatter_rows(src, dest, n_out, chunk=SC_GATHER_CHUNK):
    n, d = src.shape
    nk = dest.shape[0]
    nw = SC_CORES * SC_SUBCORES
    per_w = n // nw
    assert per_w * nw == n and per_w % chunk == 0 and chunk % 8 == 0 and chunk <= 128
    mesh = plsc.VectorSubcoreMesh(core_axis_name="c", subcore_axis_name="s")

    @functools.partial(
        pl.kernel, mesh=mesh,
        out_type=jax.ShapeDtypeStruct((n_out, d), src.dtype),
        scratch_types=[pltpu.VMEM((nk, chunk), jnp.int32), pltpu.VMEM((chunk, d), src.dtype)],
    )
    def scatter_kernel(src_hbm, dest_hbm, out_hbm, idx_v, rows_v):
        wid = lax.axis_index("s") * SC_CORES + lax.axis_index("c")
        base = wid * per_w

        @pl.loop(0, per_w // chunk)
        def _(j):
            off = pl.multiple_of(base + j * chunk, 8)
            pltpu.sync_copy(src_hbm.at[pl.ds(off, chunk)], rows_v)
            for k in range(nk):
                pltpu.sync_copy(dest_hbm.at[k, pl.ds(off, chunk)], idx_v.at[k])
            for k in range(nk):
                pltpu.sync_copy(rows_v, out_hbm.at[idx_v.at[k]])

    return scatter_kernel(src, dest)


def _expert_changed(be_ref, i):
    return jnp.logical_or(i == 0, be_ref[i] != be_ref[jnp.maximum(i - 1, 0)])


def _moe_up_kernel(be_ref, nv_ref, x_ref, wg_ref, wu_ref, o_ref, wg_bf, wu_bf):
    i = pl.program_id(1)

    @pl.when(_expert_changed(be_ref, i))
    def _():
        wg_bf[...] = wg_ref[...].astype(BF16)
        wu_bf[...] = wu_ref[...].astype(BF16)

    @pl.when(nv_ref[i] > 0)
    def _():
        xp = x_ref[...]
        xp = jnp.where(lax.broadcasted_iota(jnp.int32, xp.shape, 0) < nv_ref[i], xp, jnp.uint32(0))
        x = _unpack_pairs(xp).astype(BF16)
        half = PIPE_COLS // 2
        cols = lambda c: slice(c * half, (c + 1) * half)

        def produce(c):
            return (jnp.dot(x, wg_bf[:, cols(c)], preferred_element_type=F32),
                    jnp.dot(x, wu_bf[:, cols(c)], preferred_element_type=F32))

        def consume(c, gu):
            o_ref[:, cols(c)] = (_silu(gu[0]) * gu[1]).astype(o_ref.dtype)

        _pipelined(o_ref.shape[1] // half, produce, consume)


def moe_up(block_e, block_valid, xb, wg, wu, tf=1792):
    ns = xb.shape[0]
    d, ff = wg.shape[1], wg.shape[2]
    nb = ns // MOE_BLOCK
    grid_spec = pltpu.PrefetchScalarGridSpec(
        num_scalar_prefetch=2,
        grid=(ff // tf, nb),
        in_specs=[pl.BlockSpec((MOE_BLOCK, d // 2), lambda f, i, be, nv: (i, 0)),
                  pl.BlockSpec((None, d, tf), lambda f, i, be, nv: (be[i], 0, f)),
                  pl.BlockSpec((None, d, tf), lambda f, i, be, nv: (be[i], 0, f))],
        out_specs=pl.BlockSpec((MOE_BLOCK, tf), lambda f, i, be, nv: (i, f)),
        scratch_shapes=[pltpu.VMEM((d, tf), BF16), pltpu.VMEM((d, tf), BF16)],
    )
    return pl.pallas_call(
        _moe_up_kernel,
        grid_spec=grid_spec,
        out_shape=jax.ShapeDtypeStruct((ns, ff), BF16),
        compiler_params=_cparams(("arbitrary", "arbitrary")),
        name="moe_up",
    )(block_e, block_valid, xb, wg, wu)


def _moe_down_kernel(be_ref, nv_ref, hid_ref, wd_ref, o_ref, wd_bf):
    i = pl.program_id(0)

    @pl.when(_expert_changed(be_ref, i))
    def _():
        wd_bf[...] = wd_ref[...].astype(BF16)

    @pl.when(nv_ref[i] > 0)
    def _():
        o_ref[...] = _pack_pairs(jnp.dot(hid_ref[...], wd_bf[...], preferred_element_type=F32))


def moe_down(block_e, block_valid, hid, wd):
    ns, ff = hid.shape
    d = wd.shape[2]
    nb = ns // MOE_BLOCK
    grid_spec = pltpu.PrefetchScalarGridSpec(
        num_scalar_prefetch=2,
        grid=(nb,),
        in_specs=[pl.BlockSpec((MOE_BLOCK, ff), lambda i, be, nv: (i, 0)),
                  pl.BlockSpec((None, ff, d), lambda i, be, nv: (be[i], 0, 0))],
        out_specs=pl.BlockSpec((MOE_BLOCK, d // 2), lambda i, be, nv: (i, 0)),
        scratch_shapes=[pltpu.VMEM((ff, d), BF16)],
    )
    return pl.pallas_call(
        _moe_down_kernel,
        grid_spec=grid_spec,
        out_shape=jax.ShapeDtypeStruct((ns, d // 2), jnp.uint32),
        compiler_params=_cparams(("arbitrary",)),
        name="moe_down",
    )(block_e, block_valid, hid, wd)


def _final_kernel(h_ref, y0_ref, y1_ref, r_ref, nw_ref, o_ref):
    gates = r_ref[...]
    hn = (h_ref[...] + gates[:, TOP_K:TOP_K + 1] * _unpack_pairs(y0_ref[...])
          + gates[:, TOP_K + 1:TOP_K + 2] * _unpack_pairs(y1_ref[...]))
    y = hn * lax.rsqrt(jnp.mean(hn * hn, axis=-1, keepdims=True) + NORM_EPS)
    o_ref[...] = y * nw_ref[...]


def final_combine(h, yk, r, norm_w, tm=1024):
    n, d = h.shape
    tm = min(tm, n)
    nblk = n // tm
    row = pl.BlockSpec((tm, d), lambda i: (i, 0))
    return pl.pallas_call(
        _final_kernel,
        grid=(nblk,),
        in_specs=[row, pl.BlockSpec((tm, d // 2), lambda i: (i, 0)),
                  pl.BlockSpec((tm, d // 2), lambda i: (i + nblk, 0)),
                  pl.BlockSpec((tm, 128), lambda i: (i, 0)), pl.BlockSpec((1, d), lambda i: (0, 0))],
        out_specs=row,
        out_shape=jax.ShapeDtypeStruct((n, d), F32),
        compiler_params=_cparams(("parallel",)),
        name="final_combine",
    )(h, yk, yk, r, norm_w.reshape(1, d))


def _chunk_rows(x, bsz, seq):
    hh = x.shape[1]
    return jnp.transpose(x.reshape(bsz, seq, hh), (0, 2, 1)).reshape(bsz, hh, seq // CHUNK, CHUNK)


def hybrid_mixer_layer(h, u, bsz, seq, w_in, wb, gdn_conv_w, gdn_A_log, gdn_dt_bias, gdn_norm_w, gdn_proj,
                       ssm_conv_w, ssm_conv_b, ssm_A_log, ssm_dt_bias, ssm_D, ssm_norm_w, ssm_proj, w_out,
                       next_norm_w, packed_u, router_w=None):
    c0 = 0
    c1 = c0 + GDN_CONV_DIM
    c2 = c1 + GDN_V_DIM
    c3 = c2 + GDN_V_HEADS
    c4 = c3 + GDN_V_HEADS
    c5 = c4 + SSM_D_INNER
    c6 = c5 + SSM_CONV_DIM
    c7 = c6 + SSM_HEADS
    c8 = c7 + D_MODEL
    qkv = proj_act(u, wb[:, c0:c1], gdn_conv_w, jnp.zeros((GDN_CONV_DIM,), F32), seq, name="proj_qkv")
    z_a = proj_act(u, wb[:, c1:c2], None, None, seq, name="proj_za")
    z_b = proj_act(u, wb[:, c4:c5], None, None, seq, name="proj_zb")
    xbc = proj_act(u, wb[:, c5:c6], ssm_conv_w, ssm_conv_b, seq, name="proj_xbc")
    gates = proj_act(u, wb[:, c7:], None, None, seq, act=_sigmoid, name="proj_gates")
    n_small = 2 * GDN_V_HEADS + SSM_HEADS
    w_small = jnp.concatenate([w_in[:, c2:c4], w_in[:, c6:c7]], axis=1)
    w_small = jnp.pad(w_small, ((0, 0), (0, 128 - n_small)))
    small = matmul(u, w_small, F32, tn=128, full_precision=True, name="proj_small")
    a4 = _chunk_rows(small[:, :GDN_V_HEADS], bsz, seq)
    b4 = _chunk_rows(small[:, GDN_V_HEADS:2 * GDN_V_HEADS], bsz, seq)
    dt_raw = small[:, 2 * GDN_V_HEADS:n_small]

    oa = gdn_mixer(qkv, z_a, a4, b4, gdn_A_log, gdn_dt_bias, gdn_norm_w, bsz=bsz, seq=seq)
    ob = ssd_mixer(xbc, z_b, dt_raw, ssm_A_log, ssm_dt_bias, ssm_D, ssm_norm_w, bsz=bsz, seq=seq)
    return merge_out(h, oa, ob, gates, gdn_proj.astype(BF16), ssm_proj.astype(BF16),
                     w_out.astype(BF16), next_norm_w, packed_u, router_w)


def moe_layer(h, u, r, w_gate, w_up, w_down, final_norm_w):
    n, d = h.shape
    n_assign = n * TOP_K
    experts = jnp.arange(N_EXPERTS, dtype=jnp.int32)[None, :]
    onehots = [(r[:, k].astype(jnp.int32)[:, None] == experts).astype(jnp.int32) for k in range(TOP_K)]
    per_tok = sum(onehots)
    before = jnp.cumsum(per_tok, axis=0) - per_tok
    counts = jnp.sum(per_tok, axis=0)
    padded = (counts + MOE_BLOCK - 1) // MOE_BLOCK * MOE_BLOCK
    ends = jnp.cumsum(padded)
    pstart = ends - padded
    dest_km = jnp.stack([jnp.sum((before + pstart[None, :]) * oh, axis=1) for oh in onehots]).astype(jnp.int32)
    n_blocks = -(-n_assign // MOE_BLOCK) + N_EXPERTS
    n_slots = n_blocks * MOE_BLOCK
    block_start = jnp.arange(n_blocks, dtype=jnp.int32) * MOE_BLOCK
    block_e = jnp.minimum(jnp.sum(block_start[:, None] >= ends[None, :], axis=1), N_EXPERTS - 1).astype(jnp.int32)
    block_valid = jnp.clip((pstart + counts)[block_e] - block_start, 0, MOE_BLOCK).astype(jnp.int32)
    xb = sc_scatter_rows(u, dest_km, n_slots)
    hid = moe_up(block_e, block_valid, xb, w_gate, w_up)
    yb = moe_down(block_e, block_valid, hid, w_down)
    yk = sc_gather_rows(yb, dest_km.reshape(-1))
    return final_combine(h, yk, r, final_norm_w)


def kernel(x, mix_norm_w, w_in, gdn_conv_w, gdn_A_log, gdn_dt_bias, gdn_norm_w, gdn_proj, ssm_conv_w, ssm_conv_b, ssm_A_log, ssm_dt_bias, ssm_D, ssm_norm_w, ssm_proj, w_out, ffn_norm_w, dense_w_gate, dense_w_up, dense_w_down, router_w, moe_w_gate, moe_w_up, moe_w_down, final_norm_w):
    bsz, seq, d = x.shape
    assert d == D_MODEL and w_in.shape[0] == 2, "dense-FFN layer followed by a final MoE layer"
    h = x.reshape(bsz * seq, d)
    u = rmsnorm(h, mix_norm_w[0], BF16)
    w_in_bf = w_in.astype(BF16)

    def mixer(layer, h, u, packed_u, router_w=None):
        return hybrid_mixer_layer(
            h, u, bsz, seq, w_in[layer], w_in_bf[layer], gdn_conv_w[layer], gdn_A_log[layer], gdn_dt_bias[layer],
            gdn_norm_w[layer], gdn_proj[layer], ssm_conv_w[layer], ssm_conv_b[layer], ssm_A_log[layer],
            ssm_dt_bias[layer], ssm_D[layer], ssm_norm_w[layer], ssm_proj[layer], w_out[layer],
            ffn_norm_w[layer], packed_u, router_w)

    h, u = mixer(0, h, u, False)
    h, u = dense_ffn(h, u, dense_w_gate[0].astype(BF16), dense_w_up[0].astype(BF16),
                     dense_w_down[0].astype(BF16), mix_norm_w[1])
    h, u, r = mixer(1, h, u, True, router_w[0])
    out = moe_layer(h, u, r, moe_w_gate[0], moe_w_up[0], moe_w_down[0], final_norm_w)
    return out.reshape(bsz, seq, d)
```
